```python
import jax, jax.numpy as jnp
from jax import lax
import numpy as np

D_MODEL = 1024
BATCH = 8
SEQ = 2048
DEPTH = 2

HEAD_DIM = 64
N_HEADS_A = D_MODEL // HEAD_DIM
WIDTH_A = N_HEADS_A * HEAD_DIM
N_Q_HEADS_B = D_MODEL // HEAD_DIM
N_KV_HEADS_B = 4
GROUP_B = N_Q_HEADS_B // N_KV_HEADS_B
WIDTH_B = N_Q_HEADS_B * HEAD_DIM
KV_WIDTH_B = N_KV_HEADS_B * HEAD_DIM
WINDOW = 128
Q_BLOCK = 128
ROT_DIM = HEAD_DIM // 4
ROPE_THETA = 500000.0
EPS = 1e-6
N_A_LAYERS = DEPTH // 2
N_B_LAYERS = DEPTH - N_A_LAYERS

kernel_name = "yoco_fox_swa_sink_hybrid"


def rmsnorm(x, g):
    xf = x.astype(jnp.float32)
    y = xf * lax.rsqrt(jnp.mean(xf * xf, axis=-1, keepdims=True) + EPS)
    return (y * g.astype(jnp.float32)).astype(x.dtype)


def partial_rope(x, positions):
    x_rot, x_pass = x[..., :ROT_DIM], x[..., ROT_DIM:]
    half = ROT_DIM // 2
    inv_freq = jnp.power(jnp.float32(ROPE_THETA), -jnp.arange(0, ROT_DIM, 2, dtype=jnp.float32) / ROT_DIM)
    ang = positions.astype(jnp.float32)[:, None] * inv_freq[None, :]
    cos = jnp.cos(ang)[None, :, None, :]
    sin = jnp.sin(ang)[None, :, None, :]
    xr = x_rot.astype(jnp.float32)
    x1, x2 = xr[..., :half], xr[..., half:]
    rot = jnp.concatenate([x1 * cos - x2 * sin, x1 * sin + x2 * cos], axis=-1)
    return jnp.concatenate([rot.astype(x.dtype), x_pass], axis=-1)


def fox_attention(q, k, v, log_f):
    b, s, h, d = q.shape
    nb = s // Q_BLOCK
    scale = HEAD_DIM ** -0.5
    c = jnp.cumsum(log_f, axis=1)
    c_k = jnp.transpose(c, (0, 2, 1))
    q_blocks = jnp.moveaxis(q.reshape(b, nb, Q_BLOCK, h, d), 1, 0)
    c_blocks = jnp.moveaxis(c_k.reshape(b, h, nb, Q_BLOCK), 2, 0)
    k_pos = jnp.arange(s)

    def block(args):
        idx, qi, ci = args
        logits = jnp.einsum('bqhd,bkhd->bhqk', qi, k, preferred_element_type=jnp.float32) * scale
        logits = logits + ci[..., :, None] - c_k[..., None, :]
        q_pos = idx * Q_BLOCK + jnp.arange(Q_BLOCK)
        causal = k_pos[None, :] <= q_pos[:, None]
        logits = jnp.where(causal, logits, -jnp.inf)
        p = jax.nn.softmax(logits, axis=-1)
        return jnp.einsum('bhqk,bkhd->bqhd', p.astype(v.dtype), v)

    out = lax.map(block, (jnp.arange(nb), q_blocks, c_blocks))
    return jnp.moveaxis(out, 0, 1).reshape(b, s, h, d)


def swa_sink_attention(q, k, v, sinks):
    b, s, hq, d = q.shape
    hkv = k.shape[2]
    g = hq // hkv
    nb = s // WINDOW
    scale = HEAD_DIM ** -0.5
    qb = q.reshape(b, nb, WINDOW, hkv, g, d)
    pad = ((0, 0), (WINDOW, 0), (0, 0), (0, 0))
    kb = jnp.pad(k, pad).reshape(b, nb + 1, WINDOW, hkv, d)
    vb = jnp.pad(v, pad).reshape(b, nb + 1, WINDOW, hkv, d)
    k_band = jnp.concatenate([kb[:, :-1], kb[:, 1:]], axis=2)
    v_band = jnp.concatenate([vb[:, :-1], vb[:, 1:]], axis=2)
    logits = jnp.einsum('bnqhgd,bnkhd->bnhgqk', qb, k_band, preferred_element_type=jnp.float32) * scale
    diff = (jnp.arange(WINDOW)[:, None] + WINDOW) - jnp.arange(2 * WINDOW)[None, :]
    in_window = (diff >= 0) & (diff < WINDOW)
    k_abs = jnp.arange(nb)[:, None] * WINDOW + jnp.arange(2 * WINDOW)[None, :] - WINDOW
    valid = in_window[None] & (k_abs >= 0)[:, None, :]
    logits = jnp.where(valid[None, :, None, None], logits, -jnp.inf)
    sink = jnp.broadcast_to(sinks.astype(jnp.float32).reshape(1, 1, hkv, g, 1, 1), logits.shape[:-1] + (1,))
    probs = jax.nn.softmax(jnp.concatenate([logits, sink], axis=-1), axis=-1)[..., :-1]
    out = jnp.einsum('bnhgqk,bnkhd->bnqhgd', probs.astype(v.dtype), v_band)
    return out.reshape(b, s, hq, d)


def setup_inputs(seed: int = 0) -> dict:
    key = jax.random.key(seed)
    ks = jax.random.split(key, 20)
    f32 = jnp.float32
    in_a = 3 * WIDTH_A + N_HEADS_A + WIDTH_A
    in_b = WIDTH_B + WIDTH_B
    return {
        "x": jax.random.normal(ks[0], (BATCH, SEQ, D_MODEL), f32),
        "positions": jnp.arange(SEQ, dtype=jnp.int32),
        "norm_a_g": 1.0 + 0.02 * jax.random.normal(ks[1], (N_A_LAYERS, D_MODEL), f32),
        "w_in_a": jax.random.normal(ks[2], (N_A_LAYERS, D_MODEL, in_a), f32) * D_MODEL ** -0.5,
        "b_forget": 3.0 + 0.1 * jax.random.normal(ks[3], (N_A_LAYERS, N_HEADS_A), f32),
        "qnorm_a_g": 1.0 + 0.02 * jax.random.normal(ks[4], (N_A_LAYERS, HEAD_DIM), f32),
        "knorm_a_g": 1.0 + 0.02 * jax.random.normal(ks[5], (N_A_LAYERS, HEAD_DIM), f32),
        "w_out_a": jax.random.normal(ks[6], (N_A_LAYERS, WIDTH_A, D_MODEL), f32) * WIDTH_A ** -0.5,
        "kv_norm_g": 1.0 + 0.02 * jax.random.normal(ks[7], (D_MODEL,), f32),
        "w_kv": jax.random.normal(ks[8], (D_MODEL, 2 * KV_WIDTH_B), f32) * D_MODEL ** -0.5,
        "knorm_b_g": 1.0 + 0.02 * jax.random.normal(ks[9], (HEAD_DIM,), f32),
        "norm_b_g": 1.0 + 0.02 * jax.random.normal(ks[10], (N_B_LAYERS, D_MODEL), f32),
        "w_in_b": jax.random.normal(ks[11], (N_B_LAYERS, D_MODEL, in_b), f32) * D_MODEL ** -0.5,
        "qnorm_b_g": 1.0 + 0.02 * jax.random.normal(ks[12], (N_B_LAYERS, HEAD_DIM), f32),
        "sinks": 0.5 * jax.random.normal(ks[13], (N_B_LAYERS, N_Q_HEADS_B), f32),
        "w_out_b": jax.random.normal(ks[14], (N_B_LAYERS, WIDTH_B, D_MODEL), f32) * WIDTH_B ** -0.5,
    }


def reference(x, positions, norm_a_g, w_in_a, b_forget, qnorm_a_g, knorm_a_g, w_out_a,
              kv_norm_g, w_kv, knorm_b_g, norm_b_g, w_in_b, qnorm_b_g, sinks, w_out_b):
    b, s, _ = x.shape
    h = x
    k_shared = None
    v_shared = None
    for layer in range(DEPTH):
        if layer < N_A_LAYERS:
            i = layer
            u = rmsnorm(h, norm_a_g[i])
            proj = u @ w_in_a[i]
            q, k, v, f_logit, gate = jnp.split(
                proj, [WIDTH_A, 2 * WIDTH_A, 3 * WIDTH_A, 3 * WIDTH_A + N_HEADS_A], axis=-1)
            q = rmsnorm(q.reshape(b, s, N_HEADS_A, HEAD_DIM), qnorm_a_g[i])
            k = rmsnorm(k.reshape(b, s, N_HEADS_A, HEAD_DIM), knorm_a_g[i])
            v = v.reshape(b, s, N_HEADS_A, HEAD_DIM)
            log_f = jax.nn.log_sigmoid((f_logit + b_forget[i]).astype(jnp.float32))
            o = fox_attention(q, k, v, log_f).reshape(b, s, WIDTH_A)
            h = h + (o * jax.nn.silu(gate)) @ w_out_a[i]
        else:
            if layer == N_A_LAYERS:
                u_kv = rmsnorm(h, kv_norm_g)
                k_s, v_s = jnp.split(u_kv @ w_kv, [KV_WIDTH_B], axis=-1)
                k_shared = partial_rope(rmsnorm(k_s.reshape(b, s, N_KV_HEADS_B, HEAD_DIM), knorm_b_g), positions)
                v_shared = v_s.reshape(b, s, N_KV_HEADS_B, HEAD_DIM)
            j = layer - N_A_LAYERS
            u = rmsnorm(h, norm_b_g[j])
            q, gate = jnp.split(u @ w_in_b[j], [WIDTH_B], axis=-1)
            q = partial_rope(rmsnorm(q.reshape(b, s, N_Q_HEADS_B, HEAD_DIM), qnorm_b_g[j]), positions)
            o = swa_sink_attention(q, k_shared, v_shared, sinks[j]).reshape(b, s, WIDTH_B)
            h = h + (o * jax.nn.silu(gate)) @ w_out_b[j]
    return h
```

```python
import functools

import jax
import jax.numpy as jnp
from jax import lax
from jax.experimental import pallas as pl
from jax.experimental.pallas import tpu as pltpu

D_MODEL = 1024
HEAD_DIM = 64
N_HEADS = 16
N_KV_HEADS = 4
GROUP = N_HEADS // N_KV_HEADS
KV_WIDTH = N_KV_HEADS * HEAD_DIM
WINDOW = 128
ROT_DIM = HEAD_DIM // 4
ROT_HALF = ROT_DIM // 2
ROPE_THETA = 500000.0
EPS = 1e-6
SCALE = HEAD_DIM ** -0.5
NEG = -1e30

F32 = jnp.float32
BF16 = jnp.bfloat16

ROW_TILE = 512
FOX_Q_TILE = 256
VMEM_LIMIT = 56 * 1024 * 1024


def _dot(a, b):
    return jnp.dot(a, b, preferred_element_type=F32)


def _dot_nt(a, b):
    return lax.dot_general(a, b, (((1,), (1,)), ((), ())), preferred_element_type=F32)


def _head_norm_fm(t, gain_col):
    blocks = []
    for h in range(t.shape[0] // HEAD_DIM):
        blk = t[h * HEAD_DIM:(h + 1) * HEAD_DIM, :]
        ms = jnp.mean(blk * blk, axis=0, keepdims=True)
        blocks.append(blk * lax.rsqrt(ms + EPS))
    return jnp.concatenate(blocks, axis=0) * gain_col


def _rope_fm(t, cos, sin):
    blocks = []
    for h in range(t.shape[0] // HEAD_DIM):
        base = h * HEAD_DIM
        x1 = t[base:base + ROT_HALF, :]
        x2 = t[base + ROT_HALF:base + ROT_DIM, :]
        blocks.append(x1 * cos - x2 * sin)
        blocks.append(x1 * sin + x2 * cos)
        blocks.append(t[base + ROT_DIM:base + HEAD_DIM, :])
    return jnp.concatenate(blocks, axis=0)


def _split3(x):
    hi = x.astype(BF16)
    r1 = x - hi.astype(F32)
    mid = r1.astype(BF16)
    lo = (r1 - mid.astype(F32)).astype(BF16)
    return hi, mid, lo


def _inproj_a_kernel(x_ref, g_ref, wqkT_ref, wvg_ref, wfT_ref, bf_ref, gq_ref, gk_ref,
                     q_ref, kT_ref, v_ref, sg_ref, cT_ref, carry_ref):
    tm = x_ref.shape[1]

    @pl.when(pl.program_id(1) == 0)
    def _():
        carry_ref[...] = jnp.zeros_like(carry_ref)

    x = x_ref[0]
    ms = jnp.mean(x * x, axis=-1, keepdims=True)
    u = (x * lax.rsqrt(ms + EPS) * g_ref[...]).astype(BF16)

    qkT = _dot_nt(wqkT_ref[...], u)
    qT = _head_norm_fm(qkT[:D_MODEL], gq_ref[...])
    kT = _head_norm_fm(qkT[D_MODEL:], gk_ref[...])
    q_ref[0] = qT.T.astype(BF16)
    kT_ref[0] = kT.astype(BF16)

    vg = _dot(u, wvg_ref[...])
    v_ref[0] = vg[:, :D_MODEL].astype(BF16)
    gate = vg[:, D_MODEL:]
    sg_ref[0] = (gate * jax.nn.sigmoid(gate)).astype(BF16)

    f = _dot_nt(wfT_ref[...], u) + bf_ref[...]
    log_f = jnp.minimum(f, 0.0) - jnp.log1p(jnp.exp(-jnp.abs(f)))
    row = lax.broadcasted_iota(jnp.int32, (tm, tm), 0)
    col = lax.broadcasted_iota(jnp.int32, (tm, tm), 1)
    tri = (row <= col).astype(BF16)
    hi, mid, lo = _split3(log_f)
    c = (_dot(hi, tri) + _dot(mid, tri) + _dot(lo, tri)) + carry_ref[:, 0:1]
    cT_ref[0] = c
    carry_ref[...] = jnp.broadcast_to(c[:, tm - 1:tm], carry_ref.shape)


def _inproj_a(x, g, wqkT, wvg, wfT, bf, gq, gk):
    b, s, d = x.shape
    tm = ROW_TILE
    full = lambda shape: pl.BlockSpec(shape, lambda bi, i: (0,) * len(shape))
    tok = pl.BlockSpec((1, tm, d), lambda bi, i: (bi, i, 0))
    return pl.pallas_call(
        _inproj_a_kernel,
        grid=(b, s // tm),
        in_specs=[tok, full(g.shape), full(wqkT.shape), full(wvg.shape), full(wfT.shape),
                  full(bf.shape), full(gq.shape), full(gk.shape)],
        out_specs=[tok,
                   pl.BlockSpec((1, d, tm), lambda bi, i: (bi, 0, i)),
                   tok, tok,
                   pl.BlockSpec((1, N_HEADS, tm), lambda bi, i: (bi, 0, i))],
        out_shape=[jax.ShapeDtypeStruct((b, s, d), BF16),
                   jax.ShapeDtypeStruct((b, d, s), BF16),
                   jax.ShapeDtypeStruct((b, s, d), BF16),
                   jax.ShapeDtypeStruct((b, s, d), BF16),
                   jax.ShapeDtypeStruct((b, N_HEADS, s), F32)],
        scratch_shapes=[pltpu.VMEM((N_HEADS, 128), F32)],
        compiler_params=pltpu.CompilerParams(
            dimension_semantics=("parallel", "arbitrary"), vmem_limit_bytes=VMEM_LIMIT),
        name="inproj_a",
    )(x, g, wqkT, wvg, wfT, bf, gq, gk)


def _fox_kernel(q_ref, kT_ref, v_ref, c_ref, o_ref):
    s_len = q_ref.shape[1]
    tq = FOX_Q_TILE
    lane = lax.broadcasted_iota(jnp.int32, (tq, 128), 1)
    first = lane < HEAD_DIM
    row = lax.broadcasted_iota(jnp.int32, (tq, tq), 0)
    col = lax.broadcasted_iota(jnp.int32, (tq, tq), 1)
    causal = col <= row
    for qb in range(s_len // tq):
        r0 = qb * tq
        q2 = q_ref[0, r0:r0 + tq, :]
        outs = []
        for hh in range(2):
            qm = jnp.where(first if hh == 0 else jnp.logical_not(first), q2, jnp.zeros_like(q2))
            s_diag = _dot(qm, kT_ref[0, :, r0:r0 + tq]) - c_ref[0, 0, hh:hh + 1, r0:r0 + tq]
            s_diag = jnp.where(causal, s_diag, NEG)
            m = jnp.max(s_diag, axis=1, keepdims=True)
            if qb > 0:
                s_off = _dot(qm, kT_ref[0, :, :r0]) - c_ref[0, 0, hh:hh + 1, :r0]
                m = jnp.maximum(m, jnp.max(s_off, axis=1, keepdims=True))
            p_diag = jnp.exp(s_diag - m)
            l = jnp.sum(p_diag, axis=1, keepdims=True)
            pv = _dot(p_diag.astype(BF16), v_ref[0, r0:r0 + tq, :])
            if qb > 0:
                p_off = jnp.exp(s_off - m)
                l = l + jnp.sum(p_off, axis=1, keepdims=True)
                pv = pv + _dot(p_off.astype(BF16), v_ref[0, :r0, :])
            outs.append(pv / l)
        o_ref[0, r0:r0 + tq, :] = jnp.where(first, outs[0], outs[1]).astype(BF16)


def _fox_attention(q, kT, v, c4):
    b, s, d = q.shape
    n_pairs = d // 128
    return pl.pallas_call(
        _fox_kernel,
        grid=(b, n_pairs),
        in_specs=[pl.BlockSpec((1, s, 128), lambda bi, j: (bi, 0, j)),
                  pl.BlockSpec((1, 128, s), lambda bi, j: (bi, j, 0)),
                  pl.BlockSpec((1, s, 128), lambda bi, j: (bi, 0, j)),
                  pl.BlockSpec((1, 1, 2, s), lambda bi, j: (bi, j, 0, 0))],
        out_specs=pl.BlockSpec((1, s, 128), lambda bi, j: (bi, 0, j)),
        out_shape=jax.ShapeDtypeStruct((b, s, d), BF16),
        compiler_params=pltpu.CompilerParams(
            dimension_semantics=("parallel", "parallel"), vmem_limit_bytes=VMEM_LIMIT),
        name="fox_attention",
    )(q, kT, v, c4)


def _mid_kernel(o_ref, sg_ref, x_ref, wo_ref, gkv_ref, gb_ref, wkT_ref, wv_ref, gkn_ref,
                wqT_ref, wg_ref, gqn_ref, pos_ref, invf_ref,
                h_ref, kTb_ref, vrep_ref, qb_ref, sgb_ref):
    og = (o_ref[0].astype(F32) * sg_ref[0].astype(F32)).astype(BF16)
    h = x_ref[0] + _dot(og, wo_ref[...])
    h_ref[0] = h

    ms = jnp.mean(h * h, axis=-1, keepdims=True)
    hn = h * lax.rsqrt(ms + EPS)
    u_kv = (hn * gkv_ref[...]).astype(BF16)
    u_b = (hn * gb_ref[...]).astype(BF16)

    ang = invf_ref[...] * pos_ref[...].astype(F32)
    cos = jnp.cos(ang)
    sin = jnp.sin(ang)

    kT = _head_norm_fm(_dot_nt(wkT_ref[...], u_kv), gkn_ref[...])
    kTb_ref[0] = _rope_fm(kT, cos, sin).astype(BF16)

    v = _dot(u_kv, wv_ref[...]).astype(BF16)
    r_idx = lax.broadcasted_iota(jnp.int32, (KV_WIDTH, GROUP * HEAD_DIM), 0)
    c_idx = lax.broadcasted_iota(jnp.int32, (KV_WIDTH, GROUP * HEAD_DIM), 1)
    c_in_head = jnp.bitwise_and(c_idx, HEAD_DIM - 1)
    for g in range(N_KV_HEADS):
        rep = (r_idx == c_in_head + g * HEAD_DIM).astype(BF16)
        vrep_ref[0, g] = _dot(v, rep).astype(BF16)

    qT = _head_norm_fm(_dot_nt(wqT_ref[...], u_b), gqn_ref[...])
    qb_ref[0] = _rope_fm(qT, cos, sin).T.astype(BF16)

    gate = _dot(u_b, wg_ref[...])
    sgb_ref[0] = (gate * jax.nn.sigmoid(gate)).astype(BF16)


def _mid(o, sg, x, wo, gkv, gb, wkT, wv, gkn, wqT, wg, gqn, pos, invf):
    b, s, d = x.shape
    tm = ROW_TILE
    full = lambda a: pl.BlockSpec(a.shape, lambda bi, i: (0,) * a.ndim)
    tok = pl.BlockSpec((1, tm, d), lambda bi, i: (bi, i, 0))
    return pl.pallas_call(
        _mid_kernel,
        grid=(b, s // tm),
        in_specs=[tok, tok, tok, full(wo), full(gkv), full(gb), full(wkT), full(wv), full(gkn),
                  full(wqT), full(wg), full(gqn),
                  pl.BlockSpec((1, tm), lambda bi, i: (0, i)), full(invf)],
        out_specs=[tok,
                   pl.BlockSpec((1, KV_WIDTH, tm), lambda bi, i: (bi, 0, i)),
                   pl.BlockSpec((1, N_KV_HEADS, tm, GROUP * HEAD_DIM), lambda bi, i: (bi, 0, i, 0)),
                   tok, tok],
        out_shape=[jax.ShapeDtypeStruct((b, s, d), F32),
                   jax.ShapeDtypeStruct((b, KV_WIDTH, s), BF16),
                   jax.ShapeDtypeStruct((b, N_KV_HEADS, s, GROUP * HEAD_DIM), BF16),
                   jax.ShapeDtypeStruct((b, s, d), BF16),
                   jax.ShapeDtypeStruct((b, s, d), BF16)],
        compiler_params=pltpu.CompilerParams(
            dimension_semantics=("parallel", "parallel"), vmem_limit_bytes=VMEM_LIMIT),
        name="mid_proj",
    )(o, sg, x, wo, gkv, gb, wkT, wv, gkn, wqT, wg, gqn, pos, invf)


def _swa_kernel(sinks_ref, q_ref, kT_ref, vrep_ref, o_ref):
    s_len = q_ref.shape[1]
    g = pl.program_id(1)
    w = WINDOW
    gw = GROUP * HEAD_DIM
    kT = kT_ref[0]
    kT_rep = jnp.concatenate([kT] * GROUP, axis=0)
    lane_grp = lax.broadcasted_iota(jnp.int32, (w, gw), 1) // HEAD_DIM
    grp_masks = [lane_grp == i for i in range(GROUP)]
    row = lax.broadcasted_iota(jnp.int32, (GROUP * w, 2 * w), 0) % w
    col = lax.broadcasted_iota(jnp.int32, (GROUP * w, 2 * w), 1)
    band = jnp.logical_and(col > row, col <= row + w)
    row0 = lax.broadcasted_iota(jnp.int32, (GROUP * w, w), 0) % w
    col0 = lax.broadcasted_iota(jnp.int32, (GROUP * w, w), 1)
    band0 = col0 <= row0
    sink = jnp.concatenate(
        [jnp.full((w, 1), sinks_ref[g * GROUP + i], F32) for i in range(GROUP)], axis=0)
    for n in range(s_len // w):
        base = n * w
        q4 = q_ref[0, base:base + w, :]
        qm = jnp.concatenate([jnp.where(msk, q4, jnp.zeros_like(q4)) for msk in grp_masks], axis=0)
        k0 = max(base - w, 0)
        k1 = base + w
        s = _dot(qm, kT_rep[:, k0:k1])
        s = jnp.where(band if n > 0 else band0, s, NEG)
        m = jnp.maximum(jnp.max(s, axis=1, keepdims=True), sink)
        p = jnp.exp(s - m)
        denom = jnp.sum(p, axis=1, keepdims=True) + jnp.exp(sink - m)
        pv = _dot(p.astype(BF16), vrep_ref[0, 0, k0:k1, :]) / denom
        o = jnp.zeros((w, gw), F32)
        for i in range(GROUP):
            o = jnp.where(grp_masks[i], pv[i * w:(i + 1) * w, :], o)
        o_ref[0, base:base + w, :] = o.astype(BF16)


def _swa_attention(sinks, q, kT, vrep):
    b, s, d = q.shape
    gw = GROUP * HEAD_DIM
    return pl.pallas_call(
        _swa_kernel,
        grid=(b, N_KV_HEADS),
        in_specs=[pl.BlockSpec(memory_space=pltpu.SMEM),
                  pl.BlockSpec((1, s, gw), lambda bi, g: (bi, 0, g)),
                  pl.BlockSpec((1, HEAD_DIM, s), lambda bi, g: (bi, g, 0)),
                  pl.BlockSpec((1, 1, s, gw), lambda bi, g: (bi, g, 0, 0))],
        out_specs=pl.BlockSpec((1, s, gw), lambda bi, g: (bi, 0, g)),
        out_shape=jax.ShapeDtypeStruct((b, s, d), BF16),
        compiler_params=pltpu.CompilerParams(
            dimension_semantics=("parallel", "parallel"), vmem_limit_bytes=VMEM_LIMIT),
        name="swa_attention",
    )(sinks, q, kT, vrep)


def _outproj_kernel(o_ref, sg_ref, h_ref, wo_ref, out_ref):
    og = (o_ref[0].astype(F32) * sg_ref[0].astype(F32)).astype(BF16)
    out_ref[0] = h_ref[0] + _dot(og, wo_ref[...])


def _outproj(o, sg, h, wo):
    b, s, d = h.shape
    tm = ROW_TILE
    tok = pl.BlockSpec((1, tm, d), lambda bi, i: (bi, i, 0))
    return pl.pallas_call(
        _outproj_kernel,
        grid=(b, s // tm),
        in_specs=[tok, tok, tok, pl.BlockSpec(wo.shape, lambda bi, i: (0, 0))],
        out_specs=tok,
        out_shape=jax.ShapeDtypeStruct((b, s, d), F32),
        compiler_params=pltpu.CompilerParams(
            dimension_semantics=("parallel", "parallel"), vmem_limit_bytes=VMEM_LIMIT),
        name="outproj_b",
    )(o, sg, h, wo)


def _gain_col(g, n_heads, scale=1.0):
    return (jnp.tile(g.astype(F32), n_heads) * scale).reshape(n_heads * HEAD_DIM, 1)


def kernel(x, positions, norm_a_g, w_in_a, b_forget, qnorm_a_g, knorm_a_g, w_out_a, kv_norm_g, w_kv,
           knorm_b_g, norm_b_g, w_in_b, qnorm_b_g, sinks, w_out_b):
    b, s, d = x.shape
    wa = w_in_a[0]
    wqkT = wa[:, :2 * d].T.astype(BF16)
    wvg = jnp.concatenate([wa[:, 2 * d:3 * d], wa[:, 3 * d + N_HEADS:]], axis=1).astype(BF16)
    wfT = wa[:, 3 * d:3 * d + N_HEADS].T.astype(BF16)
    q, kT, v, sg, cT = _inproj_a(
        x, norm_a_g[0].reshape(1, d), wqkT, wvg, wfT, b_forget[0].reshape(N_HEADS, 1),
        _gain_col(qnorm_a_g[0], N_HEADS, SCALE), _gain_col(knorm_a_g[0], N_HEADS))
    o = _fox_attention(q, kT, v, cT.reshape(b, N_HEADS // 2, 2, s))

    wb = w_in_b[0]
    inv_freq = jnp.power(jnp.float32(ROPE_THETA),
                         -jnp.arange(0, ROT_DIM, 2, dtype=F32) / ROT_DIM).reshape(ROT_HALF, 1)
    h, kTb, vrep, qb, sgb = _mid(
        o, sg, x, w_out_a[0].astype(BF16), kv_norm_g.reshape(1, d), norm_b_g[0].reshape(1, d),
        w_kv[:, :KV_WIDTH].T.astype(BF16), w_kv[:, KV_WIDTH:].astype(BF16),
        _gain_col(knorm_b_g, N_KV_HEADS),
        wb[:, :d].T.astype(BF16), wb[:, d:].astype(BF16), _gain_col(qnorm_b_g[0], N_HEADS, SCALE),
        positions.reshape(1, s), inv_freq)
    ob = _swa_attention(sinks[0], qb, kTb, vrep)
    return _outproj(ob, sgb, h, w_out_b[0].astype(BF16))
```

```python
import functools

import jax
import jax.numpy as jnp
from jax import lax
from jax.experimental import pallas as pl
from jax.experimental.pallas import tpu as pltpu

D_MODEL = 1024
HEAD_DIM = 64
N_HEADS = 16
N_KV_HEADS = 4
GROUP = N_HEADS // N_KV_HEADS
KV_WIDTH = N_KV_HEADS * HEAD_DIM
WINDOW = 128
ROT_DIM = HEAD_DIM // 4
ROT_HALF = ROT_DIM // 2
ROPE_THETA = 500000.0
EPS = 1e-6
SCALE = HEAD_DIM ** -0.5
LOG2E = 1.4426950408889634
NEG = -1e30

F32 = jnp.float32
BF16 = jnp.bfloat16

ROW_TILE = 512
FOX_Q_TILE = 256
VMEM_LIMIT = 56 * 1024 * 1024


def _dot(a, b):
    return jnp.dot(a, b, preferred_element_type=F32)


def _dot_nt(a, b):
    return lax.dot_general(a, b, (((1,), (1,)), ((), ())), preferred_element_type=F32)


def _head_norm_fm(t, gain_col):
    blocks = []
    for h in range(t.shape[0] // HEAD_DIM):
        blk = t[h * HEAD_DIM:(h + 1) * HEAD_DIM, :]
        ms = jnp.mean(blk * blk, axis=0, keepdims=True)
        blocks.append(blk * lax.rsqrt(ms + EPS))
    return jnp.concatenate(blocks, axis=0) * gain_col


def _rope_fm(t, cos, sin):
    blocks = []
    for h in range(t.shape[0] // HEAD_DIM):
        base = h * HEAD_DIM
        x1 = t[base:base + ROT_HALF, :]
        x2 = t[base + ROT_HALF:base + ROT_DIM, :]
        blocks.append(x1 * cos - x2 * sin)
        blocks.append(x1 * sin + x2 * cos)
        blocks.append(t[base + ROT_DIM:base + HEAD_DIM, :])
    return jnp.concatenate(blocks, axis=0)


def _split3(x):
    hi = x.astype(BF16)
    r1 = x - hi.astype(F32)
    mid = r1.astype(BF16)
    lo = (r1 - mid.astype(F32)).astype(BF16)
    return hi, mid, lo


def _inproj_a_kernel(x_ref, g_ref, wqkT_ref, wvg_ref, wfT_ref, bf_ref, gq_ref, gk_ref,
                     q_ref, kT_ref, v_ref, sg_ref, cT_ref, carry_ref):
    tm = x_ref.shape[1]

    @pl.when(pl.program_id(1) == 0)
    def _():
        carry_ref[...] = jnp.zeros_like(carry_ref)

    x = x_ref[0]
    ms = jnp.mean(x * x, axis=-1, keepdims=True)
    u = (x * lax.rsqrt(ms + EPS) * g_ref[...]).astype(BF16)

    qkT = _dot_nt(wqkT_ref[...], u)
    qT = _head_norm_fm(qkT[:D_MODEL], gq_ref[...])
    kT = _head_norm_fm(qkT[D_MODEL:], gk_ref[...])
    q_ref[0] = qT.T.astype(BF16)
    kT_ref[0] = kT.astype(BF16)

    vg = _dot(u, wvg_ref[...])
    v_ref[0] = vg[:, :D_MODEL].astype(BF16)
    gate = vg[:, D_MODEL:]
    sg_ref[0] = (gate * jax.nn.sigmoid(gate)).astype(BF16)

    f = _dot_nt(wfT_ref[...], u) + bf_ref[...]
    log_f = jnp.minimum(f, 0.0) - jnp.log1p(jnp.exp(-jnp.abs(f)))
    row = lax.broadcasted_iota(jnp.int32, (tm, tm), 0)
    col = lax.broadcasted_iota(jnp.int32, (tm, tm), 1)
    tri = (row <= col).astype(BF16)
    hi, mid, lo = _split3(log_f)
    c = (_dot(hi, tri) + _dot(mid, tri) + _dot(lo, tri)) + carry_ref[:, 0:1]
    cT_ref[0] = c * LOG2E
    carry_ref[...] = jnp.broadcast_to(c[:, tm - 1:tm], carry_ref.shape)


def _inproj_a(x, g, wqkT, wvg, wfT, bf, gq, gk):
    b, s, d = x.shape
    tm = ROW_TILE
    full = lambda shape: pl.BlockSpec(shape, lambda bi, i: (0,) * len(shape))
    tok = pl.BlockSpec((1, tm, d), lambda bi, i: (bi, i, 0))
    return pl.pallas_call(
        _inproj_a_kernel,
        grid=(b, s // tm),
        in_specs=[tok, full(g.shape), full(wqkT.shape), full(wvg.shape), full(wfT.shape),
                  full(bf.shape), full(gq.shape), full(gk.shape)],
        out_specs=[tok,
                   pl.BlockSpec((1, d, tm), lambda bi, i: (bi, 0, i)),
                   tok, tok,
                   pl.BlockSpec((1, N_HEADS, tm), lambda bi, i: (bi, 0, i))],
        out_shape=[jax.ShapeDtypeStruct((b, s, d), BF16),
                   jax.ShapeDtypeStruct((b, d, s), BF16),
                   jax.ShapeDtypeStruct((b, s, d), BF16),
                   jax.ShapeDtypeStruct((b, s, d), BF16),
                   jax.ShapeDtypeStruct((b, N_HEADS, s), F32)],
        scratch_shapes=[pltpu.VMEM((N_HEADS, 128), F32)],
        compiler_params=pltpu.CompilerParams(
            dimension_semantics=("parallel", "arbitrary"), vmem_limit_bytes=VMEM_LIMIT),
        name="inproj_a",
    )(x, g, wqkT, wvg, wfT, bf, gq, gk)


BIAS_ROWS = 16


def _fox_kernel(q_ref, kT_ref, v_ref, c_ref, o_ref, bias_ref, vext_ref):
    s_len = q_ref.shape[1]
    tq = FOX_Q_TILE
    lane = lax.broadcasted_iota(jnp.int32, (tq, 128), 1)
    first = lane < HEAD_DIM
    ones_a = jnp.where(lane < 3, 1.0, 0.0).astype(BF16)
    ones_b = jnp.where(jnp.logical_and(lane >= 3, lane < 6), 1.0, 0.0).astype(BF16)
    row = lax.broadcasted_iota(jnp.int32, (2 * tq, tq), 0) % tq
    col = lax.broadcasted_iota(jnp.int32, (2 * tq, tq), 1)
    causal = col <= row

    parts = _split3(-c_ref[0, 0, 0:1, :]) + _split3(-c_ref[0, 0, 1:2, :])
    brow = lax.broadcasted_iota(jnp.int32, (BIAS_ROWS, s_len), 0)
    bias = jnp.zeros((BIAS_ROWS, s_len), F32)
    for r, part in enumerate(parts):
        bias = jnp.where(brow == r, part.astype(F32), bias)
    bias_ref[...] = bias.astype(BF16)
    vext_ref[:, :128] = v_ref[0]
    vext_ref[:, 128:] = jnp.ones((s_len, 128), BF16)

    def rhs(c0, c1):
        pad = jnp.zeros((128 - BIAS_ROWS, c1 - c0), BF16)
        return jnp.concatenate([kT_ref[0, :, c0:c1], bias_ref[:, c0:c1], pad], axis=0)

    for qb in reversed(range(s_len // tq)):
        r0 = qb * tq
        q2 = q_ref[0, r0:r0 + tq, :]
        zero = jnp.zeros_like(q2)
        lhs = jnp.concatenate(
            [jnp.concatenate([jnp.where(first, q2, zero), ones_a], axis=1),
             jnp.concatenate([jnp.where(first, zero, q2), ones_b], axis=1)], axis=0)
        s_diag = jnp.where(causal, _dot(lhs, rhs(r0, r0 + tq)), NEG)
        m = jnp.max(s_diag, axis=1, keepdims=True)
        if qb > 0:
            s_off = _dot(lhs, rhs(0, r0))
            m = jnp.maximum(m, jnp.max(s_off, axis=1, keepdims=True))
        pv = _dot(jnp.exp2(s_diag - m).astype(BF16), vext_ref[r0:r0 + tq, :])
        if qb > 0:
            pv = pv + _dot(jnp.exp2(s_off - m).astype(BF16), vext_ref[:r0, :])
        o = pv[:, :128] / pv[:, 128:]
        o_ref[0, r0:r0 + tq, :] = jnp.where(first, o[:tq], o[tq:]).astype(BF16)


def _fox_attention(q, kT, v, c4):
    b, s, d = q.shape
    n_pairs = d // 128
    return pl.pallas_call(
        _fox_kernel,
        grid=(b, n_pairs),
        in_specs=[pl.BlockSpec((1, s, 128), lambda bi, j: (bi, 0, j)),
                  pl.BlockSpec((1, 128, s), lambda bi, j: (bi, j, 0)),
                  pl.BlockSpec((1, s, 128), lambda bi, j: (bi, 0, j)),
                  pl.BlockSpec((1, 1, 2, s), lambda bi, j: (bi, j, 0, 0))],
        out_specs=pl.BlockSpec((1, s, 128), lambda bi, j: (bi, 0, j)),
        out_shape=jax.ShapeDtypeStruct((b, s, d), BF16),
        scratch_shapes=[pltpu.VMEM((BIAS_ROWS, s), BF16), pltpu.VMEM((s, 256), BF16)],
        compiler_params=pltpu.CompilerParams(
            dimension_semantics=("parallel", "parallel"), vmem_limit_bytes=VMEM_LIMIT),
        name="fox_attention",
    )(q, kT, v, c4)


def _mid_kernel(o_ref, sg_ref, x_ref, wo_ref, gkv_ref, gb_ref, wkT_ref, wv_ref, gkn_ref,
                wqT_ref, wg_ref, gqn_ref, pos_ref, invf_ref,
                h_ref, kTb_ref, vrep_ref, qb_ref, sgb_ref):
    og = (o_ref[0].astype(F32) * sg_ref[0].astype(F32)).astype(BF16)
    h = x_ref[0] + _dot(og, wo_ref[...])
    h_ref[0] = h

    ms = jnp.mean(h * h, axis=-1, keepdims=True)
    hn = h * lax.rsqrt(ms + EPS)
    u_kv = (hn * gkv_ref[...]).astype(BF16)
    u_b = (hn * gb_ref[...]).astype(BF16)

    ang = invf_ref[...] * pos_ref[...].astype(F32)
    cos = jnp.cos(ang)
    sin = jnp.sin(ang)

    kT = _head_norm_fm(_dot_nt(wkT_ref[...], u_kv), gkn_ref[...])
    kTb_ref[0] = _rope_fm(kT, cos, sin).astype(BF16)

    v = _dot(u_kv, wv_ref[...]).astype(BF16)
    r_idx = lax.broadcasted_iota(jnp.int32, (KV_WIDTH, GROUP * HEAD_DIM), 0)
    c_idx = lax.broadcasted_iota(jnp.int32, (KV_WIDTH, GROUP * HEAD_DIM), 1)
    c_in_head = jnp.bitwise_and(c_idx, HEAD_DIM - 1)
    for g in range(N_KV_HEADS):
        rep = (r_idx == c_in_head + g * HEAD_DIM).astype(BF16)
        vrep_ref[0, g] = _dot(v, rep).astype(BF16)

    qT = _head_norm_fm(_dot_nt(wqT_ref[...], u_b), gqn_ref[...])
    qb_ref[0] = _rope_fm(qT, cos, sin).T.astype(BF16)

    gate = _dot(u_b, wg_ref[...])
    sgb_ref[0] = (gate * jax.nn.sigmoid(gate)).astype(BF16)


def _mid(o, sg, x, wo, gkv, gb, wkT, wv, gkn, wqT, wg, gqn, pos, invf):
    b, s, d = x.shape
    tm = ROW_TILE
    full = lambda a: pl.BlockSpec(a.shape, lambda bi, i: (0,) * a.ndim)
    tok = pl.BlockSpec((1, tm, d), lambda bi, i: (bi, i, 0))
    return pl.pallas_call(
        _mid_kernel,
        grid=(b, s // tm),
        in_specs=[tok, tok, tok, full(wo), full(gkv), full(gb), full(wkT), full(wv), full(gkn),
                  full(wqT), full(wg), full(gqn),
                  pl.BlockSpec((1, tm), lambda bi, i: (0, i)), full(invf)],
        out_specs=[tok,
                   pl.BlockSpec((1, KV_WIDTH, tm), lambda bi, i: (bi, 0, i)),
                   pl.BlockSpec((1, N_KV_HEADS, tm, GROUP * HEAD_DIM), lambda bi, i: (bi, 0, i, 0)),
                   tok, tok],
        out_shape=[jax.ShapeDtypeStruct((b, s, d), F32),
                   jax.ShapeDtypeStruct((b, KV_WIDTH, s), BF16),
                   jax.ShapeDtypeStruct((b, N_KV_HEADS, s, GROUP * HEAD_DIM), BF16),
                   jax.ShapeDtypeStruct((b, s, d), BF16),
                   jax.ShapeDtypeStruct((b, s, d), BF16)],
        compiler_params=pltpu.CompilerParams(
            dimension_semantics=("parallel", "parallel"), vmem_limit_bytes=VMEM_LIMIT),
        name="mid_proj",
    )(o, sg, x, wo, gkv, gb, wkT, wv, gkn, wqT, wg, gqn, pos, invf)


def _swa_kernel(sinks_ref, q_ref, kT_ref, vrep_ref, o_ref):
    s_len = q_ref.shape[1]
    g = pl.program_id(1)
    w = WINDOW
    gw = GROUP * HEAD_DIM
    kT = kT_ref[0]
    kT_rep = jnp.concatenate([kT] * GROUP, axis=0)
    lane_grp = lax.broadcasted_iota(jnp.int32, (w, gw), 1) // HEAD_DIM
    grp_masks = [lane_grp == i for i in range(GROUP)]
    row = lax.broadcasted_iota(jnp.int32, (GROUP * w, 2 * w), 0) % w
    col = lax.broadcasted_iota(jnp.int32, (GROUP * w, 2 * w), 1)
    band = jnp.logical_and(col > row, col <= row + w)
    row0 = lax.broadcasted_iota(jnp.int32, (GROUP * w, w), 0) % w
    col0 = lax.broadcasted_iota(jnp.int32, (GROUP * w, w), 1)
    band0 = col0 <= row0
    sink = jnp.concatenate(
        [jnp.full((w, 1), sinks_ref[g * GROUP + i], F32) for i in range(GROUP)], axis=0)
    for n in range(s_len // w):
        base = n * w
        q4 = q_ref[0, base:base + w, :]
        qm = jnp.concatenate([jnp.where(msk, q4, jnp.zeros_like(q4)) for msk in grp_masks], axis=0)
        k0 = max(base - w, 0)
        k1 = base + w
        s = _dot(qm, kT_rep[:, k0:k1])
        s = jnp.where(band if n > 0 else band0, s, NEG)
        m = jnp.maximum(jnp.max(s, axis=1, keepdims=True), sink)
        p = jnp.exp(s - m)
        denom = jnp.sum(p, axis=1, keepdims=True) + jnp.exp(sink - m)
        pv = _dot(p.astype(BF16), vrep_ref[0, 0, k0:k1, :]) / denom
        o = jnp.zeros((w, gw), F32)
        for i in range(GROUP):
            o = jnp.where(grp_masks[i], pv[i * w:(i + 1) * w, :], o)
        o_ref[0, base:base + w, :] = o.astype(BF16)


def _swa_attention(sinks, q, kT, vrep):
    b, s, d = q.shape
    gw = GROUP * HEAD_DIM
    return pl.pallas_call(
        _swa_kernel,
        grid=(b, N_KV_HEADS),
        in_specs=[pl.BlockSpec(memory_space=pltpu.SMEM),
                  pl.BlockSpec((1, s, gw), lambda bi, g: (bi, 0, g)),
                  pl.BlockSpec((1, HEAD_DIM, s), lambda bi, g: (bi, g, 0)),
                  pl.BlockSpec((1, 1, s, gw), lambda bi, g: (bi, g, 0, 0))],
        out_specs=pl.BlockSpec((1, s, gw), lambda bi, g: (bi, 0, g)),
        out_shape=jax.ShapeDtypeStruct((b, s, d), BF16),
        compiler_params=pltpu.CompilerParams(
            dimension_semantics=("parallel", "parallel"), vmem_limit_bytes=VMEM_LIMIT),
        name="swa_attention",
    )(sinks, q, kT, vrep)


def _outproj_kernel(o_ref, sg_ref, h_ref, wo_ref, out_ref):
    og = (o_ref[0].astype(F32) * sg_ref[0].astype(F32)).astype(BF16)
    out_ref[0] = h_ref[0] + _dot(og, wo_ref[...])


def _outproj(o, sg, h, wo):
    b, s, d = h.shape
    tm = ROW_TILE
    tok = pl.BlockSpec((1, tm, d), lambda bi, i: (bi, i, 0))
    return pl.pallas_call(
        _outproj_kernel,
        grid=(b, s // tm),
        in_specs=[tok, tok, tok, pl.BlockSpec(wo.shape, lambda bi, i: (0, 0))],
        out_specs=tok,
        out_shape=jax.ShapeDtypeStruct((b, s, d), F32),
        compiler_params=pltpu.CompilerParams(
            dimension_semantics=("parallel", "parallel"), vmem_limit_bytes=VMEM_LIMIT),
        name="outproj_b",
    )(o, sg, h, wo)


def _gain_col(g, n_heads, scale=1.0):
    return (jnp.tile(g.astype(F32), n_heads) * scale).reshape(n_heads * HEAD_DIM, 1)


def kernel(x, positions, norm_a_g, w_in_a, b_forget, qnorm_a_g, knorm_a_g, w_out_a, kv_norm_g, w_kv,
           knorm_b_g, norm_b_g, w_in_b, qnorm_b_g, sinks, w_out_b):
    b, s, d = x.shape
    wa = w_in_a[0]
    wqkT = wa[:, :2 * d].T.astype(BF16)
    wvg = jnp.concatenate([wa[:, 2 * d:3 * d], wa[:, 3 * d + N_HEADS:]], axis=1).astype(BF16)
    wfT = wa[:, 3 * d:3 * d + N_HEADS].T.astype(BF16)
    q, kT, v, sg, cT = _inproj_a(
        x, norm_a_g[0].reshape(1, d), wqkT, wvg, wfT, b_forget[0].reshape(N_HEADS, 1),
        _gain_col(qnorm_a_g[0], N_HEADS, SCALE * LOG2E), _gain_col(knorm_a_g[0], N_HEADS))
    o = _fox_attention(q, kT, v, cT.reshape(b, N_HEADS // 2, 2, s))

    wb = w_in_b[0]
    inv_freq = jnp.power(jnp.float32(ROPE_THETA),
                         -jnp.arange(0, ROT_DIM, 2, dtype=F32) / ROT_DIM).reshape(ROT_HALF, 1)
    h, kTb, vrep, qb, sgb = _mid(
        o, sg, x, w_out_a[0].astype(BF16), kv_norm_g.reshape(1, d), norm_b_g[0].reshape(1, d),
        w_kv[:, :KV_WIDTH].T.astype(BF16), w_kv[:, KV_WIDTH:].astype(BF16),
        _gain_col(knorm_b_g, N_KV_HEADS),
        wb[:, :d].T.astype(BF16), wb[:, d:].astype(BF16), _gain_col(qnorm_b_g[0], N_HEADS, SCALE),
        positions.reshape(1, s), inv_freq)
    ob = _swa_attention(sinks[0], qb, kTb, vrep)
    return _outproj(ob, sgb, h, w_out_b[0].astype(BF16))
```

```python
import functools

import jax
import jax.numpy as jnp
from jax import lax
from jax.experimental import pallas as pl
from jax.experimental.pallas import tpu as pltpu

D_MODEL = 1024
HEAD_DIM = 64
N_HEADS = 16
N_KV_HEADS = 4
GROUP = N_HEADS // N_KV_HEADS
KV_WIDTH = N_KV_HEADS * HEAD_DIM
WINDOW = 128
ROT_DIM = HEAD_DIM // 4
ROT_HALF = ROT_DIM // 2
ROPE_THETA = 500000.0
EPS = 1e-6
SCALE = HEAD_DIM ** -0.5
LOG2E = 1.4426950408889634
NEG = -1e30

F32 = jnp.float32
BF16 = jnp.bfloat16

ROW_TILE = 512
FOX_Q_TILE = 256
SWA_LOOKAHEAD = 2
VMEM_LIMIT = 56 * 1024 * 1024


def _dot(a, b):
    return jnp.dot(a, b, preferred_element_type=F32)


def _dot_nt(a, b):
    return lax.dot_general(a, b, (((1,), (1,)), ((), ())), preferred_element_type=F32)


def _head_norm_fm(t, gain_col):
    blocks = []
    for h in range(t.shape[0] // HEAD_DIM):
        blk = t[h * HEAD_DIM:(h + 1) * HEAD_DIM, :]
        ms = jnp.mean(blk * blk, axis=0, keepdims=True)
        blocks.append(blk * lax.rsqrt(ms + EPS))
    return jnp.concatenate(blocks, axis=0) * gain_col


def _rope_fm(t, cos, sin):
    blocks = []
    for h in range(t.shape[0] // HEAD_DIM):
        base = h * HEAD_DIM
        x1 = t[base:base + ROT_HALF, :]
        x2 = t[base + ROT_HALF:base + ROT_DIM, :]
        blocks.append(x1 * cos - x2 * sin)
        blocks.append(x1 * sin + x2 * cos)
        blocks.append(t[base + ROT_DIM:base + HEAD_DIM, :])
    return jnp.concatenate(blocks, axis=0)


def _split3(x):
    hi = x.astype(BF16)
    r1 = x - hi.astype(F32)
    mid = r1.astype(BF16)
    lo = (r1 - mid.astype(F32)).astype(BF16)
    return hi, mid, lo


def _inproj_a_kernel(x_ref, g_ref, wqkT_ref, wvg_ref, wfT_ref, bf_ref, gq_ref, gk_ref,
                     q_ref, kT_ref, v_ref, sg_ref, cT_ref, carry_ref):
    tm = x_ref.shape[1]

    @pl.when(pl.program_id(1) == 0)
    def _():
        carry_ref[...] = jnp.zeros_like(carry_ref)

    x = x_ref[0]
    ms = jnp.mean(x * x, axis=-1, keepdims=True)
    u = (x * lax.rsqrt(ms + EPS) * g_ref[...]).astype(BF16)

    qkT = _dot_nt(wqkT_ref[...], u)
    qT = _head_norm_fm(qkT[:D_MODEL], gq_ref[...])
    kT = _head_norm_fm(qkT[D_MODEL:], gk_ref[...])
    q_ref[0] = qT.T.astype(BF16)
    kT_ref[0] = kT.astype(BF16)

    vg = _dot(u, wvg_ref[...])
    v_ref[0] = vg[:, :D_MODEL].astype(BF16)
    gate = vg[:, D_MODEL:]
    sg_ref[0] = (gate * jax.nn.sigmoid(gate)).astype(BF16)

    f = _dot_nt(wfT_ref[...], u) + bf_ref[...]
    log_f = jnp.minimum(f, 0.0) - jnp.log1p(jnp.exp(-jnp.abs(f)))
    row = lax.broadcasted_iota(jnp.int32, (tm, tm), 0)
    col = lax.broadcasted_iota(jnp.int32, (tm, tm), 1)
    tri = (row <= col).astype(BF16)
    hi, mid, lo = _split3(log_f)
    c = (_dot(hi, tri) + _dot(mid, tri) + _dot(lo, tri)) + carry_ref[:, 0:1]
    cT_ref[0] = c * LOG2E
    carry_ref[...] = jnp.broadcast_to(c[:, tm - 1:tm], carry_ref.shape)


def _inproj_a(x, g, wqkT, wvg, wfT, bf, gq, gk):
    b, s, d = x.shape
    tm = ROW_TILE
    full = lambda shape: pl.BlockSpec(shape, lambda bi, i: (0,) * len(shape))
    tok = pl.BlockSpec((1, tm, d), lambda bi, i: (bi, i, 0))
    return pl.pallas_call(
        _inproj_a_kernel,
        grid=(b, s // tm),
        in_specs=[tok, full(g.shape), full(wqkT.shape), full(wvg.shape), full(wfT.shape),
                  full(bf.shape), full(gq.shape), full(gk.shape)],
        out_specs=[tok,
                   pl.BlockSpec((1, d, tm), lambda bi, i: (bi, 0, i)),
                   tok, tok,
                   pl.BlockSpec((1, N_HEADS, tm), lambda bi, i: (bi, 0, i))],
        out_shape=[jax.ShapeDtypeStruct((b, s, d), BF16),
                   jax.ShapeDtypeStruct((b, d, s), BF16),
                   jax.ShapeDtypeStruct((b, s, d), BF16),
                   jax.ShapeDtypeStruct((b, s, d), BF16),
                   jax.ShapeDtypeStruct((b, N_HEADS, s), F32)],
        scratch_shapes=[pltpu.VMEM((N_HEADS, 128), F32)],
        compiler_params=pltpu.CompilerParams(
            dimension_semantics=("parallel", "arbitrary"), vmem_limit_bytes=VMEM_LIMIT),
        name="inproj_a",
    )(x, g, wqkT, wvg, wfT, bf, gq, gk)


BIAS_ROWS = 16


def _fox_kernel(q_ref, kT_ref, v_ref, c_ref, o_ref, bias_ref, vext_ref):
    s_len = q_ref.shape[1]
    tq = FOX_Q_TILE
    lane = lax.broadcasted_iota(jnp.int32, (tq, 128), 1)
    first = lane < HEAD_DIM
    ones_a = jnp.where(lane < 3, 1.0, 0.0).astype(BF16)
    ones_b = jnp.where(jnp.logical_and(lane >= 3, lane < 6), 1.0, 0.0).astype(BF16)
    row = lax.broadcasted_iota(jnp.int32, (2 * tq, tq), 0) % tq
    col = lax.broadcasted_iota(jnp.int32, (2 * tq, tq), 1)
    causal = col <= row

    parts = _split3(-c_ref[0, 0, 0:1, :]) + _split3(-c_ref[0, 0, 1:2, :])
    brow = lax.broadcasted_iota(jnp.int32, (BIAS_ROWS, s_len), 0)
    bias = jnp.zeros((BIAS_ROWS, s_len), F32)
    for r, part in enumerate(parts):
        bias = jnp.where(brow == r, part.astype(F32), bias)
    bias_ref[...] = bias.astype(BF16)
    vext_ref[:, :128] = v_ref[0]
    vext_ref[:, 128:] = jnp.ones((s_len, 128), BF16)

    def rhs(c0, c1):
        pad = jnp.zeros((128 - BIAS_ROWS, c1 - c0), BF16)
        return jnp.concatenate([kT_ref[0, :, c0:c1], bias_ref[:, c0:c1], pad], axis=0)

    def logits(qb):
        r0 = qb * tq
        q2 = q_ref[0, r0:r0 + tq, :]
        zero = jnp.zeros_like(q2)
        lhs = jnp.concatenate(
            [jnp.concatenate([jnp.where(first, q2, zero), ones_a], axis=1),
             jnp.concatenate([jnp.where(first, zero, q2), ones_b], axis=1)], axis=0)
        s_diag = jnp.where(causal, _dot(lhs, rhs(r0, r0 + tq)), NEG)
        s_off = _dot(lhs, rhs(0, r0)) if qb > 0 else None
        return s_diag, s_off

    order = list(reversed(range(s_len // tq)))
    pending = logits(order[0])
    for idx, qb in enumerate(order):
        r0 = qb * tq
        s_diag, s_off = pending
        if idx + 1 < len(order):
            pending = logits(order[idx + 1])
        m = jnp.max(s_diag, axis=1, keepdims=True)
        if qb > 0:
            m = jnp.maximum(m, jnp.max(s_off, axis=1, keepdims=True))
        pv = _dot(jnp.exp2(s_diag - m).astype(BF16), vext_ref[r0:r0 + tq, :])
        if qb > 0:
            pv = pv + _dot(jnp.exp2(s_off - m).astype(BF16), vext_ref[:r0, :])
        o = pv[:, :128] / pv[:, 128:]
        o_ref[0, r0:r0 + tq, :] = jnp.where(first, o[:tq], o[tq:]).astype(BF16)


def _fox_attention(q, kT, v, c4):
    b, s, d = q.shape
    n_pairs = d // 128
    return pl.pallas_call(
        _fox_kernel,
        grid=(b, n_pairs),
        in_specs=[pl.BlockSpec((1, s, 128), lambda bi, j: (bi, 0, j)),
                  pl.BlockSpec((1, 128, s), lambda bi, j: (bi, j, 0)),
                  pl.BlockSpec((1, s, 128), lambda bi, j: (bi, 0, j)),
                  pl.BlockSpec((1, 1, 2, s), lambda bi, j: (bi, j, 0, 0))],
        out_specs=pl.BlockSpec((1, s, 128), lambda bi, j: (bi, 0, j)),
        out_shape=jax.ShapeDtypeStruct((b, s, d), BF16),
        scratch_shapes=[pltpu.VMEM((BIAS_ROWS, s), BF16), pltpu.VMEM((s, 256), BF16)],
        compiler_params=pltpu.CompilerParams(
            dimension_semantics=("parallel", "parallel"), vmem_limit_bytes=VMEM_LIMIT),
        name="fox_attention",
    )(q, kT, v, c4)


def _mid_kernel(o_ref, sg_ref, x_ref, wo_ref, gkv_ref, gb_ref, wkT_ref, wvT_ref, gkn_ref,
                wqT_ref, wgT_ref, gqn_ref, pos_ref, invf_ref,
                h_ref, kb_ref, vT_ref, qT_ref, sgT_ref):
    og = (o_ref[0].astype(F32) * sg_ref[0].astype(F32)).astype(BF16)
    h = x_ref[0] + _dot(og, wo_ref[...])
    h_ref[0] = h

    ms = jnp.mean(h * h, axis=-1, keepdims=True)
    hn = h * lax.rsqrt(ms + EPS)
    u_kv = (hn * gkv_ref[...]).astype(BF16)
    u_b = (hn * gb_ref[...]).astype(BF16)

    ang = invf_ref[...] * pos_ref[...].astype(F32)
    cos = jnp.cos(ang)
    sin = jnp.sin(ang)

    kT = _head_norm_fm(_dot_nt(wkT_ref[...], u_kv), gkn_ref[...])
    kb_ref[0] = _rope_fm(kT, cos, sin).T.astype(BF16)
    vT_ref[0] = _dot_nt(wvT_ref[...], u_kv).astype(BF16)

    qT = _head_norm_fm(_dot_nt(wqT_ref[...], u_b), gqn_ref[...])
    qT_ref[0] = _rope_fm(qT, cos, sin).astype(BF16)

    gateT = _dot_nt(wgT_ref[...], u_b)
    sgT_ref[0] = (gateT * jax.nn.sigmoid(gateT)).astype(BF16)


def _mid(o, sg, x, wo, gkv, gb, wkT, wvT, gkn, wqT, wgT, gqn, pos, invf):
    b, s, d = x.shape
    tm = ROW_TILE
    full = lambda a: pl.BlockSpec(a.shape, lambda bi, i: (0,) * a.ndim)
    tok = pl.BlockSpec((1, tm, d), lambda bi, i: (bi, i, 0))
    feat = pl.BlockSpec((1, d, tm), lambda bi, i: (bi, 0, i))
    return pl.pallas_call(
        _mid_kernel,
        grid=(b, s // tm),
        in_specs=[tok, tok, tok, full(wo), full(gkv), full(gb), full(wkT), full(wvT), full(gkn),
                  full(wqT), full(wgT), full(gqn),
                  pl.BlockSpec((1, tm), lambda bi, i: (0, i)), full(invf)],
        out_specs=[tok,
                   pl.BlockSpec((1, tm, KV_WIDTH), lambda bi, i: (bi, i, 0)),
                   pl.BlockSpec((1, KV_WIDTH, tm), lambda bi, i: (bi, 0, i)),
                   feat, feat],
        out_shape=[jax.ShapeDtypeStruct((b, s, d), F32),
                   jax.ShapeDtypeStruct((b, s, KV_WIDTH), BF16),
                   jax.ShapeDtypeStruct((b, KV_WIDTH, s), BF16),
                   jax.ShapeDtypeStruct((b, d, s), BF16),
                   jax.ShapeDtypeStruct((b, d, s), BF16)],
        compiler_params=pltpu.CompilerParams(
            dimension_semantics=("parallel", "parallel"), vmem_limit_bytes=VMEM_LIMIT),
        name="mid_proj",
    )(o, sg, x, wo, gkv, gb, wkT, wvT, gkn, wqT, wgT, gqn, pos, invf)


def _swa_kernel(sinks_ref, qT_ref, kb_ref, vT_ref, oT_ref, kmask_ref):
    s_len = qT_ref.shape[2]
    g = pl.program_id(1)
    w = WINDOW
    nq = GROUP * w

    k_all = kb_ref[0]
    lane_head = lax.shift_right_logical(
        lax.broadcasted_iota(jnp.int32, k_all.shape, 1), HEAD_DIM.bit_length() - 1)
    kmask_ref[...] = jnp.where(lane_head == g, k_all, jnp.zeros_like(k_all))

    q_head = lax.shift_right_logical(lax.broadcasted_iota(jnp.int32, (1, nq), 1), w.bit_length() - 1)
    sink = jnp.zeros((1, nq), F32)
    for i in range(GROUP):
        sink = jnp.where(q_head == i, sinks_ref[g * GROUP + i] * LOG2E, sink)

    def valid(nk):
        key = lax.broadcasted_iota(jnp.int32, (nk, nq), 0)
        qry = jnp.bitwise_and(lax.broadcasted_iota(jnp.int32, (nk, nq), 1), w - 1)
        if nk == w:
            return key <= qry
        return jnp.logical_and(key > qry, key <= qry + w)

    band = valid(2 * w)
    band0 = valid(w)
    n_blocks = s_len // w

    def logits(n):
        base = n * w
        q_rep = jnp.concatenate(
            [jnp.concatenate([qT_ref[0, i * HEAD_DIM:(i + 1) * HEAD_DIM, base:base + w]] * N_KV_HEADS,
                             axis=0) for i in range(GROUP)], axis=1)
        return jnp.where(band if n > 0 else band0,
                         _dot(kmask_ref[max(base - w, 0):base + w, :], q_rep), NEG)

    pending = [logits(n) for n in range(min(SWA_LOOKAHEAD, n_blocks))]
    for n in range(n_blocks):
        base = n * w
        k0 = max(base - w, 0)
        k1 = base + w
        sT = pending.pop(0)
        if n + SWA_LOOKAHEAD < n_blocks:
            pending.append(logits(n + SWA_LOOKAHEAD))
        m = jnp.maximum(jnp.max(sT, axis=0, keepdims=True), sink)
        p = jnp.exp2(sT - m)
        denom = jnp.sum(p, axis=0, keepdims=True) + jnp.exp2(sink - m)
        outT = _dot(vT_ref[0, :, k0:k1], p.astype(BF16)) * (1.0 / denom)
        for i in range(GROUP):
            oT_ref[0, i * HEAD_DIM:(i + 1) * HEAD_DIM, base:base + w] = (
                outT[:, i * w:(i + 1) * w].astype(BF16))


def _swa_attention(sinks, qT, kb, vT):
    b, d, s = qT.shape
    gw = GROUP * HEAD_DIM
    return pl.pallas_call(
        _swa_kernel,
        grid=(b, N_KV_HEADS),
        in_specs=[pl.BlockSpec(memory_space=pltpu.SMEM),
                  pl.BlockSpec((1, gw, s), lambda bi, g: (bi, g, 0)),
                  pl.BlockSpec((1, s, KV_WIDTH), lambda bi, g: (bi, 0, 0)),
                  pl.BlockSpec((1, HEAD_DIM, s), lambda bi, g: (bi, g, 0))],
        out_specs=pl.BlockSpec((1, gw, s), lambda bi, g: (bi, g, 0)),
        out_shape=jax.ShapeDtypeStruct((b, d, s), BF16),
        scratch_shapes=[pltpu.VMEM((s, KV_WIDTH), BF16)],
        compiler_params=pltpu.CompilerParams(
            dimension_semantics=("parallel", "parallel"), vmem_limit_bytes=VMEM_LIMIT),
        name="swa_attention",
    )(sinks, qT, kb, vT)


def _outproj_kernel(oT_ref, sgT_ref, h_ref, woT_ref, out_ref):
    ogT = (oT_ref[0].astype(F32) * sgT_ref[0].astype(F32)).astype(BF16)
    out_ref[0] = h_ref[0] + _dot(woT_ref[...], ogT).T


def _outproj(oT, sgT, h, woT):
    b, s, d = h.shape
    tm = ROW_TILE
    tok = pl.BlockSpec((1, tm, d), lambda bi, i: (bi, i, 0))
    feat = pl.BlockSpec((1, d, tm), lambda bi, i: (bi, 0, i))
    return pl.pallas_call(
        _outproj_kernel,
        grid=(b, s // tm),
        in_specs=[feat, feat, tok, pl.BlockSpec(woT.shape, lambda bi, i: (0, 0))],
        out_specs=tok,
        out_shape=jax.ShapeDtypeStruct((b, s, d), F32),
        compiler_params=pltpu.CompilerParams(
            dimension_semantics=("parallel", "parallel"), vmem_limit_bytes=VMEM_LIMIT),
        name="outproj_b",
    )(oT, sgT, h, woT)


def _gain_col(g, n_heads, scale=1.0):
    return (jnp.tile(g.astype(F32), n_heads) * scale).reshape(n_heads * HEAD_DIM, 1)


def kernel(x, positions, norm_a_g, w_in_a, b_forget, qnorm_a_g, knorm_a_g, w_out_a, kv_norm_g, w_kv,
           knorm_b_g, norm_b_g, w_in_b, qnorm_b_g, sinks, w_out_b):
    b, s, d = x.shape
    wa = w_in_a[0]
    wqkT = wa[:, :2 * d].T.astype(BF16)
    wvg = jnp.concatenate([wa[:, 2 * d:3 * d], wa[:, 3 * d + N_HEADS:]], axis=1).astype(BF16)
    wfT = wa[:, 3 * d:3 * d + N_HEADS].T.astype(BF16)
    q, kT, v, sg, cT = _inproj_a(
        x, norm_a_g[0].reshape(1, d), wqkT, wvg, wfT, b_forget[0].reshape(N_HEADS, 1),
        _gain_col(qnorm_a_g[0], N_HEADS, SCALE * LOG2E), _gain_col(knorm_a_g[0], N_HEADS))
    o = _fox_attention(q, kT, v, cT.reshape(b, N_HEADS // 2, 2, s))

    wb = w_in_b[0]
    inv_freq = jnp.power(jnp.float32(ROPE_THETA),
                         -jnp.arange(0, ROT_DIM, 2, dtype=F32) / ROT_DIM).reshape(ROT_HALF, 1)
    h, kb, vT, qT, sgT = _mid(
        o, sg, x, w_out_a[0].astype(BF16), kv_norm_g.reshape(1, d), norm_b_g[0].reshape(1, d),
        w_kv[:, :KV_WIDTH].T.astype(BF16), w_kv[:, KV_WIDTH:].T.astype(BF16),
        _gain_col(knorm_b_g, N_KV_HEADS),
        wb[:, :d].T.astype(BF16), wb[:, d:].T.astype(BF16),
        _gain_col(qnorm_b_g[0], N_HEADS, SCALE * LOG2E),
        positions.reshape(1, s), inv_freq)
    obT = _swa_attention(sinks[0], qT, kb, vT)
    return _outproj(obT, sgT, h, w_out_b[0].T.astype(BF16))
```

```python
import functools

import jax
import jax.numpy as jnp
from jax import lax
from jax.experimental import pallas as pl
from jax.experimental.pallas import tpu as pltpu

D_MODEL = 1024
HEAD_DIM = 64
N_HEADS = 16
N_KV_HEADS = 4
GROUP = N_HEADS // N_KV_HEADS
KV_WIDTH = N_KV_HEADS * HEAD_DIM
WINDOW = 128
ROT_DIM = HEAD_DIM // 4
ROT_HALF = ROT_DIM // 2
ROPE_THETA = 500000.0
EPS = 1e-6
SCALE = HEAD_DIM ** -0.5
LOG2E = 1.4426950408889634
NEG = -1e30

F32 = jnp.float32
BF16 = jnp.bfloat16

ROW_TILE = 512
FOX_Q_TILE = 256
FOX_PAIRS = 2
SWA_LOOKAHEAD = 2
SUM_ROWS = 16
VMEM_LIMIT = 56 * 1024 * 1024


def _dot(a, b):
    return jnp.dot(a, b, preferred_element_type=F32)


def _dot_nt(a, b):
    return lax.dot_general(a, b, (((1,), (1,)), ((), ())), preferred_element_type=F32)


def _head_norm_fm(t, gain_col):
    blocks = []
    for h in range(t.shape[0] // HEAD_DIM):
        blk = t[h * HEAD_DIM:(h + 1) * HEAD_DIM, :]
        ms = jnp.mean(blk * blk, axis=0, keepdims=True)
        blocks.append(blk * lax.rsqrt(ms + EPS))
    return jnp.concatenate(blocks, axis=0) * gain_col


def _rope_fm(t, cos, sin):
    blocks = []
    for h in range(t.shape[0] // HEAD_DIM):
        base = h * HEAD_DIM
        x1 = t[base:base + ROT_HALF, :]
        x2 = t[base + ROT_HALF:base + ROT_DIM, :]
        blocks.append(x1 * cos - x2 * sin)
        blocks.append(x1 * sin + x2 * cos)
        blocks.append(t[base + ROT_DIM:base + HEAD_DIM, :])
    return jnp.concatenate(blocks, axis=0)


def _split3(x):
    hi = x.astype(BF16)
    r1 = x - hi.astype(F32)
    mid = r1.astype(BF16)
    lo = (r1 - mid.astype(F32)).astype(BF16)
    return hi, mid, lo


def _inproj_a_kernel(x_ref, g_ref, wqkT_ref, wvg_ref, wfT_ref, bf_ref, gq_ref, gk_ref,
                     q_ref, kT_ref, v_ref, sg_ref, cT_ref, carry_ref):
    tm = x_ref.shape[1]

    @pl.when(pl.program_id(1) == 0)
    def _():
        carry_ref[...] = jnp.zeros_like(carry_ref)

    x = x_ref[0]
    ms = jnp.mean(x * x, axis=-1, keepdims=True)
    u = (x * lax.rsqrt(ms + EPS) * g_ref[...]).astype(BF16)

    qkT = _dot_nt(wqkT_ref[...], u)
    qT = _head_norm_fm(qkT[:D_MODEL], gq_ref[...])
    kT = _head_norm_fm(qkT[D_MODEL:], gk_ref[...])
    q_ref[0] = qT.T.astype(BF16)
    kT_ref[0] = kT.astype(BF16)

    vg = _dot(u, wvg_ref[...])
    v_ref[0] = vg[:, :D_MODEL].astype(BF16)
    gate = vg[:, D_MODEL:]
    sg_ref[0] = (gate * jax.nn.sigmoid(gate)).astype(BF16)

    f = _dot_nt(wfT_ref[...], u) + bf_ref[...]
    log_f = jnp.minimum(f, 0.0) - jnp.log1p(jnp.exp(-jnp.abs(f)))
    row = lax.broadcasted_iota(jnp.int32, (tm, tm), 0)
    col = lax.broadcasted_iota(jnp.int32, (tm, tm), 1)
    tri = (row <= col).astype(BF16)
    hi, mid, lo = _split3(log_f)
    c = (_dot(hi, tri) + _dot(mid, tri) + _dot(lo, tri)) + carry_ref[:, 0:1]
    cT_ref[0] = c * LOG2E
    carry_ref[...] = jnp.broadcast_to(c[:, tm - 1:tm], carry_ref.shape)


def _inproj_a(x, g, wqkT, wvg, wfT, bf, gq, gk):
    b, s, d = x.shape
    tm = ROW_TILE
    full = lambda shape: pl.BlockSpec(shape, lambda bi, i: (0,) * len(shape))
    tok = pl.BlockSpec((1, tm, d), lambda bi, i: (bi, i, 0))
    return pl.pallas_call(
        _inproj_a_kernel,
        grid=(b, s // tm),
        in_specs=[tok, full(g.shape), full(wqkT.shape), full(wvg.shape), full(wfT.shape),
                  full(bf.shape), full(gq.shape), full(gk.shape)],
        out_specs=[tok,
                   pl.BlockSpec((1, d, tm), lambda bi, i: (bi, 0, i)),
                   tok, tok,
                   pl.BlockSpec((1, N_HEADS, tm), lambda bi, i: (bi, 0, i))],
        out_shape=[jax.ShapeDtypeStruct((b, s, d), BF16),
                   jax.ShapeDtypeStruct((b, d, s), BF16),
                   jax.ShapeDtypeStruct((b, s, d), BF16),
                   jax.ShapeDtypeStruct((b, s, d), BF16),
                   jax.ShapeDtypeStruct((b, N_HEADS, s), F32)],
        scratch_shapes=[pltpu.VMEM((N_HEADS, 128), F32)],
        compiler_params=pltpu.CompilerParams(
            dimension_semantics=("parallel", "arbitrary"), vmem_limit_bytes=VMEM_LIMIT),
        name="inproj_a",
    )(x, g, wqkT, wvg, wfT, bf, gq, gk)


BIAS_ROWS = 16


def _fox_kernel(q_ref, kT_ref, v_ref, c_ref, o_ref, bias_ref, vext_ref):
    s_len = q_ref.shape[1]
    n_pairs = q_ref.shape[2] // 128
    tq = FOX_Q_TILE
    lane = lax.broadcasted_iota(jnp.int32, (tq, 128), 1)
    first = lane < HEAD_DIM
    ones_a = jnp.where(lane < 3, 1.0, 0.0).astype(BF16)
    ones_b = jnp.where(jnp.logical_and(lane >= 3, lane < 6), 1.0, 0.0).astype(BF16)
    row = lax.broadcasted_iota(jnp.int32, (2 * tq, tq), 0) % tq
    col = lax.broadcasted_iota(jnp.int32, (2 * tq, tq), 1)
    causal = col <= row

    brow = lax.broadcasted_iota(jnp.int32, (BIAS_ROWS, s_len), 0)
    for p in range(n_pairs):
        parts = _split3(-c_ref[0, p, 0:1, :]) + _split3(-c_ref[0, p, 1:2, :])
        bias = jnp.zeros((BIAS_ROWS, s_len), F32)
        for r, part in enumerate(parts):
            bias = jnp.where(brow == r, part.astype(F32), bias)
        bias_ref[p] = bias.astype(BF16)
        vext_ref[p, :, :128] = v_ref[0, :, p * 128:(p + 1) * 128]
        vext_ref[p, :, 128:] = jnp.ones((s_len, 128), BF16)

    def rhs(p, c0, c1):
        pad = jnp.zeros((128 - BIAS_ROWS, c1 - c0), BF16)
        return jnp.concatenate(
            [kT_ref[0, p * 128:(p + 1) * 128, c0:c1], bias_ref[p, :, c0:c1], pad], axis=0)

    def logits(item):
        p, qb = item
        r0 = qb * tq
        q2 = q_ref[0, r0:r0 + tq, p * 128:(p + 1) * 128]
        zero = jnp.zeros_like(q2)
        lhs = jnp.concatenate(
            [jnp.concatenate([jnp.where(first, q2, zero), ones_a], axis=1),
             jnp.concatenate([jnp.where(first, zero, q2), ones_b], axis=1)], axis=0)
        s_diag = jnp.where(causal, _dot(lhs, rhs(p, r0, r0 + tq)), NEG)
        s_off = _dot(lhs, rhs(p, 0, r0)) if qb > 0 else None
        return s_diag, s_off

    order = [(p, qb) for qb in reversed(range(s_len // tq)) for p in range(n_pairs)]
    pending = logits(order[0])
    for idx, (p, qb) in enumerate(order):
        r0 = qb * tq
        s_diag, s_off = pending
        if idx + 1 < len(order):
            pending = logits(order[idx + 1])
        m = jnp.max(s_diag, axis=1, keepdims=True)
        if qb > 0:
            m = jnp.maximum(m, jnp.max(s_off, axis=1, keepdims=True))
        pv = _dot(jnp.exp2(s_diag - m).astype(BF16), vext_ref[p, r0:r0 + tq, :])
        if qb > 0:
            pv = pv + _dot(jnp.exp2(s_off - m).astype(BF16), vext_ref[p, :r0, :])
        o = pv[:, :128] / pv[:, 128:]
        o_ref[0, r0:r0 + tq, p * 128:(p + 1) * 128] = jnp.where(first, o[:tq], o[tq:]).astype(BF16)


def _fox_attention(q, kT, v, c4):
    b, s, d = q.shape
    np_, wd = FOX_PAIRS, FOX_PAIRS * 128
    return pl.pallas_call(
        _fox_kernel,
        grid=(b, d // wd),
        in_specs=[pl.BlockSpec((1, s, wd), lambda bi, j: (bi, 0, j)),
                  pl.BlockSpec((1, wd, s), lambda bi, j: (bi, j, 0)),
                  pl.BlockSpec((1, s, wd), lambda bi, j: (bi, 0, j)),
                  pl.BlockSpec((1, np_, 2, s), lambda bi, j: (bi, j, 0, 0))],
        out_specs=pl.BlockSpec((1, s, wd), lambda bi, j: (bi, 0, j)),
        out_shape=jax.ShapeDtypeStruct((b, s, d), BF16),
        scratch_shapes=[pltpu.VMEM((np_, BIAS_ROWS, s), BF16), pltpu.VMEM((np_, s, 256), BF16)],
        compiler_params=pltpu.CompilerParams(
            dimension_semantics=("parallel", "parallel"), vmem_limit_bytes=VMEM_LIMIT),
        name="fox_attention",
    )(q, kT, v, c4)


def _mid_kernel(o_ref, sg_ref, x_ref, wo_ref, gkv_ref, gb_ref, wkT_ref, wvT_ref, gkn_ref,
                wqT_ref, wgT_ref, gqn_ref, pos_ref, invf_ref,
                h_ref, kb_ref, vT_ref, qT_ref, sgT_ref):
    og = (o_ref[0].astype(F32) * sg_ref[0].astype(F32)).astype(BF16)
    h = x_ref[0] + _dot(og, wo_ref[...])
    h_ref[0] = h

    ms = jnp.mean(h * h, axis=-1, keepdims=True)
    hn = h * lax.rsqrt(ms + EPS)
    u_kv = (hn * gkv_ref[...]).astype(BF16)
    u_b = (hn * gb_ref[...]).astype(BF16)

    ang = invf_ref[...] * pos_ref[...].astype(F32)
    cos = jnp.cos(ang)
    sin = jnp.sin(ang)

    kT = _head_norm_fm(_dot_nt(wkT_ref[...], u_kv), gkn_ref[...])
    kb_ref[0] = _rope_fm(kT, cos, sin).T.astype(BF16)
    vT_ref[0] = _dot_nt(wvT_ref[...], u_kv).astype(BF16)

    qT = _head_norm_fm(_dot_nt(wqT_ref[...], u_b), gqn_ref[...])
    qT_ref[0] = _rope_fm(qT, cos, sin).astype(BF16)

    gateT = _dot_nt(wgT_ref[...], u_b)
    sgT_ref[0] = (gateT * jax.nn.sigmoid(gateT)).astype(BF16)


def _mid(o, sg, x, wo, gkv, gb, wkT, wvT, gkn, wqT, wgT, gqn, pos, invf):
    b, s, d = x.shape
    tm = ROW_TILE
    full = lambda a: pl.BlockSpec(a.shape, lambda bi, i: (0,) * a.ndim)
    tok = pl.BlockSpec((1, tm, d), lambda bi, i: (bi, i, 0))
    feat = pl.BlockSpec((1, d, tm), lambda bi, i: (bi, 0, i))
    return pl.pallas_call(
        _mid_kernel,
        grid=(b, s // tm),
        in_specs=[tok, tok, tok, full(wo), full(gkv), full(gb), full(wkT), full(wvT), full(gkn),
                  full(wqT), full(wgT), full(gqn),
                  pl.BlockSpec((1, tm), lambda bi, i: (0, i)), full(invf)],
        out_specs=[tok,
                   pl.BlockSpec((1, tm, KV_WIDTH), lambda bi, i: (bi, i, 0)),
                   pl.BlockSpec((1, KV_WIDTH, tm), lambda bi, i: (bi, 0, i)),
                   feat, feat],
        out_shape=[jax.ShapeDtypeStruct((b, s, d), F32),
                   jax.ShapeDtypeStruct((b, s, KV_WIDTH), BF16),
                   jax.ShapeDtypeStruct((b, KV_WIDTH, s), BF16),
                   jax.ShapeDtypeStruct((b, d, s), BF16),
                   jax.ShapeDtypeStruct((b, d, s), BF16)],
        compiler_params=pltpu.CompilerParams(
            dimension_semantics=("parallel", "parallel"), vmem_limit_bytes=VMEM_LIMIT),
        name="mid_proj",
    )(o, sg, x, wo, gkv, gb, wkT, wvT, gkn, wqT, wgT, gqn, pos, invf)


def _swa_kernel(sinks_ref, qT_ref, kb_ref, vT_ref, oT_ref, kmask_ref):
    s_len = qT_ref.shape[2]
    g = pl.program_id(1)
    w = WINDOW
    nq = GROUP * w

    k_all = kb_ref[0]
    lane_head = lax.shift_right_logical(
        lax.broadcasted_iota(jnp.int32, k_all.shape, 1), HEAD_DIM.bit_length() - 1)
    kmask_ref[...] = jnp.where(lane_head == g, k_all, jnp.zeros_like(k_all))

    q_head = lax.shift_right_logical(lax.broadcasted_iota(jnp.int32, (1, nq), 1), w.bit_length() - 1)
    sink = jnp.zeros((1, nq), F32)
    for i in range(GROUP):
        sink = jnp.where(q_head == i, sinks_ref[g * GROUP + i] * LOG2E, sink)

    key = lax.broadcasted_iota(jnp.int32, (w, nq), 0)
    qry = jnp.bitwise_and(lax.broadcasted_iota(jnp.int32, (w, nq), 1), w - 1)
    own = key <= qry
    ones_rows = jnp.ones((SUM_ROWS, 2 * w), BF16)
    n_blocks = s_len // w

    def logits(n):
        base = n * w
        q_rep = jnp.concatenate(
            [jnp.concatenate([qT_ref[0, i * HEAD_DIM:(i + 1) * HEAD_DIM, base:base + w]] * N_KV_HEADS,
                             axis=0) for i in range(GROUP)], axis=1)
        s_all = _dot(kmask_ref[max(base - w, 0):base + w, :], q_rep)
        if n == 0:
            return jnp.where(own, s_all, NEG)
        return jnp.where(own, s_all[w:], s_all[:w])

    pending = [logits(n) for n in range(min(SWA_LOOKAHEAD, n_blocks))]
    for n in range(n_blocks):
        base = n * w
        k0 = max(base - w, 0)
        k1 = base + w
        sT = pending.pop(0)
        if n + SWA_LOOKAHEAD < n_blocks:
            pending.append(logits(n + SWA_LOOKAHEAD))
        m = jnp.maximum(jnp.max(sT, axis=0, keepdims=True), sink)
        p = jnp.exp2(sT - m).astype(BF16)
        if n > 0:
            zero = jnp.zeros_like(p)
            p = jnp.concatenate([jnp.where(own, zero, p), jnp.where(own, p, zero)], axis=0)
        v_ext = jnp.concatenate([vT_ref[0, :, k0:k1], ones_rows[:, :k1 - k0]], axis=0)
        out_ext = _dot(v_ext, p)
        denom = out_ext[HEAD_DIM:HEAD_DIM + 1] + jnp.exp2(sink - m)
        outT = out_ext[:HEAD_DIM] * (1.0 / denom)
        for i in range(GROUP):
            oT_ref[0, i * HEAD_DIM:(i + 1) * HEAD_DIM, base:base + w] = (
                outT[:, i * w:(i + 1) * w].astype(BF16))


def _swa_attention(sinks, qT, kb, vT):
    b, d, s = qT.shape
    gw = GROUP * HEAD_DIM
    return pl.pallas_call(
        _swa_kernel,
        grid=(b, N_KV_HEADS),
        in_specs=[pl.BlockSpec(memory_space=pltpu.SMEM),
                  pl.BlockSpec((1, gw, s), lambda bi, g: (bi, g, 0)),
                  pl.BlockSpec((1, s, KV_WIDTH), lambda bi, g: (bi, 0, 0)),
                  pl.BlockSpec((1, HEAD_DIM, s), lambda bi, g: (bi, g, 0))],
        out_specs=pl.BlockSpec((1, gw, s), lambda bi, g: (bi, g, 0)),
        out_shape=jax.ShapeDtypeStruct((b, d, s), BF16),
        scratch_shapes=[pltpu.VMEM((s, KV_WIDTH), BF16)],
        compiler_params=pltpu.CompilerParams(
            dimension_semantics=("parallel", "parallel"), vmem_limit_bytes=VMEM_LIMIT),
        name="swa_attention",
    )(sinks, qT, kb, vT)


def _outproj_kernel(oT_ref, sgT_ref, h_ref, woT_ref, out_ref):
    ogT = (oT_ref[0].astype(F32) * sgT_ref[0].astype(F32)).astype(BF16)
    out_ref[0] = h_ref[0] + _dot(woT_ref[...], ogT).T


def _outproj(oT, sgT, h, woT):
    b, s, d = h.shape
    tm = ROW_TILE
    tok = pl.BlockSpec((1, tm, d), lambda bi, i: (bi, i, 0))
    feat = pl.BlockSpec((1, d, tm), lambda bi, i: (bi, 0, i))
    return pl.pallas_call(
        _outproj_kernel,
        grid=(b, s // tm),
        in_specs=[feat, feat, tok, pl.BlockSpec(woT.shape, lambda bi, i: (0, 0))],
        out_specs=tok,
        out_shape=jax.ShapeDtypeStruct((b, s, d), F32),
        compiler_params=pltpu.CompilerParams(
            dimension_semantics=("parallel", "parallel"), vmem_limit_bytes=VMEM_LIMIT),
        name="outproj_b",
    )(oT, sgT, h, woT)


def _gain_col(g, n_heads, scale=1.0):
    return (jnp.tile(g.astype(F32), n_heads) * scale).reshape(n_heads * HEAD_DIM, 1)


def kernel(x, positions, norm_a_g, w_in_a, b_forget, qnorm_a_g, knorm_a_g, w_out_a, kv_norm_g, w_kv,
           knorm_b_g, norm_b_g, w_in_b, qnorm_b_g, sinks, w_out_b):
    b, s, d = x.shape
    wa = w_in_a[0]
    wqkT = wa[:, :2 * d].T.astype(BF16)
    wvg = jnp.concatenate([wa[:, 2 * d:3 * d], wa[:, 3 * d + N_HEADS:]], axis=1).astype(BF16)
    wfT = wa[:, 3 * d:3 * d + N_HEADS].T.astype(BF16)
    q, kT, v, sg, cT = _inproj_a(
        x, norm_a_g[0].reshape(1, d), wqkT, wvg, wfT, b_forget[0].reshape(N_HEADS, 1),
        _gain_col(qnorm_a_g[0], N_HEADS, SCALE * LOG2E), _gain_col(knorm_a_g[0], N_HEADS))
    o = _fox_attention(q, kT, v, cT.reshape(b, N_HEADS // 2, 2, s))

    wb = w_in_b[0]
    inv_freq = jnp.power(jnp.float32(ROPE_THETA),
                         -jnp.arange(0, ROT_DIM, 2, dtype=F32) / ROT_DIM).reshape(ROT_HALF, 1)
    h, kb, vT, qT, sgT = _mid(
        o, sg, x, w_out_a[0].astype(BF16), kv_norm_g.reshape(1, d), norm_b_g[0].reshape(1, d),
        w_kv[:, :KV_WIDTH].T.astype(BF16), w_kv[:, KV_WIDTH:].T.astype(BF16),
        _gain_col(knorm_b_g, N_KV_HEADS),
        wb[:, :d].T.astype(BF16), wb[:, d:].T.astype(BF16),
        _gain_col(qnorm_b_g[0], N_HEADS, SCALE * LOG2E),
        positions.reshape(1, s), inv_freq)
    obT = _swa_attention(sinks[0], qT, kb, vT)
    return _outproj(obT, sgT, h, w_out_b[0].T.astype(BF16))
```

```python
import jax
import jax.numpy as jnp
from jax import lax
from jax.experimental import pallas as pl
from jax.experimental.pallas import tpu as pltpu

D_MODEL = 1024
HEAD_DIM = 64
N_HEADS = 16
N_KV_HEADS = 4
GROUP = N_HEADS // N_KV_HEADS
KV_WIDTH = N_KV_HEADS * HEAD_DIM
WINDOW = 128
ROT_DIM = HEAD_DIM // 4
ROT_HALF = ROT_DIM // 2
ROPE_THETA = 500000.0
EPS = 1e-6
SCALE = HEAD_DIM ** -0.5
LOG2E = 1.4426950408889634
NEG = -1e30

F32 = jnp.float32
BF16 = jnp.bfloat16

ROW_TILE = 512
FOX_Q_TILE = 256
FOX_PAIRS = 2
SWA_LOOKAHEAD = 2
SUM_ROWS = 16
VMEM_LIMIT = 56 * 1024 * 1024


def _dot(a, b):
    return jnp.dot(a, b, preferred_element_type=F32)


def _dot_nt(a, b):
    return lax.dot_general(a, b, (((1,), (1,)), ((), ())), preferred_element_type=F32)


def _head_norm_fm(t, gain_col):
    blocks = []
    for h in range(t.shape[0] // HEAD_DIM):
        blk = t[h * HEAD_DIM:(h + 1) * HEAD_DIM, :]
        ms = jnp.mean(blk * blk, axis=0, keepdims=True)
        blocks.append(blk * lax.rsqrt(ms + EPS))
    return jnp.concatenate(blocks, axis=0) * gain_col


def _rope_fm(t, cos, sin):
    blocks = []
    for h in range(t.shape[0] // HEAD_DIM):
        base = h * HEAD_DIM
        x1 = t[base:base + ROT_HALF, :]
        x2 = t[base + ROT_HALF:base + ROT_DIM, :]
        blocks.append(x1 * cos - x2 * sin)
        blocks.append(x1 * sin + x2 * cos)
        blocks.append(t[base + ROT_DIM:base + HEAD_DIM, :])
    return jnp.concatenate(blocks, axis=0)


def _split3(x):
    hi = x.astype(BF16)
    r1 = x - hi.astype(F32)
    mid = r1.astype(BF16)
    lo = (r1 - mid.astype(F32)).astype(BF16)
    return hi, mid, lo


def _inproj_a_kernel(x_ref, g_ref, wqkT_ref, wvg_ref, wfT_ref, bf_ref, gq_ref, gk_ref,
                     q_ref, kT_ref, v_ref, sg_ref, cT_ref, carry_ref):
    tm = x_ref.shape[1]

    @pl.when(pl.program_id(1) == 0)
    def _():
        carry_ref[...] = jnp.zeros_like(carry_ref)

    x = x_ref[0]
    ms = jnp.mean(x * x, axis=-1, keepdims=True)
    u = (x * lax.rsqrt(ms + EPS) * g_ref[...]).astype(BF16)

    qkT = _dot_nt(wqkT_ref[...], u)
    qT = _head_norm_fm(qkT[:D_MODEL], gq_ref[...])
    kT = _head_norm_fm(qkT[D_MODEL:], gk_ref[...])
    q_ref[0] = qT.T.astype(BF16)
    kT_ref[0] = kT.astype(BF16)

    vg = _dot(u, wvg_ref[...])
    v_ref[0] = vg[:, :D_MODEL].astype(BF16)
    gate = vg[:, D_MODEL:]
    sg_ref[0] = (gate * jax.nn.sigmoid(gate)).astype(BF16)

    f = _dot_nt(wfT_ref[...], u) + bf_ref[...]
    log_f = jnp.minimum(f, 0.0) - jnp.log1p(jnp.exp(-jnp.abs(f)))
    row = lax.broadcasted_iota(jnp.int32, (tm, tm), 0)
    col = lax.broadcasted_iota(jnp.int32, (tm, tm), 1)
    tri = (row <= col).astype(BF16)
    hi, mid, lo = _split3(log_f)
    c = (_dot(hi, tri) + _dot(mid, tri) + _dot(lo, tri)) + carry_ref[:, 0:1]
    cT_ref[0] = c * LOG2E
    carry_ref[...] = jnp.broadcast_to(c[:, tm - 1:tm], carry_ref.shape)


def _inproj_a(x, g, wqkT, wvg, wfT, bf, gq, gk):
    b, s, d = x.shape
    tm = ROW_TILE
    full = lambda shape: pl.BlockSpec(shape, lambda bi, i: (0,) * len(shape))
    tok = pl.BlockSpec((1, tm, d), lambda bi, i: (bi, i, 0))
    return pl.pallas_call(
        _inproj_a_kernel,
        grid=(b, s // tm),
        in_specs=[tok, full(g.shape), full(wqkT.shape), full(wvg.shape), full(wfT.shape),
                  full(bf.shape), full(gq.shape), full(gk.shape)],
        out_specs=[tok,
                   pl.BlockSpec((1, d, tm), lambda bi, i: (bi, 0, i)),
                   tok, tok,
                   pl.BlockSpec((1, N_HEADS, tm), lambda bi, i: (bi, 0, i))],
        out_shape=[jax.ShapeDtypeStruct((b, s, d), BF16),
                   jax.ShapeDtypeStruct((b, d, s), BF16),
                   jax.ShapeDtypeStruct((b, s, d), BF16),
                   jax.ShapeDtypeStruct((b, s, d), BF16),
                   jax.ShapeDtypeStruct((b, N_HEADS, s), F32)],
        scratch_shapes=[pltpu.VMEM((N_HEADS, 128), F32)],
        compiler_params=pltpu.CompilerParams(
            dimension_semantics=("parallel", "arbitrary"), vmem_limit_bytes=VMEM_LIMIT),
        name="inproj_a",
    )(x, g, wqkT, wvg, wfT, bf, gq, gk)


BIAS_ROWS = 16


def _fox_kernel(q_ref, kT_ref, v_ref, c_ref, o_ref, bias_ref, vext_ref):
    s_len = q_ref.shape[1]
    n_pairs = q_ref.shape[2] // 128
    tq = FOX_Q_TILE
    lane = lax.broadcasted_iota(jnp.int32, (tq, 128), 1)
    first = lane < HEAD_DIM
    ones_a = jnp.where(lane < 3, 1.0, 0.0).astype(BF16)
    ones_b = jnp.where(jnp.logical_and(lane >= 3, lane < 6), 1.0, 0.0).astype(BF16)
    row = lax.broadcasted_iota(jnp.int32, (2 * tq, tq), 0) % tq
    col = lax.broadcasted_iota(jnp.int32, (2 * tq, tq), 1)
    causal = col <= row

    brow = lax.broadcasted_iota(jnp.int32, (BIAS_ROWS, s_len), 0)
    for p in range(n_pairs):
        parts = _split3(-c_ref[0, p, 0:1, :]) + _split3(-c_ref[0, p, 1:2, :])
        bias = jnp.zeros((BIAS_ROWS, s_len), F32)
        for r, part in enumerate(parts):
            bias = jnp.where(brow == r, part.astype(F32), bias)
        bias_ref[p] = bias.astype(BF16)
        vext_ref[p, :, :128] = v_ref[0, :, p * 128:(p + 1) * 128]
        vext_ref[p, :, 128:] = jnp.ones((s_len, 128), BF16)

    def rhs(p, c0, c1):
        pad = jnp.zeros((128 - BIAS_ROWS, c1 - c0), BF16)
        return jnp.concatenate(
            [kT_ref[0, p * 128:(p + 1) * 128, c0:c1], bias_ref[p, :, c0:c1], pad], axis=0)

    def logits(item):
        p, qb = item
        r0 = qb * tq
        q2 = q_ref[0, r0:r0 + tq, p * 128:(p + 1) * 128]
        zero = jnp.zeros_like(q2)
        lhs = jnp.concatenate(
            [jnp.concatenate([jnp.where(first, q2, zero), ones_a], axis=1),
             jnp.concatenate([jnp.where(first, zero, q2), ones_b], axis=1)], axis=0)
        s_diag = jnp.where(causal, _dot(lhs, rhs(p, r0, r0 + tq)), NEG)
        s_off = _dot(lhs, rhs(p, 0, r0)) if qb > 0 else None
        return s_diag, s_off

    order = [(p, qb) for qb in reversed(range(s_len // tq)) for p in range(n_pairs)]
    pending = logits(order[0])
    for idx, (p, qb) in enumerate(order):
        r0 = qb * tq
        s_diag, s_off = pending
        if idx + 1 < len(order):
            pending = logits(order[idx + 1])
        m = jnp.max(s_diag, axis=1, keepdims=True)
        if qb > 0:
            m = jnp.maximum(m, jnp.max(s_off, axis=1, keepdims=True))
        pv = _dot(jnp.exp2(s_diag - m).astype(BF16), vext_ref[p, r0:r0 + tq, :])
        if qb > 0:
            pv = pv + _dot(jnp.exp2(s_off - m).astype(BF16), vext_ref[p, :r0, :])
        o = pv[:, :128] / pv[:, 128:]
        o_ref[0, r0:r0 + tq, p * 128:(p + 1) * 128] = jnp.where(first, o[:tq], o[tq:]).astype(BF16)


def _fox_attention(q, kT, v, c4):
    b, s, d = q.shape
    np_, wd = FOX_PAIRS, FOX_PAIRS * 128
    return pl.pallas_call(
        _fox_kernel,
        grid=(b, d // wd),
        in_specs=[pl.BlockSpec((1, s, wd), lambda bi, j: (bi, 0, j)),
                  pl.BlockSpec((1, wd, s), lambda bi, j: (bi, j, 0)),
                  pl.BlockSpec((1, s, wd), lambda bi, j: (bi, 0, j)),
                  pl.BlockSpec((1, np_, 2, s), lambda bi, j: (bi, j, 0, 0))],
        out_specs=pl.BlockSpec((1, s, wd), lambda bi, j: (bi, 0, j)),
        out_shape=jax.ShapeDtypeStruct((b, s, d), BF16),
        scratch_shapes=[pltpu.VMEM((np_, BIAS_ROWS, s), BF16), pltpu.VMEM((np_, s, 256), BF16)],
        compiler_params=pltpu.CompilerParams(
            dimension_semantics=("parallel", "parallel"), vmem_limit_bytes=VMEM_LIMIT),
        name="fox_attention",
    )(q, kT, v, c4)


def _layer_b_kernel(sinks_ref, o_ref, sg_ref, x_ref, wo_ref, gkv_ref, gb_ref, wkT_ref, wvT_ref,
                    gkn_ref, wqT_ref, wgT_ref, gqn_ref, pos_ref, invf_ref, woT_ref,
                    out_ref, kext_ref, vext_ref, qT_ref, sgT_ref, oT_ref):
    tm = x_ref.shape[1]
    w = WINDOW
    nq = GROUP * w
    first_tile = pl.program_id(1) == 0

    og = (o_ref[0].astype(F32) * sg_ref[0].astype(F32)).astype(BF16)
    h = x_ref[0] + _dot(og, wo_ref[...])
    out_ref[0] = h

    ms = jnp.mean(h * h, axis=-1, keepdims=True)
    hn = h * lax.rsqrt(ms + EPS)
    u_kv = (hn * gkv_ref[...]).astype(BF16)
    u_b = (hn * gb_ref[...]).astype(BF16)

    ang = invf_ref[...] * pos_ref[...].astype(F32)
    cos = jnp.cos(ang)
    sin = jnp.sin(ang)

    @pl.when(first_tile)
    def _():
        kext_ref[0:w, :] = jnp.zeros((w, KV_WIDTH), BF16)
        vext_ref[:, 0:w] = jnp.zeros((KV_WIDTH, w), BF16)

    @pl.when(jnp.logical_not(first_tile))
    def _():
        kext_ref[0:w, :] = kext_ref[tm:tm + w, :]
        vext_ref[:, 0:w] = vext_ref[:, tm:tm + w]

    kT = _head_norm_fm(_dot_nt(wkT_ref[...], u_kv), gkn_ref[...])
    kext_ref[w:, :] = _rope_fm(kT, cos, sin).T.astype(BF16)
    vext_ref[:, w:] = _dot_nt(wvT_ref[...], u_kv).astype(BF16)

    qT = _head_norm_fm(_dot_nt(wqT_ref[...], u_b), gqn_ref[...])
    qT_ref[...] = _rope_fm(qT, cos, sin).astype(BF16)
    gateT = _dot_nt(wgT_ref[...], u_b)
    sgT_ref[...] = (gateT * jax.nn.sigmoid(gateT)).astype(BF16)

    lane_head = lax.shift_right_logical(
        lax.broadcasted_iota(jnp.int32, (2 * w, KV_WIDTH), 1), HEAD_DIM.bit_length() - 1)
    q_head = lax.shift_right_logical(lax.broadcasted_iota(jnp.int32, (1, nq), 1), w.bit_length() - 1)
    sinks = []
    for g in range(N_KV_HEADS):
        sink = jnp.zeros((1, nq), F32)
        for i in range(GROUP):
            sink = jnp.where(q_head == i, sinks_ref[g * GROUP + i] * LOG2E, sink)
        sinks.append(sink)
    key = lax.broadcasted_iota(jnp.int32, (w, nq), 0)
    qry = jnp.bitwise_and(lax.broadcasted_iota(jnp.int32, (w, nq), 1), w - 1)
    own = key <= qry
    ones_rows = jnp.ones((SUM_ROWS, 2 * w), BF16)

    def logits(item):
        g, n = item
        k2 = kext_ref[n * w:(n + 2) * w, :]
        k_band = jnp.where(lane_head == g, k2, jnp.zeros_like(k2))
        q_rep = jnp.concatenate(
            [jnp.concatenate(
                [qT_ref[(g * GROUP + i) * HEAD_DIM:(g * GROUP + i + 1) * HEAD_DIM, n * w:(n + 1) * w]]
                * N_KV_HEADS, axis=0) for i in range(GROUP)], axis=1)
        s_all = _dot(k_band, q_rep)
        s_prev = s_all[:w]
        if n == 0:
            s_prev = jnp.where(first_tile, NEG, s_prev)
        return jnp.where(own, s_all[w:], s_prev)

    items = [(g, n) for n in range(tm // w) for g in range(N_KV_HEADS)]
    pending = [logits(it) for it in items[:SWA_LOOKAHEAD]]
    for idx, (g, n) in enumerate(items):
        sT = pending.pop(0)
        if idx + SWA_LOOKAHEAD < len(items):
            pending.append(logits(items[idx + SWA_LOOKAHEAD]))
        m = jnp.maximum(jnp.max(sT, axis=0, keepdims=True), sinks[g])
        p = jnp.exp2(sT - m).astype(BF16)
        zero = jnp.zeros_like(p)
        p2 = jnp.concatenate([jnp.where(own, zero, p), jnp.where(own, p, zero)], axis=0)
        v_ext = jnp.concatenate(
            [vext_ref[g * HEAD_DIM:(g + 1) * HEAD_DIM, n * w:(n + 2) * w], ones_rows], axis=0)
        out_ext = _dot(v_ext, p2)
        denom = out_ext[HEAD_DIM:HEAD_DIM + 1] + jnp.exp2(sinks[g] - m)
        outT = out_ext[:HEAD_DIM] * (1.0 / denom)
        for i in range(GROUP):
            r0 = (g * GROUP + i) * HEAD_DIM
            oT_ref[r0:r0 + HEAD_DIM, n * w:(n + 1) * w] = outT[:, i * w:(i + 1) * w].astype(BF16)

    ogT = (oT_ref[...].astype(F32) * sgT_ref[...].astype(F32)).astype(BF16)
    out_ref[0] = out_ref[0] + _dot(woT_ref[...], ogT).T


def _layer_b(sinks, o, sg, x, wo, gkv, gb, wkT, wvT, gkn, wqT, wgT, gqn, pos, invf, woT):
    b, s, d = x.shape
    tm = ROW_TILE
    full = lambda a: pl.BlockSpec(a.shape, lambda bi, i: (0,) * a.ndim)
    tok = pl.BlockSpec((1, tm, d), lambda bi, i: (bi, i, 0))
    return pl.pallas_call(
        _layer_b_kernel,
        grid=(b, s // tm),
        in_specs=[pl.BlockSpec(memory_space=pltpu.SMEM),
                  tok, tok, tok, full(wo), full(gkv), full(gb), full(wkT), full(wvT), full(gkn),
                  full(wqT), full(wgT), full(gqn),
                  pl.BlockSpec((1, tm), lambda bi, i: (0, i)), full(invf), full(woT)],
        out_specs=tok,
        out_shape=jax.ShapeDtypeStruct((b, s, d), F32),
        scratch_shapes=[pltpu.VMEM((tm + WINDOW, KV_WIDTH), BF16),
                        pltpu.VMEM((KV_WIDTH, tm + WINDOW), BF16),
                        pltpu.VMEM((d, tm), BF16),
                        pltpu.VMEM((d, tm), BF16),
                        pltpu.VMEM((d, tm), BF16)],
        compiler_params=pltpu.CompilerParams(
            dimension_semantics=("parallel", "arbitrary"), vmem_limit_bytes=VMEM_LIMIT),
        name="layer_b",
    )(sinks, o, sg, x, wo, gkv, gb, wkT, wvT, gkn, wqT, wgT, gqn, pos, invf, woT)


def _gain_col(g, n_heads, scale=1.0):
    return (jnp.tile(g.astype(F32), n_heads) * scale).reshape(n_heads * HEAD_DIM, 1)


def kernel(x, positions, norm_a_g, w_in_a, b_forget, qnorm_a_g, knorm_a_g, w_out_a, kv_norm_g, w_kv,
           knorm_b_g, norm_b_g, w_in_b, qnorm_b_g, sinks, w_out_b):
    b, s, d = x.shape
    wa = w_in_a[0]
    wqkT = wa[:, :2 * d].T.astype(BF16)
    wvg = jnp.concatenate([wa[:, 2 * d:3 * d], wa[:, 3 * d + N_HEADS:]], axis=1).astype(BF16)
    wfT = wa[:, 3 * d:3 * d + N_HEADS].T.astype(BF16)
    q, kT, v, sg, cT = _inproj_a(
        x, norm_a_g[0].reshape(1, d), wqkT, wvg, wfT, b_forget[0].reshape(N_HEADS, 1),
        _gain_col(qnorm_a_g[0], N_HEADS, SCALE * LOG2E), _gain_col(knorm_a_g[0], N_HEADS))
    o = _fox_attention(q, kT, v, cT.reshape(b, N_HEADS // 2, 2, s))

    wb = w_in_b[0]
    inv_freq = jnp.power(jnp.float32(ROPE_THETA),
                         -jnp.arange(0, ROT_DIM, 2, dtype=F32) / ROT_DIM).reshape(ROT_HALF, 1)
    return _layer_b(
        sinks[0], o, sg, x, w_out_a[0].astype(BF16), kv_norm_g.reshape(1, d), norm_b_g[0].reshape(1, d),
        w_kv[:, :KV_WIDTH].T.astype(BF16), w_kv[:, KV_WIDTH:].T.astype(BF16),
        _gain_col(knorm_b_g, N_KV_HEADS),
        wb[:, :d].T.astype(BF16), wb[:, d:].T.astype(BF16),
        _gain_col(qnorm_b_g[0], N_HEADS, SCALE * LOG2E),
        positions.reshape(1, s), inv_freq, w_out_b[0].T.astype(BF16))
```

```python
import jax
import jax.numpy as jnp
from jax import lax
from jax.experimental import pallas as pl
from jax.experimental.pallas import tpu as pltpu

D_MODEL = 1024
HEAD_DIM = 64
N_HEADS = 16
N_KV_HEADS = 4
GROUP = N_HEADS // N_KV_HEADS
KV_WIDTH = N_KV_HEADS * HEAD_DIM
WINDOW = 128
ROT_DIM = HEAD_DIM // 4
ROT_HALF = ROT_DIM // 2
ROPE_THETA = 500000.0
EPS = 1e-6
SCALE = HEAD_DIM ** -0.5
LOG2E = 1.4426950408889634
NEG = -1e30

F32 = jnp.float32
BF16 = jnp.bfloat16

ROW_TILE = 512
SUB_TILE = 256
FOX_Q_TILE = 256
FOX_PAIRS = 2
SWA_LOOKAHEAD = 2
SUM_ROWS = 16
VMEM_LIMIT = 56 * 1024 * 1024


def _dot(a, b):
    return jnp.dot(a, b, preferred_element_type=F32)


def _dot_nt(a, b):
    return lax.dot_general(a, b, (((1,), (1,)), ((), ())), preferred_element_type=F32)


def _head_norm_fm(t, gain_col):
    blocks = []
    for h in range(t.shape[0] // HEAD_DIM):
        blk = t[h * HEAD_DIM:(h + 1) * HEAD_DIM, :]
        ms = jnp.mean(blk * blk, axis=0, keepdims=True)
        blocks.append(blk * lax.rsqrt(ms + EPS))
    return jnp.concatenate(blocks, axis=0) * gain_col


def _rope_fm(t, cos, sin):
    blocks = []
    for h in range(t.shape[0] // HEAD_DIM):
        base = h * HEAD_DIM
        x1 = t[base:base + ROT_HALF, :]
        x2 = t[base + ROT_HALF:base + ROT_DIM, :]
        blocks.append(x1 * cos - x2 * sin)
        blocks.append(x1 * sin + x2 * cos)
        blocks.append(t[base + ROT_DIM:base + HEAD_DIM, :])
    return jnp.concatenate(blocks, axis=0)


def _split3(x):
    hi = x.astype(BF16)
    r1 = x - hi.astype(F32)
    mid = r1.astype(BF16)
    lo = (r1 - mid.astype(F32)).astype(BF16)
    return hi, mid, lo


def _inproj_a_kernel(x_ref, g_ref, wqkT_ref, wvg_ref, wfT_ref, bf_ref, gq_ref, gk_ref,
                     q_ref, kT_ref, v_ref, sg_ref, cT_ref, carry_ref):
    tm = x_ref.shape[1]

    @pl.when(pl.program_id(1) == 0)
    def _():
        carry_ref[...] = jnp.zeros_like(carry_ref)

    u_parts = []
    for rows in [slice(j * SUB_TILE, (j + 1) * SUB_TILE) for j in range(tm // SUB_TILE)]:
        x = x_ref[0, rows, :]
        ms = jnp.mean(x * x, axis=-1, keepdims=True)
        u32 = x * lax.rsqrt(ms + EPS) * g_ref[...]
        u_parts.append(u32.astype(BF16))
        qT = _head_norm_fm(_dot_nt(wqkT_ref[:D_MODEL, :], u32), gq_ref[...])
        kT = _head_norm_fm(_dot_nt(wqkT_ref[D_MODEL:, :], u32), gk_ref[...])
        q_ref[0, rows, :] = qT.T.astype(BF16)
        kT_ref[0, :, rows] = kT.astype(BF16)
    u = jnp.concatenate(u_parts, axis=0)

    f = _dot_nt(wfT_ref[...], u) + bf_ref[...]
    gate = _dot(u, wvg_ref[:, D_MODEL:])
    sg_ref[0] = (gate * jax.nn.sigmoid(gate)).astype(BF16)

    log_f = jnp.minimum(f, 0.0) - jnp.log1p(jnp.exp(-jnp.abs(f)))
    row = lax.broadcasted_iota(jnp.int32, (tm, tm), 0)
    col = lax.broadcasted_iota(jnp.int32, (tm, tm), 1)
    tri = (row <= col).astype(BF16)
    hi, mid, lo = _split3(log_f)
    c = (_dot(hi, tri) + _dot(mid, tri) + _dot(lo, tri)) + carry_ref[:, 0:1]
    v_ref[0] = _dot(u, wvg_ref[:, :D_MODEL]).astype(BF16)
    cT_ref[0] = c * LOG2E
    carry_ref[...] = jnp.broadcast_to(c[:, tm - 1:tm], carry_ref.shape)


def _inproj_a(x, g, wqkT, wvg, wfT, bf, gq, gk):
    b, s, d = x.shape
    tm = ROW_TILE
    full = lambda shape: pl.BlockSpec(shape, lambda bi, i: (0,) * len(shape))
    tok = pl.BlockSpec((1, tm, d), lambda bi, i: (bi, i, 0))
    return pl.pallas_call(
        _inproj_a_kernel,
        grid=(b, s // tm),
        in_specs=[tok, full(g.shape), full(wqkT.shape), full(wvg.shape), full(wfT.shape),
                  full(bf.shape), full(gq.shape), full(gk.shape)],
        out_specs=[tok,
                   pl.BlockSpec((1, d, tm), lambda bi, i: (bi, 0, i)),
                   tok, tok,
                   pl.BlockSpec((1, N_HEADS, tm), lambda bi, i: (bi, 0, i))],
        out_shape=[jax.ShapeDtypeStruct((b, s, d), BF16),
                   jax.ShapeDtypeStruct((b, d, s), BF16),
                   jax.ShapeDtypeStruct((b, s, d), BF16),
                   jax.ShapeDtypeStruct((b, s, d), BF16),
                   jax.ShapeDtypeStruct((b, N_HEADS, s), F32)],
        scratch_shapes=[pltpu.VMEM((N_HEADS, 128), F32)],
        compiler_params=pltpu.CompilerParams(
            dimension_semantics=("parallel", "arbitrary"), vmem_limit_bytes=VMEM_LIMIT),
        name="inproj_a",
    )(x, g, wqkT, wvg, wfT, bf, gq, gk)


BIAS_ROWS = 16


def _fox_kernel(q_ref, kT_ref, v_ref, c_ref, o_ref, bias_ref, vext_ref):
    s_len = q_ref.shape[1]
    n_pairs = q_ref.shape[2] // 128
    tq = FOX_Q_TILE
    lane = lax.broadcasted_iota(jnp.int32, (tq, 128), 1)
    first = lane < HEAD_DIM
    ones_a = jnp.where(lane < 3, 1.0, 0.0).astype(BF16)
    ones_b = jnp.where(jnp.logical_and(lane >= 3, lane < 6), 1.0, 0.0).astype(BF16)
    row = lax.broadcasted_iota(jnp.int32, (2 * tq, tq), 0) % tq
    col = lax.broadcasted_iota(jnp.int32, (2 * tq, tq), 1)
    causal = col <= row

    brow = lax.broadcasted_iota(jnp.int32, (BIAS_ROWS, s_len), 0)
    for p in range(n_pairs):
        parts = _split3(-c_ref[0, p, 0:1, :]) + _split3(-c_ref[0, p, 1:2, :])
        bias = jnp.zeros((BIAS_ROWS, s_len), F32)
        for r, part in enumerate(parts):
            bias = jnp.where(brow == r, part.astype(F32), bias)
        bias_ref[p] = bias.astype(BF16)
        vext_ref[p, :, :128] = v_ref[0, :, p * 128:(p + 1) * 128]
        vext_ref[p, :, 128:] = jnp.ones((s_len, 128), BF16)

    def rhs(p, c0, c1):
        pad = jnp.zeros((128 - BIAS_ROWS, c1 - c0), BF16)
        return jnp.concatenate(
            [kT_ref[0, p * 128:(p + 1) * 128, c0:c1], bias_ref[p, :, c0:c1], pad], axis=0)

    def logits(item):
        p, qb = item
        r0 = qb * tq
        q2 = q_ref[0, r0:r0 + tq, p * 128:(p + 1) * 128]
        zero = jnp.zeros_like(q2)
        lhs = jnp.concatenate(
            [jnp.concatenate([jnp.where(first, q2, zero), ones_a], axis=1),
             jnp.concatenate([jnp.where(first, zero, q2), ones_b], axis=1)], axis=0)
        s_diag = jnp.where(causal, _dot(lhs, rhs(p, r0, r0 + tq)), NEG)
        s_off = _dot(lhs, rhs(p, 0, r0)) if qb > 0 else None
        return s_diag, s_off

    order = [(p, qb) for qb in reversed(range(s_len // tq)) for p in range(n_pairs)]
    pending = logits(order[0])
    for idx, (p, qb) in enumerate(order):
        r0 = qb * tq
        s_diag, s_off = pending
        if idx + 1 < len(order):
            pending = logits(order[idx + 1])
        m = jnp.max(s_diag, axis=1, keepdims=True)
        if qb > 0:
            m = jnp.maximum(m, jnp.max(s_off, axis=1, keepdims=True))
        pv = _dot(jnp.exp2(s_diag - m).astype(BF16), vext_ref[p, r0:r0 + tq, :])
        if qb > 0:
            pv = pv + _dot(jnp.exp2(s_off - m).astype(BF16), vext_ref[p, :r0, :])
        o = pv[:, :128] / pv[:, 128:]
        o_ref[0, r0:r0 + tq, p * 128:(p + 1) * 128] = jnp.where(first, o[:tq], o[tq:]).astype(BF16)


def _fox_attention(q, kT, v, c4):
    b, s, d = q.shape
    np_, wd = FOX_PAIRS, FOX_PAIRS * 128
    return pl.pallas_call(
        _fox_kernel,
        grid=(b, d // wd),
        in_specs=[pl.BlockSpec((1, s, wd), lambda bi, j: (bi, 0, j)),
                  pl.BlockSpec((1, wd, s), lambda bi, j: (bi, j, 0)),
                  pl.BlockSpec((1, s, wd), lambda bi, j: (bi, 0, j)),
                  pl.BlockSpec((1, np_, 2, s), lambda bi, j: (bi, j, 0, 0))],
        out_specs=pl.BlockSpec((1, s, wd), lambda bi, j: (bi, 0, j)),
        out_shape=jax.ShapeDtypeStruct((b, s, d), BF16),
        scratch_shapes=[pltpu.VMEM((np_, BIAS_ROWS, s), BF16), pltpu.VMEM((np_, s, 256), BF16)],
        compiler_params=pltpu.CompilerParams(
            dimension_semantics=("parallel", "parallel"), vmem_limit_bytes=VMEM_LIMIT),
        name="fox_attention",
    )(q, kT, v, c4)


def _layer_b_kernel(sinks_ref, o_ref, sg_ref, x_ref, wo_ref, gkv_ref, gb_ref, wkT_ref, wvT_ref,
                    gkn_ref, wqT_ref, wgT_ref, gqn_ref, pos_ref, invf_ref, woT_ref,
                    out_ref, kext_ref, vext_ref, qT_ref, sgT_ref, oT_ref):
    tm = x_ref.shape[1]
    w = WINDOW
    nq = GROUP * w
    first_tile = pl.program_id(1) == 0

    parts = [slice(j * SUB_TILE, (j + 1) * SUB_TILE) for j in range(tm // SUB_TILE)]
    for rows in parts:
        og = (o_ref[0, rows, :].astype(F32) * sg_ref[0, rows, :].astype(F32)).astype(BF16)
        out_ref[0, rows, :] = x_ref[0, rows, :] + _dot(og, wo_ref[...])

    @pl.when(first_tile)
    def _():
        kext_ref[0:w, :] = jnp.zeros((w, KV_WIDTH), BF16)
        vext_ref[:, 0:w] = jnp.zeros((KV_WIDTH, w), BF16)

    @pl.when(jnp.logical_not(first_tile))
    def _():
        kext_ref[0:w, :] = kext_ref[tm:tm + w, :]
        vext_ref[:, 0:w] = vext_ref[:, tm:tm + w]

    for cols in parts:
        h = out_ref[0, cols, :]
        ms = jnp.mean(h * h, axis=-1, keepdims=True)
        hn = h * lax.rsqrt(ms + EPS)
        u_kv = hn * gkv_ref[...]
        u_b = hn * gb_ref[...]

        ang = invf_ref[...] * pos_ref[:, cols].astype(F32)
        cos = jnp.cos(ang)
        sin = jnp.sin(ang)

        ext = slice(w + cols.start, w + cols.stop)
        kT = _head_norm_fm(_dot_nt(wkT_ref[...], u_kv), gkn_ref[...])
        kext_ref[ext, :] = _rope_fm(kT, cos, sin).T.astype(BF16)
        vext_ref[:, ext] = _dot_nt(wvT_ref[...], u_kv).astype(BF16)

        qT = _head_norm_fm(_dot_nt(wqT_ref[...], u_b), gqn_ref[...])
        qT_ref[:, cols] = _rope_fm(qT, cos, sin).astype(BF16)
        gateT = _dot_nt(wgT_ref[...], u_b)
        sgT_ref[:, cols] = (gateT * jax.nn.sigmoid(gateT)).astype(BF16)

    lane_head = lax.shift_right_logical(
        lax.broadcasted_iota(jnp.int32, (2 * w, KV_WIDTH), 1), HEAD_DIM.bit_length() - 1)
    q_head = lax.shift_right_logical(lax.broadcasted_iota(jnp.int32, (1, nq), 1), w.bit_length() - 1)
    sinks = []
    for g in range(N_KV_HEADS):
        sink = jnp.zeros((1, nq), F32)
        for i in range(GROUP):
            sink = jnp.where(q_head == i, sinks_ref[g * GROUP + i] * LOG2E, sink)
        sinks.append(sink)
    key = lax.broadcasted_iota(jnp.int32, (w, nq), 0)
    qry = jnp.bitwise_and(lax.broadcasted_iota(jnp.int32, (w, nq), 1), w - 1)
    own = key <= qry
    ones_rows = jnp.ones((SUM_ROWS, 2 * w), BF16)

    def logits(item):
        g, n = item
        k2 = kext_ref[n * w:(n + 2) * w, :]
        k_band = jnp.where(lane_head == g, k2, jnp.zeros_like(k2))
        q_rep = jnp.concatenate(
            [jnp.concatenate(
                [qT_ref[(g * GROUP + i) * HEAD_DIM:(g * GROUP + i + 1) * HEAD_DIM, n * w:(n + 1) * w]]
                * N_KV_HEADS, axis=0) for i in range(GROUP)], axis=1)
        s_all = _dot(k_band, q_rep)
        s_prev = s_all[:w]
        if n == 0:
            s_prev = jnp.where(first_tile, NEG, s_prev)
        return jnp.where(own, s_all[w:], s_prev)

    items = [(g, n) for n in range(tm // w) for g in range(N_KV_HEADS)]
    pending = [logits(it) for it in items[:SWA_LOOKAHEAD]]
    for idx, (g, n) in enumerate(items):
        sT = pending.pop(0)
        if idx + SWA_LOOKAHEAD < len(items):
            pending.append(logits(items[idx + SWA_LOOKAHEAD]))
        m = jnp.maximum(jnp.max(sT, axis=0, keepdims=True), sinks[g])
        p = jnp.exp2(sT - m).astype(BF16)
        zero = jnp.zeros_like(p)
        p2 = jnp.concatenate([jnp.where(own, zero, p), jnp.where(own, p, zero)], axis=0)
        v_ext = jnp.concatenate(
            [vext_ref[g * HEAD_DIM:(g + 1) * HEAD_DIM, n * w:(n + 2) * w], ones_rows], axis=0)
        out_ext = _dot(v_ext, p2)
        denom = out_ext[HEAD_DIM:HEAD_DIM + 1] + jnp.exp2(sinks[g] - m)
        outT = out_ext[:HEAD_DIM] * (1.0 / denom)
        for i in range(GROUP):
            r0 = (g * GROUP + i) * HEAD_DIM
            oT_ref[r0:r0 + HEAD_DIM, n * w:(n + 1) * w] = outT[:, i * w:(i + 1) * w].astype(BF16)

        if g == N_KV_HEADS - 1 and (n + 1) * w % SUB_TILE == 0:
            cols = slice((n + 1) * w - SUB_TILE, (n + 1) * w)
            ogT = oT_ref[:, cols].astype(F32) * sgT_ref[:, cols].astype(F32)
            half = D_MODEL // 2
            outT = jnp.concatenate(
                [_dot(woT_ref[:half, :], ogT), _dot(woT_ref[half:, :], ogT)], axis=0)
            out_ref[0, cols, :] = out_ref[0, cols, :] + outT.T


def _layer_b(sinks, o, sg, x, wo, gkv, gb, wkT, wvT, gkn, wqT, wgT, gqn, pos, invf, woT):
    b, s, d = x.shape
    tm = ROW_TILE
    full = lambda a: pl.BlockSpec(a.shape, lambda bi, i: (0,) * a.ndim)
    tok = pl.BlockSpec((1, tm, d), lambda bi, i: (bi, i, 0))
    return pl.pallas_call(
        _layer_b_kernel,
        grid=(b, s // tm),
        in_specs=[pl.BlockSpec(memory_space=pltpu.SMEM),
                  tok, tok, tok, full(wo), full(gkv), full(gb), full(wkT), full(wvT), full(gkn),
                  full(wqT), full(wgT), full(gqn),
                  pl.BlockSpec((1, tm), lambda bi, i: (0, i)), full(invf), full(woT)],
        out_specs=tok,
        out_shape=jax.ShapeDtypeStruct((b, s, d), F32),
        scratch_shapes=[pltpu.VMEM((tm + WINDOW, KV_WIDTH), BF16),
                        pltpu.VMEM((KV_WIDTH, tm + WINDOW), BF16),
                        pltpu.VMEM((d, tm), BF16),
                        pltpu.VMEM((d, tm), BF16),
                        pltpu.VMEM((d, tm), BF16)],
        compiler_params=pltpu.CompilerParams(
            dimension_semantics=("parallel", "arbitrary"), vmem_limit_bytes=VMEM_LIMIT),
        name="layer_b",
    )(sinks, o, sg, x, wo, gkv, gb, wkT, wvT, gkn, wqT, wgT, gqn, pos, invf, woT)


def _gain_col(g, n_heads, scale=1.0):
    return (jnp.tile(g.astype(F32), n_heads) * scale).reshape(n_heads * HEAD_DIM, 1)


def kernel(x, positions, norm_a_g, w_in_a, b_forget, qnorm_a_g, knorm_a_g, w_out_a, kv_norm_g, w_kv,
           knorm_b_g, norm_b_g, w_in_b, qnorm_b_g, sinks, w_out_b):
    b, s, d = x.shape
    wa = w_in_a[0]
    wqkT = wa[:, :2 * d].T
    wvg = jnp.concatenate([wa[:, 2 * d:3 * d], wa[:, 3 * d + N_HEADS:]], axis=1).astype(BF16)
    wfT = wa[:, 3 * d:3 * d + N_HEADS].T.astype(BF16)
    q, kT, v, sg, cT = _inproj_a(
        x, norm_a_g[0].reshape(1, d), wqkT, wvg, wfT, b_forget[0].reshape(N_HEADS, 1),
        _gain_col(qnorm_a_g[0], N_HEADS, SCALE * LOG2E), _gain_col(knorm_a_g[0], N_HEADS))
    o = _fox_attention(q, kT, v, cT.reshape(b, N_HEADS // 2, 2, s))

    wb = w_in_b[0]
    inv_freq = jnp.power(jnp.float32(ROPE_THETA),
                         -jnp.arange(0, ROT_DIM, 2, dtype=F32) / ROT_DIM).reshape(ROT_HALF, 1)
    return _layer_b(
        sinks[0], o, sg, x, w_out_a[0].astype(BF16), kv_norm_g.reshape(1, d), norm_b_g[0].reshape(1, d),
        w_kv[:, :KV_WIDTH].T, w_kv[:, KV_WIDTH:].T,
        _gain_col(knorm_b_g, N_KV_HEADS),
        wb[:, :d].T, wb[:, d:].T,
        _gain_col(qnorm_b_g[0], N_HEADS, SCALE * LOG2E),
        positions.reshape(1, s), inv_freq, w_out_b[0].T)
```

```python
import jax
import jax.numpy as jnp
from jax import lax
from jax.experimental import pallas as pl
from jax.experimental.pallas import tpu as pltpu

D_MODEL = 1024
HEAD_DIM = 64
N_HEADS = 16
N_KV_HEADS = 4
GROUP = N_HEADS // N_KV_HEADS
KV_WIDTH = N_KV_HEADS * HEAD_DIM
WINDOW = 128
ROT_DIM = HEAD_DIM // 4
ROT_HALF = ROT_DIM // 2
ROPE_THETA = 500000.0
EPS = 1e-6
SCALE = HEAD_DIM ** -0.5
LOG2E = 1.4426950408889634
NEG = -1e30

F32 = jnp.float32
BF16 = jnp.bfloat16

ROW_TILE = 512
SUB_TILE = 256
FOX_Q_TILE = 256
FOX_PAIRS = 2
SWA_LOOKAHEAD = 2
SUM_ROWS = 16
VMEM_LIMIT = 56 * 1024 * 1024


def _dot(a, b):
    return jnp.dot(a, b, preferred_element_type=F32)


def _dot_nt(a, b):
    return lax.dot_general(a, b, (((1,), (1,)), ((), ())), preferred_element_type=F32)


def _head_norm_fm(t, gain_col):
    blocks = []
    for h in range(t.shape[0] // HEAD_DIM):
        blk = t[h * HEAD_DIM:(h + 1) * HEAD_DIM, :]
        ms = jnp.mean(blk * blk, axis=0, keepdims=True)
        blocks.append(blk * lax.rsqrt(ms + EPS))
    return jnp.concatenate(blocks, axis=0) * gain_col


def _rope_fm(t, cos, sin):
    blocks = []
    for h in range(t.shape[0] // HEAD_DIM):
        base = h * HEAD_DIM
        x1 = t[base:base + ROT_HALF, :]
        x2 = t[base + ROT_HALF:base + ROT_DIM, :]
        blocks.append(x1 * cos - x2 * sin)
        blocks.append(x1 * sin + x2 * cos)
        blocks.append(t[base + ROT_DIM:base + HEAD_DIM, :])
    return jnp.concatenate(blocks, axis=0)


def _split3(x):
    hi = x.astype(BF16)
    r1 = x - hi.astype(F32)
    mid = r1.astype(BF16)
    lo = (r1 - mid.astype(F32)).astype(BF16)
    return hi, mid, lo


def _inproj_a_kernel(x_ref, g_ref, wqkT_ref, wvg_ref, wfT_ref, bf_ref, gq_ref, gk_ref,
                     q_ref, kT_ref, v_ref, sg_ref, cT_ref, carry_ref):
    tm = x_ref.shape[1]

    @pl.when(pl.program_id(1) == 0)
    def _():
        carry_ref[...] = jnp.zeros_like(carry_ref)

    u_parts = []
    for rows in [slice(j * SUB_TILE, (j + 1) * SUB_TILE) for j in range(tm // SUB_TILE)]:
        x = x_ref[0, rows, :]
        ms = jnp.mean(x * x, axis=-1, keepdims=True)
        u_part = (x * lax.rsqrt(ms + EPS) * g_ref[...]).astype(BF16)
        u_parts.append(u_part)
        qT = _head_norm_fm(_dot_nt(wqkT_ref[:D_MODEL, :], u_part), gq_ref[...])
        kT = _head_norm_fm(_dot_nt(wqkT_ref[D_MODEL:, :], u_part), gk_ref[...])
        q_ref[0, rows, :] = qT.T.astype(BF16)
        kT_ref[0, :, rows] = kT.astype(BF16)
    u = jnp.concatenate(u_parts, axis=0)

    f = _dot_nt(wfT_ref[...], u) + bf_ref[...]
    gate = _dot(u, wvg_ref[:, D_MODEL:])
    sg_ref[0] = (gate * jax.nn.sigmoid(gate)).astype(BF16)

    log_f = jnp.minimum(f, 0.0) - jnp.log1p(jnp.exp(-jnp.abs(f)))
    row = lax.broadcasted_iota(jnp.int32, (tm, tm), 0)
    col = lax.broadcasted_iota(jnp.int32, (tm, tm), 1)
    tri = (row <= col).astype(BF16)
    hi, mid, lo = _split3(log_f)
    c = (_dot(hi, tri) + _dot(mid, tri) + _dot(lo, tri)) + carry_ref[:, 0:1]
    v_ref[0] = _dot(u, wvg_ref[:, :D_MODEL]).astype(BF16)
    cT_ref[0] = c * LOG2E
    carry_ref[...] = jnp.broadcast_to(c[:, tm - 1:tm], carry_ref.shape)


def _inproj_a(x, g, wqkT, wvg, wfT, bf, gq, gk):
    b, s, d = x.shape
    tm = ROW_TILE
    full = lambda shape: pl.BlockSpec(shape, lambda bi, i: (0,) * len(shape))
    tok = pl.BlockSpec((1, tm, d), lambda bi, i: (bi, i, 0))
    return pl.pallas_call(
        _inproj_a_kernel,
        grid=(b, s // tm),
        in_specs=[tok, full(g.shape), full(wqkT.shape), full(wvg.shape), full(wfT.shape),
                  full(bf.shape), full(gq.shape), full(gk.shape)],
        out_specs=[tok,
                   pl.BlockSpec((1, d, tm), lambda bi, i: (bi, 0, i)),
                   tok, tok,
                   pl.BlockSpec((1, N_HEADS, tm), lambda bi, i: (bi, 0, i))],
        out_shape=[jax.ShapeDtypeStruct((b, s, d), BF16),
                   jax.ShapeDtypeStruct((b, d, s), BF16),
                   jax.ShapeDtypeStruct((b, s, d), BF16),
                   jax.ShapeDtypeStruct((b, s, d), BF16),
                   jax.ShapeDtypeStruct((b, N_HEADS, s), F32)],
        scratch_shapes=[pltpu.VMEM((N_HEADS, 128), F32)],
        compiler_params=pltpu.CompilerParams(
            dimension_semantics=("parallel", "arbitrary"), vmem_limit_bytes=VMEM_LIMIT),
        name="inproj_a",
    )(x, g, wqkT, wvg, wfT, bf, gq, gk)


BIAS_ROWS = 16


def _fox_kernel(q_ref, kT_ref, v_ref, c_ref, o_ref, bias_ref, vext_ref):
    s_len = q_ref.shape[1]
    n_pairs = q_ref.shape[2] // 128
    tq = FOX_Q_TILE
    lane = lax.broadcasted_iota(jnp.int32, (tq, 128), 1)
    first = lane < HEAD_DIM
    ones_a = jnp.where(lane < 3, 1.0, 0.0).astype(BF16)
    ones_b = jnp.where(jnp.logical_and(lane >= 3, lane < 6), 1.0, 0.0).astype(BF16)
    row = lax.broadcasted_iota(jnp.int32, (2 * tq, tq), 0) % tq
    col = lax.broadcasted_iota(jnp.int32, (2 * tq, tq), 1)
    causal = col <= row

    brow = lax.broadcasted_iota(jnp.int32, (BIAS_ROWS, s_len), 0)
    for p in range(n_pairs):
        parts = _split3(-c_ref[0, p, 0:1, :]) + _split3(-c_ref[0, p, 1:2, :])
        bias = jnp.zeros((BIAS_ROWS, s_len), F32)
        for r, part in enumerate(parts):
            bias = jnp.where(brow == r, part.astype(F32), bias)
        bias_ref[p] = bias.astype(BF16)
        vext_ref[p, :, :128] = v_ref[0, :, p * 128:(p + 1) * 128]
        vext_ref[p, :, 128:] = jnp.ones((s_len, 128), BF16)

    def rhs(p, c0, c1):
        pad = jnp.zeros((128 - BIAS_ROWS, c1 - c0), BF16)
        return jnp.concatenate(
            [kT_ref[0, p * 128:(p + 1) * 128, c0:c1], bias_ref[p, :, c0:c1], pad], axis=0)

    def logits(item):
        p, qb = item
        r0 = qb * tq
        q2 = q_ref[0, r0:r0 + tq, p * 128:(p + 1) * 128]
        zero = jnp.zeros_like(q2)
        lhs = jnp.concatenate(
            [jnp.concatenate([jnp.where(first, q2, zero), ones_a], axis=1),
             jnp.concatenate([jnp.where(first, zero, q2), ones_b], axis=1)], axis=0)
        s_diag = jnp.where(causal, _dot(lhs, rhs(p, r0, r0 + tq)), NEG)
        s_off = _dot(lhs, rhs(p, 0, r0)) if qb > 0 else None
        return s_diag, s_off

    order = [(p, qb) for qb in reversed(range(s_len // tq)) for p in range(n_pairs)]
    pending = logits(order[0])
    for idx, (p, qb) in enumerate(order):
        r0 = qb * tq
        s_diag, s_off = pending
        if idx + 1 < len(order):
            pending = logits(order[idx + 1])
        m = jnp.max(s_diag, axis=1, keepdims=True)
        if qb > 0:
            m = jnp.maximum(m, jnp.max(s_off, axis=1, keepdims=True))
        pv = _dot(jnp.exp2(s_diag - m).astype(BF16), vext_ref[p, r0:r0 + tq, :])
        if qb > 0:
            pv = pv + _dot(jnp.exp2(s_off - m).astype(BF16), vext_ref[p, :r0, :])
        o = pv[:, :128] / pv[:, 128:]
        o_ref[0, r0:r0 + tq, p * 128:(p + 1) * 128] = jnp.where(first, o[:tq], o[tq:]).astype(BF16)


def _fox_attention(q, kT, v, c4):
    b, s, d = q.shape
    np_, wd = FOX_PAIRS, FOX_PAIRS * 128
    return pl.pallas_call(
        _fox_kernel,
        grid=(b, d // wd),
        in_specs=[pl.BlockSpec((1, s, wd), lambda bi, j: (bi, 0, j)),
                  pl.BlockSpec((1, wd, s), lambda bi, j: (bi, j, 0)),
                  pl.BlockSpec((1, s, wd), lambda bi, j: (bi, 0, j)),
                  pl.BlockSpec((1, np_, 2, s), lambda bi, j: (bi, j, 0, 0))],
        out_specs=pl.BlockSpec((1, s, wd), lambda bi, j: (bi, 0, j)),
        out_shape=jax.ShapeDtypeStruct((b, s, d), BF16),
        scratch_shapes=[pltpu.VMEM((np_, BIAS_ROWS, s), BF16), pltpu.VMEM((np_, s, 256), BF16)],
        compiler_params=pltpu.CompilerParams(
            dimension_semantics=("parallel", "parallel"), vmem_limit_bytes=VMEM_LIMIT),
        name="fox_attention",
    )(q, kT, v, c4)


def _layer_b_kernel(sinks_ref, o_ref, sg_ref, x_ref, wo_ref, gkv_ref, gb_ref, wkT_ref, wvT_ref,
                    gkn_ref, wqT_ref, wgT_ref, gqn_ref, pos_ref, invf_ref, woT_ref,
                    out_ref, kext_ref, vext_ref, qT_ref, sgT_ref, oT_ref):
    tm = x_ref.shape[1]
    w = WINDOW
    nq = GROUP * w
    first_tile = pl.program_id(1) == 0

    parts = [slice(j * SUB_TILE, (j + 1) * SUB_TILE) for j in range(tm // SUB_TILE)]
    for rows in parts:
        og = (o_ref[0, rows, :].astype(F32) * sg_ref[0, rows, :].astype(F32)).astype(BF16)
        out_ref[0, rows, :] = x_ref[0, rows, :] + _dot(og, wo_ref[...])

    @pl.when(first_tile)
    def _():
        kext_ref[0:w, :] = jnp.zeros((w, KV_WIDTH), BF16)
        vext_ref[:, 0:w] = jnp.zeros((KV_WIDTH, w), BF16)

    @pl.when(jnp.logical_not(first_tile))
    def _():
        kext_ref[0:w, :] = kext_ref[tm:tm + w, :]
        vext_ref[:, 0:w] = vext_ref[:, tm:tm + w]

    for cols in parts:
        h = out_ref[0, cols, :]
        ms = jnp.mean(h * h, axis=-1, keepdims=True)
        hn = h * lax.rsqrt(ms + EPS)
        u_kv = (hn * gkv_ref[...]).astype(BF16)
        u_b = (hn * gb_ref[...]).astype(BF16)

        ang = invf_ref[...] * pos_ref[:, cols].astype(F32)
        cos = jnp.cos(ang)
        sin = jnp.sin(ang)

        ext = slice(w + cols.start, w + cols.stop)
        kT = _head_norm_fm(_dot_nt(wkT_ref[...], u_kv), gkn_ref[...])
        kext_ref[ext, :] = _rope_fm(kT, cos, sin).T.astype(BF16)
        vext_ref[:, ext] = _dot_nt(wvT_ref[...], u_kv).astype(BF16)

        qT = _head_norm_fm(_dot_nt(wqT_ref[...], u_b), gqn_ref[...])
        qT_ref[:, cols] = _rope_fm(qT, cos, sin).astype(BF16)
        gateT = _dot_nt(wgT_ref[...], u_b)
        sgT_ref[:, cols] = (gateT * jax.nn.sigmoid(gateT)).astype(BF16)

    lane_head = lax.shift_right_logical(
        lax.broadcasted_iota(jnp.int32, (2 * w, KV_WIDTH), 1), HEAD_DIM.bit_length() - 1)
    q_head = lax.shift_right_logical(lax.broadcasted_iota(jnp.int32, (1, nq), 1), w.bit_length() - 1)
    sinks = []
    for g in range(N_KV_HEADS):
        sink = jnp.zeros((1, nq), F32)
        for i in range(GROUP):
            sink = jnp.where(q_head == i, sinks_ref[g * GROUP + i] * LOG2E, sink)
        sinks.append(sink)
    key = lax.broadcasted_iota(jnp.int32, (w, nq), 0)
    qry = jnp.bitwise_and(lax.broadcasted_iota(jnp.int32, (w, nq), 1), w - 1)
    own = key <= qry
    ones_rows = jnp.ones((SUM_ROWS, 2 * w), BF16)

    def logits(item):
        g, n = item
        k2 = kext_ref[n * w:(n + 2) * w, :]
        k_band = jnp.where(lane_head == g, k2, jnp.zeros_like(k2))
        q_rep = jnp.concatenate(
            [jnp.concatenate(
                [qT_ref[(g * GROUP + i) * HEAD_DIM:(g * GROUP + i + 1) * HEAD_DIM, n * w:(n + 1) * w]]
                * N_KV_HEADS, axis=0) for i in range(GROUP)], axis=1)
        s_all = _dot(k_band, q_rep)
        s_prev = s_all[:w]
        if n == 0:
            s_prev = jnp.where(first_tile, NEG, s_prev)
        return jnp.where(own, s_all[w:], s_prev)

    items = [(g, n) for n in range(tm // w) for g in range(N_KV_HEADS)]
    pending = [logits(it) for it in items[:SWA_LOOKAHEAD]]
    for idx, (g, n) in enumerate(items):
        sT = pending.pop(0)
        if idx + SWA_LOOKAHEAD < len(items):
            pending.append(logits(items[idx + SWA_LOOKAHEAD]))
        m = jnp.maximum(jnp.max(sT, axis=0, keepdims=True), sinks[g])
        p = jnp.exp2(sT - m).astype(BF16)
        zero = jnp.zeros_like(p)
        p2 = jnp.concatenate([jnp.where(own, zero, p), jnp.where(own, p, zero)], axis=0)
        v_ext = jnp.concatenate(
            [vext_ref[g * HEAD_DIM:(g + 1) * HEAD_DIM, n * w:(n + 2) * w], ones_rows], axis=0)
        out_ext = _dot(v_ext, p2)
        denom = out_ext[HEAD_DIM:HEAD_DIM + 1] + jnp.exp2(sinks[g] - m)
        outT = out_ext[:HEAD_DIM] * (1.0 / denom)
        for i in range(GROUP):
            r0 = (g * GROUP + i) * HEAD_DIM
            oT_ref[r0:r0 + HEAD_DIM, n * w:(n + 1) * w] = outT[:, i * w:(i + 1) * w].astype(BF16)

        if g == N_KV_HEADS - 1 and (n + 1) * w % SUB_TILE == 0:
            cols = slice((n + 1) * w - SUB_TILE, (n + 1) * w)
            ogT = (oT_ref[:, cols].astype(F32) * sgT_ref[:, cols].astype(F32)).astype(BF16)
            half = D_MODEL // 2
            outT = jnp.concatenate(
                [_dot(woT_ref[:half, :], ogT), _dot(woT_ref[half:, :], ogT)], axis=0)
            out_ref[0, cols, :] = out_ref[0, cols, :] + outT.T


def _layer_b(sinks, o, sg, x, wo, gkv, gb, wkT, wvT, gkn, wqT, wgT, gqn, pos, invf, woT):
    b, s, d = x.shape
    tm = ROW_TILE
    full = lambda a: pl.BlockSpec(a.shape, lambda bi, i: (0,) * a.ndim)
    tok = pl.BlockSpec((1, tm, d), lambda bi, i: (bi, i, 0))
    return pl.pallas_call(
        _layer_b_kernel,
        grid=(b, s // tm),
        in_specs=[pl.BlockSpec(memory_space=pltpu.SMEM),
                  tok, tok, tok, full(wo), full(gkv), full(gb), full(wkT), full(wvT), full(gkn),
                  full(wqT), full(wgT), full(gqn),
                  pl.BlockSpec((1, tm), lambda bi, i: (0, i)), full(invf), full(woT)],
        out_specs=tok,
        out_shape=jax.ShapeDtypeStruct((b, s, d), F32),
        scratch_shapes=[pltpu.VMEM((tm + WINDOW, KV_WIDTH), BF16),
                        pltpu.VMEM((KV_WIDTH, tm + WINDOW), BF16),
                        pltpu.VMEM((d, tm), BF16),
                        pltpu.VMEM((d, tm), BF16),
                        pltpu.VMEM((d, tm), BF16)],
        compiler_params=pltpu.CompilerParams(
            dimension_semantics=("parallel", "arbitrary"), vmem_limit_bytes=VMEM_LIMIT),
        name="layer_b",
    )(sinks, o, sg, x, wo, gkv, gb, wkT, wvT, gkn, wqT, wgT, gqn, pos, invf, woT)


def _gain_col(g, n_heads, scale=1.0):
    return (jnp.tile(g.astype(F32), n_heads) * scale).reshape(n_heads * HEAD_DIM, 1)


def kernel(x, positions, norm_a_g, w_in_a, b_forget, qnorm_a_g, knorm_a_g, w_out_a, kv_norm_g, w_kv,
           knorm_b_g, norm_b_g, w_in_b, qnorm_b_g, sinks, w_out_b):
    b, s, d = x.shape
    wa = w_in_a[0]
    wqkT = wa[:, :2 * d].T.astype(BF16)
    wvg = jnp.concatenate([wa[:, 2 * d:3 * d], wa[:, 3 * d + N_HEADS:]], axis=1).astype(BF16)
    wfT = wa[:, 3 * d:3 * d + N_HEADS].T.astype(BF16)
    q, kT, v, sg, cT = _inproj_a(
        x, norm_a_g[0].reshape(1, d), wqkT, wvg, wfT, b_forget[0].reshape(N_HEADS, 1),
        _gain_col(qnorm_a_g[0], N_HEADS, SCALE * LOG2E), _gain_col(knorm_a_g[0], N_HEADS))
    o = _fox_attention(q, kT, v, cT.reshape(b, N_HEADS // 2, 2, s))

    wb = w_in_b[0]
    inv_freq = jnp.power(jnp.float32(ROPE_THETA),
                         -jnp.arange(0, ROT_DIM, 2, dtype=F32) / ROT_DIM).reshape(ROT_HALF, 1)
    return _layer_b(
        sinks[0], o, sg, x, w_out_a[0].astype(BF16), kv_norm_g.reshape(1, d), norm_b_g[0].reshape(1, d),
        w_kv[:, :KV_WIDTH].T.astype(BF16), w_kv[:, KV_WIDTH:].T.astype(BF16),
        _gain_col(knorm_b_g, N_KV_HEADS),
        wb[:, :d].T.astype(BF16), wb[:, d:].T.astype(BF16),
        _gain_col(qnorm_b_g[0], N_HEADS, SCALE * LOG2E),
        positions.reshape(1, s), inv_freq, w_out_b[0].T.astype(BF16))
```

```python
import jax
import jax.numpy as jnp
from jax import lax
from jax.experimental import pallas as pl
from jax.experimental.pallas import tpu as pltpu

D_MODEL = 1024
HEAD_DIM = 64
N_HEADS = 16
N_KV_HEADS = 4
GROUP = N_HEADS // N_KV_HEADS
KV_WIDTH = N_KV_HEADS * HEAD_DIM
WINDOW = 128
ROT_DIM = HEAD_DIM // 4
ROT_HALF = ROT_DIM // 2
ROPE_THETA = 500000.0
EPS = 1e-6
SCALE = HEAD_DIM ** -0.5
LOG2E = 1.4426950408889634
NEG = -1e30

F32 = jnp.float32
BF16 = jnp.bfloat16

ROW_TILE = 512
SUB_TILE = 256
FOX_Q_TILE = 256
FOX_PAIRS = 2
SWA_LOOKAHEAD = 2
SUM_ROWS = 16
VMEM_LIMIT = 56 * 1024 * 1024


def _dot(a, b):
    return jnp.dot(a, b, preferred_element_type=F32)


def _dot_nt(a, b):
    return lax.dot_general(a, b, (((1,), (1,)), ((), ())), preferred_element_type=F32)


def _head_norm_fm(t, gain_col):
    blocks = []
    for h in range(t.shape[0] // HEAD_DIM):
        blk = t[h * HEAD_DIM:(h + 1) * HEAD_DIM, :]
        ms = jnp.mean(blk * blk, axis=0, keepdims=True)
        blocks.append(blk * lax.rsqrt(ms + EPS))
    return jnp.concatenate(blocks, axis=0) * gain_col


def _rope_fm(t, cos, sin):
    blocks = []
    for h in range(t.shape[0] // HEAD_DIM):
        base = h * HEAD_DIM
        x1 = t[base:base + ROT_HALF, :]
        x2 = t[base + ROT_HALF:base + ROT_DIM, :]
        blocks.append(x1 * cos - x2 * sin)
        blocks.append(x1 * sin + x2 * cos)
        blocks.append(t[base + ROT_DIM:base + HEAD_DIM, :])
    return jnp.concatenate(blocks, axis=0)


def _split3(x):
    hi = x.astype(BF16)
    r1 = x - hi.astype(F32)
    mid = r1.astype(BF16)
    lo = (r1 - mid.astype(F32)).astype(BF16)
    return hi, mid, lo


def _inproj_a_kernel(x_ref, g_ref, wqkT_ref, wvg_ref, wfT_ref, bf_ref, gq_ref, gk_ref,
                     q_ref, kT_ref, v_ref, sg_ref, cT_ref, carry_ref):
    tm = x_ref.shape[1]

    @pl.when(pl.program_id(1) == 0)
    def _():
        carry_ref[...] = jnp.zeros_like(carry_ref)

    u_parts = []
    for rows in [slice(j * SUB_TILE, (j + 1) * SUB_TILE) for j in range(tm // SUB_TILE)]:
        x = x_ref[0, rows, :]
        ms = jnp.mean(x * x, axis=-1, keepdims=True)
        u32 = x * lax.rsqrt(ms + EPS) * g_ref[...]
        u_parts.append(u32.astype(BF16))
        uT = u32.T.astype(BF16)
        qT = _head_norm_fm(_dot(wqkT_ref[:D_MODEL, :], uT), gq_ref[...])
        kT = _head_norm_fm(_dot(wqkT_ref[D_MODEL:, :], uT), gk_ref[...])
        q_ref[0, rows, :] = qT.T.astype(BF16)
        kT_ref[0, :, rows] = kT.astype(BF16)
    u = jnp.concatenate(u_parts, axis=0)

    f = _dot_nt(wfT_ref[...], u) + bf_ref[...]
    gate = _dot(u, wvg_ref[:, D_MODEL:])
    sg_ref[0] = (gate * jax.nn.sigmoid(gate)).astype(BF16)

    log_f = jnp.minimum(f, 0.0) - jnp.log1p(jnp.exp(-jnp.abs(f)))
    row = lax.broadcasted_iota(jnp.int32, (tm, tm), 0)
    col = lax.broadcasted_iota(jnp.int32, (tm, tm), 1)
    tri = (row <= col).astype(BF16)
    hi, mid, lo = _split3(log_f)
    c = (_dot(hi, tri) + _dot(mid, tri) + _dot(lo, tri)) + carry_ref[:, 0:1]
    v_ref[0] = _dot(u, wvg_ref[:, :D_MODEL]).astype(BF16)
    cT_ref[0] = c * LOG2E
    carry_ref[...] = jnp.broadcast_to(c[:, tm - 1:tm], carry_ref.shape)


def _inproj_a(x, g, wqkT, wvg, wfT, bf, gq, gk):
    b, s, d = x.shape
    tm = ROW_TILE
    full = lambda shape: pl.BlockSpec(shape, lambda bi, i: (0,) * len(shape))
    tok = pl.BlockSpec((1, tm, d), lambda bi, i: (bi, i, 0))
    return pl.pallas_call(
        _inproj_a_kernel,
        grid=(b, s // tm),
        in_specs=[tok, full(g.shape), full(wqkT.shape), full(wvg.shape), full(wfT.shape),
                  full(bf.shape), full(gq.shape), full(gk.shape)],
        out_specs=[tok,
                   pl.BlockSpec((1, d, tm), lambda bi, i: (bi, 0, i)),
                   tok, tok,
                   pl.BlockSpec((1, N_HEADS, tm), lambda bi, i: (bi, 0, i))],
        out_shape=[jax.ShapeDtypeStruct((b, s, d), BF16),
                   jax.ShapeDtypeStruct((b, d, s), BF16),
                   jax.ShapeDtypeStruct((b, s, d), BF16),
                   jax.ShapeDtypeStruct((b, s, d), BF16),
                   jax.ShapeDtypeStruct((b, N_HEADS, s), F32)],
        scratch_shapes=[pltpu.VMEM((N_HEADS, 128), F32)],
        compiler_params=pltpu.CompilerParams(
            dimension_semantics=("parallel", "arbitrary"), vmem_limit_bytes=VMEM_LIMIT),
        name="inproj_a",
    )(x, g, wqkT, wvg, wfT, bf, gq, gk)


BIAS_ROWS = 16


def _fox_kernel(q_ref, kT_ref, v_ref, c_ref, o_ref, bias_ref, vext_ref):
    s_len = q_ref.shape[1]
    n_pairs = q_ref.shape[2] // 128
    tq = FOX_Q_TILE
    lane = lax.broadcasted_iota(jnp.int32, (tq, 128), 1)
    first = lane < HEAD_DIM
    ones_a = jnp.where(lane < 3, 1.0, 0.0).astype(BF16)
    ones_b = jnp.where(jnp.logical_and(lane >= 3, lane < 6), 1.0, 0.0).astype(BF16)
    row = lax.broadcasted_iota(jnp.int32, (2 * tq, tq), 0) % tq
    col = lax.broadcasted_iota(jnp.int32, (2 * tq, tq), 1)
    causal = col <= row

    brow = lax.broadcasted_iota(jnp.int32, (BIAS_ROWS, s_len), 0)
    for p in range(n_pairs):
        parts = _split3(-c_ref[0, p, 0:1, :]) + _split3(-c_ref[0, p, 1:2, :])
        bias = jnp.zeros((BIAS_ROWS, s_len), F32)
        for r, part in enumerate(parts):
            bias = jnp.where(brow == r, part.astype(F32), bias)
        bias_ref[p] = bias.astype(BF16)
        vext_ref[p, :, :128] = v_ref[0, :, p * 128:(p + 1) * 128]
        vext_ref[p, :, 128:] = jnp.ones((s_len, 128), BF16)

    def rhs(p, c0, c1):
        pad = jnp.zeros((128 - BIAS_ROWS, c1 - c0), BF16)
        return jnp.concatenate(
            [kT_ref[0, p * 128:(p + 1) * 128, c0:c1], bias_ref[p, :, c0:c1], pad], axis=0)

    def logits(item):
        p, qb = item
        r0 = qb * tq
        q2 = q_ref[0, r0:r0 + tq, p * 128:(p + 1) * 128]
        zero = jnp.zeros_like(q2)
        lhs = jnp.concatenate(
            [jnp.concatenate([jnp.where(first, q2, zero), ones_a], axis=1),
             jnp.concatenate([jnp.where(first, zero, q2), ones_b], axis=1)], axis=0)
        s_diag = jnp.where(causal, _dot(lhs, rhs(p, r0, r0 + tq)), NEG)
        s_off = _dot(lhs, rhs(p, 0, r0)) if qb > 0 else None
        return s_diag, s_off

    order = [(p, qb) for qb in reversed(range(s_len // tq)) for p in range(n_pairs)]
    pending = logits(order[0])
    for idx, (p, qb) in enumerate(order):
        r0 = qb * tq
        s_diag, s_off = pending
        if idx + 1 < len(order):
            pending = logits(order[idx + 1])
        m = jnp.max(s_diag, axis=1, keepdims=True)
        if qb > 0:
            m = jnp.maximum(m, jnp.max(s_off, axis=1, keepdims=True))
        pv = _dot(jnp.exp2(s_diag - m).astype(BF16), vext_ref[p, r0:r0 + tq, :])
        if qb > 0:
            pv = pv + _dot(jnp.exp2(s_off - m).astype(BF16), vext_ref[p, :r0, :])
        o = pv[:, :128] / pv[:, 128:]
        o_ref[0, r0:r0 + tq, p * 128:(p + 1) * 128] = jnp.where(first, o[:tq], o[tq:]).astype(BF16)


def _fox_attention(q, kT, v, c4):
    b, s, d = q.shape
    np_, wd = FOX_PAIRS, FOX_PAIRS * 128
    return pl.pallas_call(
        _fox_kernel,
        grid=(b, d // wd),
        in_specs=[pl.BlockSpec((1, s, wd), lambda bi, j: (bi, 0, j)),
                  pl.BlockSpec((1, wd, s), lambda bi, j: (bi, j, 0)),
                  pl.BlockSpec((1, s, wd), lambda bi, j: (bi, 0, j)),
                  pl.BlockSpec((1, np_, 2, s), lambda bi, j: (bi, j, 0, 0))],
        out_specs=pl.BlockSpec((1, s, wd), lambda bi, j: (bi, 0, j)),
        out_shape=jax.ShapeDtypeStruct((b, s, d), BF16),
        scratch_shapes=[pltpu.VMEM((np_, BIAS_ROWS, s), BF16), pltpu.VMEM((np_, s, 256), BF16)],
        compiler_params=pltpu.CompilerParams(
            dimension_semantics=("parallel", "parallel"), vmem_limit_bytes=VMEM_LIMIT),
        name="fox_attention",
    )(q, kT, v, c4)


def _layer_b_kernel(sinks_ref, o_ref, sg_ref, x_ref, wo_ref, gkv_ref, gb_ref, wkT_ref, wvT_ref,
                    gkn_ref, wqT_ref, wgT_ref, gqn_ref, pos_ref, invf_ref, woT_ref,
                    out_ref, kext_ref, vext_ref, qT_ref, sgT_ref, oT_ref):
    tm = x_ref.shape[1]
    w = WINDOW
    nq = GROUP * w
    first_tile = pl.program_id(1) == 0

    parts = [slice(j * SUB_TILE, (j + 1) * SUB_TILE) for j in range(tm // SUB_TILE)]
    for rows in parts:
        og = (o_ref[0, rows, :].astype(F32) * sg_ref[0, rows, :].astype(F32)).astype(BF16)
        out_ref[0, rows, :] = x_ref[0, rows, :] + _dot(og, wo_ref[...])

    @pl.when(first_tile)
    def _():
        kext_ref[0:w, :] = jnp.zeros((w, KV_WIDTH), BF16)
        vext_ref[:, 0:w] = jnp.zeros((KV_WIDTH, w), BF16)

    @pl.when(jnp.logical_not(first_tile))
    def _():
        kext_ref[0:w, :] = kext_ref[tm:tm + w, :]
        vext_ref[:, 0:w] = vext_ref[:, tm:tm + w]

    for cols in parts:
        h = out_ref[0, cols, :]
        ms = jnp.mean(h * h, axis=-1, keepdims=True)
        hn = h * lax.rsqrt(ms + EPS)
        hnT = hn.T
        u_kv = (hnT * gkv_ref[...]).astype(BF16)
        u_b = (hnT * gb_ref[...]).astype(BF16)

        ang = invf_ref[...] * pos_ref[:, cols].astype(F32)
        cos = jnp.cos(ang)
        sin = jnp.sin(ang)

        ext = slice(w + cols.start, w + cols.stop)
        kT = _head_norm_fm(_dot(wkT_ref[...], u_kv), gkn_ref[...])
        kext_ref[ext, :] = _rope_fm(kT, cos, sin).T.astype(BF16)
        vext_ref[:, ext] = _dot(wvT_ref[...], u_kv).astype(BF16)

        qT = _head_norm_fm(_dot(wqT_ref[...], u_b), gqn_ref[...])
        qT_ref[:, cols] = _rope_fm(qT, cos, sin).astype(BF16)
        gateT = _dot(wgT_ref[...], u_b)
        sgT_ref[:, cols] = (gateT * jax.nn.sigmoid(gateT)).astype(BF16)

    lane_head = lax.shift_right_logical(
        lax.broadcasted_iota(jnp.int32, (2 * w, KV_WIDTH), 1), HEAD_DIM.bit_length() - 1)
    q_head = lax.shift_right_logical(lax.broadcasted_iota(jnp.int32, (1, nq), 1), w.bit_length() - 1)
    sinks = []
    for g in range(N_KV_HEADS):
        sink = jnp.zeros((1, nq), F32)
        for i in range(GROUP):
            sink = jnp.where(q_head == i, sinks_ref[g * GROUP + i] * LOG2E, sink)
        sinks.append(sink)
    key = lax.broadcasted_iota(jnp.int32, (w, nq), 0)
    qry = jnp.bitwise_and(lax.broadcasted_iota(jnp.int32, (w, nq), 1), w - 1)
    own = key <= qry
    ones_rows = jnp.ones((SUM_ROWS, 2 * w), BF16)

    def logits(item):
        g, n = item
        k2 = kext_ref[n * w:(n + 2) * w, :]
        k_band = jnp.where(lane_head == g, k2, jnp.zeros_like(k2))
        q_rep = jnp.concatenate(
            [jnp.concatenate(
                [qT_ref[(g * GROUP + i) * HEAD_DIM:(g * GROUP + i + 1) * HEAD_DIM, n * w:(n + 1) * w]]
                * N_KV_HEADS, axis=0) for i in range(GROUP)], axis=1)
        s_all = _dot(k_band, q_rep)
        s_prev = s_all[:w]
        if n == 0:
            s_prev = jnp.where(first_tile, NEG, s_prev)
        return jnp.where(own, s_all[w:], s_prev)

    items = [(g, n) for n in range(tm // w) for g in range(N_KV_HEADS)]
    pending = [logits(it) for it in items[:SWA_LOOKAHEAD]]
    for idx, (g, n) in enumerate(items):
        sT = pending.pop(0)
        if idx + SWA_LOOKAHEAD < len(items):
            pending.append(logits(items[idx + SWA_LOOKAHEAD]))
        m = jnp.maximum(jnp.max(sT, axis=0, keepdims=True), sinks[g])
        p = jnp.exp2(sT - m).astype(BF16)
        zero = jnp.zeros_like(p)
        p2 = jnp.concatenate([jnp.where(own, zero, p), jnp.where(own, p, zero)], axis=0)
        v_ext = jnp.concatenate(
            [vext_ref[g * HEAD_DIM:(g + 1) * HEAD_DIM, n * w:(n + 2) * w], ones_rows], axis=0)
        out_ext = _dot(v_ext, p2)
        denom = out_ext[HEAD_DIM:HEAD_DIM + 1] + jnp.exp2(sinks[g] - m)
        outT = out_ext[:HEAD_DIM] * (1.0 / denom)
        for i in range(GROUP):
            r0 = (g * GROUP + i) * HEAD_DIM
            oT_ref[r0:r0 + HEAD_DIM, n * w:(n + 1) * w] = outT[:, i * w:(i + 1) * w].astype(BF16)

        if g == N_KV_HEADS - 1 and (n + 1) * w % SUB_TILE == 0:
            cols = slice((n + 1) * w - SUB_TILE, (n + 1) * w)
            ogT = (oT_ref[:, cols].astype(F32) * sgT_ref[:, cols].astype(F32)).astype(BF16)
            half = D_MODEL // 2
            outT = jnp.concatenate(
                [_dot(woT_ref[:half, :], ogT), _dot(woT_ref[half:, :], ogT)], axis=0)
            out_ref[0, cols, :] = out_ref[0, cols, :] + outT.T


def _layer_b(sinks, o, sg, x, wo, gkv, gb, wkT, wvT, gkn, wqT, wgT, gqn, pos, invf, woT):
    b, s, d = x.shape
    tm = ROW_TILE
    full = lambda a: pl.BlockSpec(a.shape, lambda bi, i: (0,) * a.ndim)
    tok = pl.BlockSpec((1, tm, d), lambda bi, i: (bi, i, 0))
    return pl.pallas_call(
        _layer_b_kernel,
        grid=(b, s // tm),
        in_specs=[pl.BlockSpec(memory_space=pltpu.SMEM),
                  tok, tok, tok, full(wo), full(gkv), full(gb), full(wkT), full(wvT), full(gkn),
                  full(wqT), full(wgT), full(gqn),
                  pl.BlockSpec((1, tm), lambda bi, i: (0, i)), full(invf), full(woT)],
        out_specs=tok,
        out_shape=jax.ShapeDtypeStruct((b, s, d), F32),
        scratch_shapes=[pltpu.VMEM((tm + WINDOW, KV_WIDTH), BF16),
                        pltpu.VMEM((KV_WIDTH, tm + WINDOW), BF16),
                        pltpu.VMEM((d, tm), BF16),
                        pltpu.VMEM((d, tm), BF16),
                        pltpu.VMEM((d, tm), BF16)],
        compiler_params=pltpu.CompilerParams(
            dimension_semantics=("parallel", "arbitrary"), vmem_limit_bytes=VMEM_LIMIT),
        name="layer_b",
    )(sinks, o, sg, x, wo, gkv, gb, wkT, wvT, gkn, wqT, wgT, gqn, pos, invf, woT)


def _gain_col(g, n_heads, scale=1.0):
    return (jnp.tile(g.astype(F32), n_heads) * scale).reshape(n_heads * HEAD_DIM, 1)


def kernel(x, positions, norm_a_g, w_in_a, b_forget, qnorm_a_g, knorm_a_g, w_out_a, kv_norm_g, w_kv,
           knorm_b_g, norm_b_g, w_in_b, qnorm_b_g, sinks, w_out_b):
    b, s, d = x.shape
    wa = w_in_a[0]
    wqkT = wa[:, :2 * d].T.astype(BF16)
    wvg = jnp.concatenate([wa[:, 2 * d:3 * d], wa[:, 3 * d + N_HEADS:]], axis=1).astype(BF16)
    wfT = wa[:, 3 * d:3 * d + N_HEADS].T.astype(BF16)
    q, kT, v, sg, cT = _inproj_a(
        x, norm_a_g[0].reshape(1, d), wqkT, wvg, wfT, b_forget[0].reshape(N_HEADS, 1),
        _gain_col(qnorm_a_g[0], N_HEADS, SCALE * LOG2E), _gain_col(knorm_a_g[0], N_HEADS))
    o = _fox_attention(q, kT, v, cT.reshape(b, N_HEADS // 2, 2, s))

    wb = w_in_b[0]
    inv_freq = jnp.power(jnp.float32(ROPE_THETA),
                         -jnp.arange(0, ROT_DIM, 2, dtype=F32) / ROT_DIM).reshape(ROT_HALF, 1)
    return _layer_b(
        sinks[0], o, sg, x, w_out_a[0].astype(BF16), kv_norm_g.reshape(d, 1), norm_b_g[0].reshape(d, 1),
        w_kv[:, :KV_WIDTH].T.astype(BF16), w_kv[:, KV_WIDTH:].T.astype(BF16),
        _gain_col(knorm_b_g, N_KV_HEADS),
        wb[:, :d].T.astype(BF16), wb[:, d:].T.astype(BF16),
        _gain_col(qnorm_b_g[0], N_HEADS, SCALE * LOG2E),
        positions.reshape(1, s), inv_freq, w_out_b[0].T.astype(BF16))
```

```python
import jax
import jax.numpy as jnp
from jax import lax
from jax.experimental import pallas as pl
from jax.experimental.pallas import tpu as pltpu

D_MODEL = 1024
HEAD_DIM = 64
N_HEADS = 16
N_KV_HEADS = 4
GROUP = N_HEADS // N_KV_HEADS
KV_WIDTH = N_KV_HEADS * HEAD_DIM
WINDOW = 128
ROT_DIM = HEAD_DIM // 4
ROT_HALF = ROT_DIM // 2
ROPE_THETA = 500000.0
EPS = 1e-6
SCALE = HEAD_DIM ** -0.5
LOG2E = 1.4426950408889634
NEG = -1e30

F32 = jnp.float32
BF16 = jnp.bfloat16

PREP_ROWS = 128
ROW_TILE = 512
SUB_TILE = 256
FOX_Q_TILE = 256
FOX_PAIRS = 2
SWA_LOOKAHEAD = 2
SUM_ROWS = 16
VMEM_LIMIT = 56 * 1024 * 1024


def _dot(a, b):
    return jnp.dot(a, b, preferred_element_type=F32)


def _dot_nt(a, b):
    return lax.dot_general(a, b, (((1,), (1,)), ((), ())), preferred_element_type=F32)


def _head_norm_fm(t, gain_col):
    blocks = []
    for h in range(t.shape[0] // HEAD_DIM):
        blk = t[h * HEAD_DIM:(h + 1) * HEAD_DIM, :]
        ms = jnp.mean(blk * blk, axis=0, keepdims=True)
        blocks.append(blk * lax.rsqrt(ms + EPS))
    return jnp.concatenate(blocks, axis=0) * gain_col


def _rope_fm(t, cos, sin):
    blocks = []
    for h in range(t.shape[0] // HEAD_DIM):
        base = h * HEAD_DIM
        x1 = t[base:base + ROT_HALF, :]
        x2 = t[base + ROT_HALF:base + ROT_DIM, :]
        blocks.append(x1 * cos - x2 * sin)
        blocks.append(x1 * sin + x2 * cos)
        blocks.append(t[base + ROT_DIM:base + HEAD_DIM, :])
    return jnp.concatenate(blocks, axis=0)


def _split3(x):
    hi = x.astype(BF16)
    r1 = x - hi.astype(F32)
    mid = r1.astype(BF16)
    lo = (r1 - mid.astype(F32)).astype(BF16)
    return hi, mid, lo


def _prep_kernel(wa_ref, woa_ref, wkv_ref, wb_ref, wob_ref,
                 wqkT_ref, wvg_ref, wfT_ref, wo_ref, wkT_ref, wvT_ref, wqT_ref, wgT_ref, woT_ref):
    d = D_MODEL
    wa = wa_ref[0]
    wqkT_ref[...] = wa[:, :2 * d].T.astype(BF16)
    wvg_ref[:, :d] = wa[:, 2 * d:3 * d].astype(BF16)
    wvg_ref[:, d:] = wa[:, 3 * d + N_HEADS:].astype(BF16)
    wfT_ref[...] = wa[:, 3 * d:3 * d + 128].T[:N_HEADS, :].astype(BF16)
    wo_ref[...] = woa_ref[0].astype(BF16)
    wkv = wkv_ref[...]
    wkT_ref[...] = wkv[:, :KV_WIDTH].T.astype(BF16)
    wvT_ref[...] = wkv[:, KV_WIDTH:].T.astype(BF16)
    wb = wb_ref[0]
    wqT_ref[...] = wb[:, :d].T.astype(BF16)
    wgT_ref[...] = wb[:, d:].T.astype(BF16)
    woT_ref[...] = wob_ref[0].T.astype(BF16)


def _prep_weights(w_in_a, w_out_a, w_kv, w_in_b, w_out_b):
    d = D_MODEL
    r = PREP_ROWS
    rows3 = lambda a: pl.BlockSpec((1, r, a.shape[2]), lambda i: (0, i, 0))
    same = lambda n: pl.BlockSpec((r, n), lambda i: (i, 0))
    trans = lambda m: pl.BlockSpec((m, r), lambda i: (0, i))
    shapes = [((2 * d, d), trans(2 * d)), ((d, 2 * d), same(2 * d)), ((N_HEADS, d), trans(N_HEADS)),
              ((d, d), same(d)), ((KV_WIDTH, d), trans(KV_WIDTH)), ((KV_WIDTH, d), trans(KV_WIDTH)),
              ((d, d), trans(d)), ((d, d), trans(d)), ((d, d), trans(d))]
    return pl.pallas_call(
        _prep_kernel,
        grid=(d // r,),
        in_specs=[rows3(w_in_a), rows3(w_out_a), pl.BlockSpec((r, w_kv.shape[1]), lambda i: (i, 0)),
                  rows3(w_in_b), rows3(w_out_b)],
        out_specs=[spec for _, spec in shapes],
        out_shape=[jax.ShapeDtypeStruct(shape, BF16) for shape, _ in shapes],
        compiler_params=pltpu.CompilerParams(
            dimension_semantics=("parallel",), vmem_limit_bytes=VMEM_LIMIT),
        name="prep_weights",
    )(w_in_a, w_out_a, w_kv, w_in_b, w_out_b)


def _inproj_a_kernel(x_ref, g_ref, wqkT_ref, wvg_ref, wfT_ref, bf_ref, gq_ref, gk_ref,
                     q_ref, kT_ref, v_ref, sg_ref, cT_ref, carry_ref):
    tm = x_ref.shape[1]

    @pl.when(pl.program_id(1) == 0)
    def _():
        carry_ref[...] = jnp.zeros_like(carry_ref)

    u_parts = []
    for rows in [slice(j * SUB_TILE, (j + 1) * SUB_TILE) for j in range(tm // SUB_TILE)]:
        x = x_ref[0, rows, :]
        ms = jnp.mean(x * x, axis=-1, keepdims=True)
        u32 = x * lax.rsqrt(ms + EPS) * g_ref[...]
        u_parts.append(u32.astype(BF16))
        uT = u32.T.astype(BF16)
        qT = _head_norm_fm(_dot(wqkT_ref[:D_MODEL, :], uT), gq_ref[...])
        kT = _head_norm_fm(_dot(wqkT_ref[D_MODEL:, :], uT), gk_ref[...])
        q_ref[0, rows, :] = qT.T.astype(BF16)
        kT_ref[0, :, rows] = kT.astype(BF16)
    u = jnp.concatenate(u_parts, axis=0)

    f = _dot_nt(wfT_ref[...], u) + bf_ref[...]
    gate = _dot(u, wvg_ref[:, D_MODEL:])
    sg_ref[0] = (gate * jax.nn.sigmoid(gate)).astype(BF16)

    log_f = jnp.minimum(f, 0.0) - jnp.log1p(jnp.exp(-jnp.abs(f)))
    row = lax.broadcasted_iota(jnp.int32, (tm, tm), 0)
    col = lax.broadcasted_iota(jnp.int32, (tm, tm), 1)
    tri = (row <= col).astype(BF16)
    hi, mid, lo = _split3(log_f)
    c = (_dot(hi, tri) + _dot(mid, tri) + _dot(lo, tri)) + carry_ref[:, 0:1]
    v_ref[0] = _dot(u, wvg_ref[:, :D_MODEL]).astype(BF16)
    cT_ref[0] = c * LOG2E
    carry_ref[...] = jnp.broadcast_to(c[:, tm - 1:tm], carry_ref.shape)


def _inproj_a(x, g, wqkT, wvg, wfT, bf, gq, gk):
    b, s, d = x.shape
    tm = ROW_TILE
    full = lambda shape: pl.BlockSpec(shape, lambda bi, i: (0,) * len(shape))
    tok = pl.BlockSpec((1, tm, d), lambda bi, i: (bi, i, 0))
    return pl.pallas_call(
        _inproj_a_kernel,
        grid=(b, s // tm),
        in_specs=[tok, full(g.shape), full(wqkT.shape), full(wvg.shape), full(wfT.shape),
                  full(bf.shape), full(gq.shape), full(gk.shape)],
        out_specs=[tok,
                   pl.BlockSpec((1, d, tm), lambda bi, i: (bi, 0, i)),
                   tok, tok,
                   pl.BlockSpec((1, N_HEADS, tm), lambda bi, i: (bi, 0, i))],
        out_shape=[jax.ShapeDtypeStruct((b, s, d), BF16),
                   jax.ShapeDtypeStruct((b, d, s), BF16),
                   jax.ShapeDtypeStruct((b, s, d), BF16),
                   jax.ShapeDtypeStruct((b, s, d), BF16),
                   jax.ShapeDtypeStruct((b, N_HEADS, s), F32)],
        scratch_shapes=[pltpu.VMEM((N_HEADS, 128), F32)],
        compiler_params=pltpu.CompilerParams(
            dimension_semantics=("parallel", "arbitrary"), vmem_limit_bytes=VMEM_LIMIT),
        name="inproj_a",
    )(x, g, wqkT, wvg, wfT, bf, gq, gk)


BIAS_ROWS = 16


def _fox_kernel(q_ref, kT_ref, v_ref, c_ref, o_ref, bias_ref, vext_ref):
    s_len = q_ref.shape[1]
    n_pairs = q_ref.shape[2] // 128
    tq = FOX_Q_TILE
    lane = lax.broadcasted_iota(jnp.int32, (tq, 128), 1)
    first = lane < HEAD_DIM
    ones_a = jnp.where(lane < 3, 1.0, 0.0).astype(BF16)
    ones_b = jnp.where(jnp.logical_and(lane >= 3, lane < 6), 1.0, 0.0).astype(BF16)
    row = lax.broadcasted_iota(jnp.int32, (2 * tq, tq), 0) % tq
    col = lax.broadcasted_iota(jnp.int32, (2 * tq, tq), 1)
    causal = col <= row

    brow = lax.broadcasted_iota(jnp.int32, (BIAS_ROWS, s_len), 0)
    for p in range(n_pairs):
        parts = _split3(-c_ref[0, p, 0:1, :]) + _split3(-c_ref[0, p, 1:2, :])
        bias = jnp.zeros((BIAS_ROWS, s_len), F32)
        for r, part in enumerate(parts):
            bias = jnp.where(brow == r, part.astype(F32), bias)
        bias_ref[p] = bias.astype(BF16)
        vext_ref[p, :, :128] = v_ref[0, :, p * 128:(p + 1) * 128]
        vext_ref[p, :, 128:] = jnp.ones((s_len, 128), BF16)

    def rhs(p, c0, c1):
        pad = jnp.zeros((128 - BIAS_ROWS, c1 - c0), BF16)
        return jnp.concatenate(
            [kT_ref[0, p * 128:(p + 1) * 128, c0:c1], bias_ref[p, :, c0:c1], pad], axis=0)

    def logits(item):
        p, qb = item
        r0 = qb * tq
        q2 = q_ref[0, r0:r0 + tq, p * 128:(p + 1) * 128]
        zero = jnp.zeros_like(q2)
        lhs = jnp.concatenate(
            [jnp.concatenate([jnp.where(first, q2, zero), ones_a], axis=1),
             jnp.concatenate([jnp.where(first, zero, q2), ones_b], axis=1)], axis=0)
        s_diag = jnp.where(causal, _dot(lhs, rhs(p, r0, r0 + tq)), NEG)
        s_off = _dot(lhs, rhs(p, 0, r0)) if qb > 0 else None
        return s_diag, s_off

    order = [(p, qb) for qb in reversed(range(s_len // tq)) for p in range(n_pairs)]
    pending = logits(order[0])
    for idx, (p, qb) in enumerate(order):
        r0 = qb * tq
        s_diag, s_off = pending
        if idx + 1 < len(order):
            pending = logits(order[idx + 1])
        m = jnp.max(s_diag, axis=1, keepdims=True)
        if qb > 0:
            m = jnp.maximum(m, jnp.max(s_off, axis=1, keepdims=True))
        pv = _dot(jnp.exp2(s_diag - m).astype(BF16), vext_ref[p, r0:r0 + tq, :])
        if qb > 0:
            pv = pv + _dot(jnp.exp2(s_off - m).astype(BF16), vext_ref[p, :r0, :])
        o = pv[:, :128] / pv[:, 128:]
        o_ref[0, r0:r0 + tq, p * 128:(p + 1) * 128] = jnp.where(first, o[:tq], o[tq:]).astype(BF16)


def _fox_attention(q, kT, v, c4):
    b, s, d = q.shape
    np_, wd = FOX_PAIRS, FOX_PAIRS * 128
    return pl.pallas_call(
        _fox_kernel,
        grid=(b, d // wd),
        in_specs=[pl.BlockSpec((1, s, wd), lambda bi, j: (bi, 0, j)),
                  pl.BlockSpec((1, wd, s), lambda bi, j: (bi, j, 0)),
                  pl.BlockSpec((1, s, wd), lambda bi, j: (bi, 0, j)),
                  pl.BlockSpec((1, np_, 2, s), lambda bi, j: (bi, j, 0, 0))],
        out_specs=pl.BlockSpec((1, s, wd), lambda bi, j: (bi, 0, j)),
        out_shape=jax.ShapeDtypeStruct((b, s, d), BF16),
        scratch_shapes=[pltpu.VMEM((np_, BIAS_ROWS, s), BF16), pltpu.VMEM((np_, s, 256), BF16)],
        compiler_params=pltpu.CompilerParams(
            dimension_semantics=("parallel", "parallel"), vmem_limit_bytes=VMEM_LIMIT),
        name="fox_attention",
    )(q, kT, v, c4)


def _layer_b_kernel(sinks_ref, o_ref, sg_ref, x_ref, wo_ref, gkv_ref, gb_ref, wkT_ref, wvT_ref,
                    gkn_ref, wqT_ref, wgT_ref, gqn_ref, pos_ref, invf_ref, woT_ref,
                    out_ref, kext_ref, vext_ref, qT_ref, sgT_ref, oT_ref):
    tm = x_ref.shape[1]
    w = WINDOW
    nq = GROUP * w
    first_tile = pl.program_id(1) == 0

    parts = [slice(j * SUB_TILE, (j + 1) * SUB_TILE) for j in range(tm // SUB_TILE)]
    for rows in parts:
        og = (o_ref[0, rows, :].astype(F32) * sg_ref[0, rows, :].astype(F32)).astype(BF16)
        out_ref[0, rows, :] = x_ref[0, rows, :] + _dot(og, wo_ref[...])

    @pl.when(first_tile)
    def _():
        kext_ref[0:w, :] = jnp.zeros((w, KV_WIDTH), BF16)
        vext_ref[:, 0:w] = jnp.zeros((KV_WIDTH, w), BF16)

    @pl.when(jnp.logical_not(first_tile))
    def _():
        kext_ref[0:w, :] = kext_ref[tm:tm + w, :]
        vext_ref[:, 0:w] = vext_ref[:, tm:tm + w]

    for cols in parts:
        h = out_ref[0, cols, :]
        ms = jnp.mean(h * h, axis=-1, keepdims=True)
        hn = h * lax.rsqrt(ms + EPS)
        hnT = hn.T
        u_kv = (hnT * gkv_ref[...]).astype(BF16)
        u_b = (hnT * gb_ref[...]).astype(BF16)

        ang = invf_ref[...] * pos_ref[:, cols].astype(F32)
        cos = jnp.cos(ang)
        sin = jnp.sin(ang)

        ext = slice(w + cols.start, w + cols.stop)
        kT = _head_norm_fm(_dot(wkT_ref[...], u_kv), gkn_ref[...])
        kext_ref[ext, :] = _rope_fm(kT, cos, sin).T.astype(BF16)
        vext_ref[:, ext] = _dot(wvT_ref[...], u_kv).astype(BF16)

        qT = _head_norm_fm(_dot(wqT_ref[...], u_b), gqn_ref[...])
        qT_ref[:, cols] = _rope_fm(qT, cos, sin).astype(BF16)
        gateT = _dot(wgT_ref[...], u_b)
        sgT_ref[:, cols] = (gateT * jax.nn.sigmoid(gateT)).astype(BF16)

    lane_head = lax.shift_right_logical(
        lax.broadcasted_iota(jnp.int32, (2 * w, KV_WIDTH), 1), HEAD_DIM.bit_length() - 1)
    q_head = lax.shift_right_logical(lax.broadcasted_iota(jnp.int32, (1, nq), 1), w.bit_length() - 1)
    sinks = []
    for g in range(N_KV_HEADS):
        sink = jnp.zeros((1, nq), F32)
        for i in range(GROUP):
            sink = jnp.where(q_head == i, sinks_ref[g * GROUP + i] * LOG2E, sink)
        sinks.append(sink)
    key = lax.broadcasted_iota(jnp.int32, (w, nq), 0)
    qry = jnp.bitwise_and(lax.broadcasted_iota(jnp.int32, (w, nq), 1), w - 1)
    own = key <= qry
    ones_rows = jnp.ones((SUM_ROWS, 2 * w), BF16)

    def logits(item):
        g, n = item
        k2 = kext_ref[n * w:(n + 2) * w, :]
        k_band = jnp.where(lane_head == g, k2, jnp.zeros_like(k2))
        q_rep = jnp.concatenate(
            [jnp.concatenate(
                [qT_ref[(g * GROUP + i) * HEAD_DIM:(g * GROUP + i + 1) * HEAD_DIM, n * w:(n + 1) * w]]
                * N_KV_HEADS, axis=0) for i in range(GROUP)], axis=1)
        s_all = _dot(k_band, q_rep)
        s_prev = s_all[:w]
        if n == 0:
            s_prev = jnp.where(first_tile, NEG, s_prev)
        return jnp.where(own, s_all[w:], s_prev)

    items = [(g, n) for n in range(tm // w) for g in range(N_KV_HEADS)]
    pending = [logits(it) for it in items[:SWA_LOOKAHEAD]]
    for idx, (g, n) in enumerate(items):
        sT = pending.pop(0)
        if idx + SWA_LOOKAHEAD < len(items):
            pending.append(logits(items[idx + SWA_LOOKAHEAD]))
        m = jnp.maximum(jnp.max(sT, axis=0, keepdims=True), sinks[g])
        p = jnp.exp2(sT - m).astype(BF16)
        zero = jnp.zeros_like(p)
        p2 = jnp.concatenate([jnp.where(own, zero, p), jnp.where(own, p, zero)], axis=0)
        v_ext = jnp.concatenate(
            [vext_ref[g * HEAD_DIM:(g + 1) * HEAD_DIM, n * w:(n + 2) * w], ones_rows], axis=0)
        out_ext = _dot(v_ext, p2)
        denom = out_ext[HEAD_DIM:HEAD_DIM + 1] + jnp.exp2(sinks[g] - m)
        outT = out_ext[:HEAD_DIM] * (1.0 / denom)
        for i in range(GROUP):
            r0 = (g * GROUP + i) * HEAD_DIM
            oT_ref[r0:r0 + HEAD_DIM, n * w:(n + 1) * w] = outT[:, i * w:(i + 1) * w].astype(BF16)

        if g == N_KV_HEADS - 1 and (n + 1) * w % SUB_TILE == 0:
            cols = slice((n + 1) * w - SUB_TILE, (n + 1) * w)
            ogT = (oT_ref[:, cols].astype(F32) * sgT_ref[:, cols].astype(F32)).astype(BF16)
            half = D_MODEL // 2
            outT = jnp.concatenate(
                [_dot(woT_ref[:half, :], ogT), _dot(woT_ref[half:, :], ogT)], axis=0)
            out_ref[0, cols, :] = out_ref[0, cols, :] + outT.T


def _layer_b(sinks, o, sg, x, wo, gkv, gb, wkT, wvT, gkn, wqT, wgT, gqn, pos, invf, woT):
    b, s, d = x.shape
    tm = ROW_TILE
    full = lambda a: pl.BlockSpec(a.shape, lambda bi, i: (0,) * a.ndim)
    tok = pl.BlockSpec((1, tm, d), lambda bi, i: (bi, i, 0))
    return pl.pallas_call(
        _layer_b_kernel,
        grid=(b, s // tm),
        in_specs=[pl.BlockSpec(memory_space=pltpu.SMEM),
                  tok, tok, tok, full(wo), full(gkv), full(gb), full(wkT), full(wvT), full(gkn),
                  full(wqT), full(wgT), full(gqn),
                  pl.BlockSpec((1, tm), lambda bi, i: (0, i)), full(invf), full(woT)],
        out_specs=tok,
        out_shape=jax.ShapeDtypeStruct((b, s, d), F32),
        scratch_shapes=[pltpu.VMEM((tm + WINDOW, KV_WIDTH), BF16),
                        pltpu.VMEM((KV_WIDTH, tm + WINDOW), BF16),
                        pltpu.VMEM((d, tm), BF16),
                        pltpu.VMEM((d, tm), BF16),
                        pltpu.VMEM((d, tm), BF16)],
        compiler_params=pltpu.CompilerParams(
            dimension_semantics=("parallel", "arbitrary"), vmem_limit_bytes=VMEM_LIMIT),
        name="layer_b",
    )(sinks, o, sg, x, wo, gkv, gb, wkT, wvT, gkn, wqT, wgT, gqn, pos, invf, woT)


def _gain_col(g, n_heads, scale=1.0):
    return (jnp.tile(g.astype(F32), n_heads) * scale).reshape(n_heads * HEAD_DIM, 1)


def kernel(x, positions, norm_a_g, w_in_a, b_forget, qnorm_a_g, knorm_a_g, w_out_a, kv_norm_g, w_kv,
           knorm_b_g, norm_b_g, w_in_b, qnorm_b_g, sinks, w_out_b):
    b, s, d = x.shape
    wqkT, wvg, wfT, wo_a, wkT, wvT, wqT, wgT, woT_b = _prep_weights(w_in_a, w_out_a, w_kv, w_in_b, w_out_b)
    q, kT, v, sg, cT = _inproj_a(
        x, norm_a_g[0].reshape(1, d), wqkT, wvg, wfT, b_forget[0].reshape(N_HEADS, 1),
        _gain_col(qnorm_a_g[0], N_HEADS, SCALE * LOG2E), _gain_col(knorm_a_g[0], N_HEADS))
    o = _fox_attention(q, kT, v, cT.reshape(b, N_HEADS // 2, 2, s))

    inv_freq = jnp.power(jnp.float32(ROPE_THETA),
                         -jnp.arange(0, ROT_DIM, 2, dtype=F32) / ROT_DIM).reshape(ROT_HALF, 1)
    return _layer_b(
        sinks[0], o, sg, x, wo_a, kv_norm_g.reshape(d, 1), norm_b_g[0].reshape(d, 1),
        wkT, wvT, _gain_col(knorm_b_g, N_KV_HEADS),
        wqT, wgT, _gain_col(qnorm_b_g[0], N_HEADS, SCALE * LOG2E),
        positions.reshape(1, s), inv_freq, woT_b)
```

```python
import jax
import jax.numpy as jnp
from jax import lax
from jax.experimental import pallas as pl
from jax.experimental.pallas import tpu as pltpu

D_MODEL = 1024
HEAD_DIM = 64
N_HEADS = 16
N_KV_HEADS = 4
GROUP = N_HEADS // N_KV_HEADS
KV_WIDTH = N_KV_HEADS * HEAD_DIM
WINDOW = 128
ROT_DIM = HEAD_DIM // 4
ROT_HALF = ROT_DIM // 2
ROPE_THETA = 500000.0
EPS = 1e-6
SCALE = HEAD_DIM ** -0.5
LOG2E = 1.4426950408889634
NEG = -1e30

F32 = jnp.float32
BF16 = jnp.bfloat16

PREP_ROWS = 128
ROW_TILE = 512
SUB_TILE = 256
FOX_Q_TILE = 256
FOX_PAIRS = 2
SWA_LOOKAHEAD = 2
SUM_ROWS = 16
VMEM_LIMIT = 56 * 1024 * 1024


def _dot(a, b):
    return jnp.dot(a, b, preferred_element_type=F32)


def _dot_nt(a, b):
    return lax.dot_general(a, b, (((1,), (1,)), ((), ())), preferred_element_type=F32)


def _head_norm_fm(t, gain_col):
    blocks = []
    for h in range(t.shape[0] // HEAD_DIM):
        blk = t[h * HEAD_DIM:(h + 1) * HEAD_DIM, :]
        ms = jnp.mean(blk * blk, axis=0, keepdims=True)
        blocks.append(blk * lax.rsqrt(ms + EPS))
    return jnp.concatenate(blocks, axis=0) * gain_col


def _rope_fm(t, cos, sin):
    blocks = []
    for h in range(t.shape[0] // HEAD_DIM):
        base = h * HEAD_DIM
        x1 = t[base:base + ROT_HALF, :]
        x2 = t[base + ROT_HALF:base + ROT_DIM, :]
        blocks.append(x1 * cos - x2 * sin)
        blocks.append(x1 * sin + x2 * cos)
        blocks.append(t[base + ROT_DIM:base + HEAD_DIM, :])
    return jnp.concatenate(blocks, axis=0)


def _split3(x):
    hi = x.astype(BF16)
    r1 = x - hi.astype(F32)
    mid = r1.astype(BF16)
    lo = (r1 - mid.astype(F32)).astype(BF16)
    return hi, mid, lo


def _prep_kernel(waT_ref, woa_ref, wkv_ref, wb_ref, wob_ref,
                 wqkT_ref, wvg_ref, wfT_ref, wo_ref, wkT_ref, wvT_ref, wqT_ref, wgT_ref, woT_ref):
    d = D_MODEL
    wqkT_ref[...] = waT_ref[:2 * d, :].astype(BF16)
    wvg_ref[:, :d] = waT_ref[2 * d:3 * d, :].T.astype(BF16)
    wvg_ref[:, d:] = waT_ref[3 * d + N_HEADS:, :].T.astype(BF16)
    wfT_ref[...] = waT_ref[3 * d:3 * d + N_HEADS, :].astype(BF16)
    wo_ref[...] = woa_ref[0].astype(BF16)
    wkv = wkv_ref[...]
    wkT_ref[...] = wkv[:, :KV_WIDTH].T.astype(BF16)
    wvT_ref[...] = wkv[:, KV_WIDTH:].T.astype(BF16)
    wb = wb_ref[0]
    wqT_ref[...] = wb[:, :d].T.astype(BF16)
    wgT_ref[...] = wb[:, d:].T.astype(BF16)
    woT_ref[...] = wob_ref[0].T.astype(BF16)


def _prep_weights(w_in_aT, w_out_a, w_kv, w_in_b, w_out_b):
    d = D_MODEL
    r = PREP_ROWS
    rows3 = lambda a: pl.BlockSpec((1, r, a.shape[2]), lambda i: (0, i, 0))
    same = lambda n: pl.BlockSpec((r, n), lambda i: (i, 0))
    trans = lambda m: pl.BlockSpec((m, r), lambda i: (0, i))
    shapes = [((2 * d, d), trans(2 * d)), ((d, 2 * d), same(2 * d)), ((N_HEADS, d), trans(N_HEADS)),
              ((d, d), same(d)), ((KV_WIDTH, d), trans(KV_WIDTH)), ((KV_WIDTH, d), trans(KV_WIDTH)),
              ((d, d), trans(d)), ((d, d), trans(d)), ((d, d), trans(d))]
    return pl.pallas_call(
        _prep_kernel,
        grid=(d // r,),
        in_specs=[pl.BlockSpec((w_in_aT.shape[0], r), lambda i: (0, i)),
                  rows3(w_out_a), pl.BlockSpec((r, w_kv.shape[1]), lambda i: (i, 0)),
                  rows3(w_in_b), rows3(w_out_b)],
        out_specs=[spec for _, spec in shapes],
        out_shape=[jax.ShapeDtypeStruct(shape, BF16) for shape, _ in shapes],
        compiler_params=pltpu.CompilerParams(
            dimension_semantics=("parallel",), vmem_limit_bytes=VMEM_LIMIT),
        name="prep_weights",
    )(w_in_aT, w_out_a, w_kv, w_in_b, w_out_b)


def _inproj_a_kernel(x_ref, g_ref, wqkT_ref, wvg_ref, wfT_ref, bf_ref, gq_ref, gk_ref,
                     q_ref, kT_ref, v_ref, sg_ref, cT_ref, carry_ref):
    tm = x_ref.shape[1]

    @pl.when(pl.program_id(1) == 0)
    def _():
        carry_ref[...] = jnp.zeros_like(carry_ref)

    u_parts = []
    for rows in [slice(j * SUB_TILE, (j + 1) * SUB_TILE) for j in range(tm // SUB_TILE)]:
        x = x_ref[0, rows, :]
        ms = jnp.mean(x * x, axis=-1, keepdims=True)
        u32 = x * lax.rsqrt(ms + EPS) * g_ref[...]
        u_parts.append(u32.astype(BF16))
        uT = u32.T.astype(BF16)
        qT = _head_norm_fm(_dot(wqkT_ref[:D_MODEL, :], uT), gq_ref[...])
        kT = _head_norm_fm(_dot(wqkT_ref[D_MODEL:, :], uT), gk_ref[...])
        q_ref[0, rows, :] = qT.T.astype(BF16)
        kT_ref[0, :, rows] = kT.astype(BF16)
    u = jnp.concatenate(u_parts, axis=0)

    f = _dot_nt(wfT_ref[...], u) + bf_ref[...]
    gate = _dot(u, wvg_ref[:, D_MODEL:])
    sg_ref[0] = (gate * jax.nn.sigmoid(gate)).astype(BF16)

    log_f = jnp.minimum(f, 0.0) - jnp.log1p(jnp.exp(-jnp.abs(f)))
    row = lax.broadcasted_iota(jnp.int32, (tm, tm), 0)
    col = lax.broadcasted_iota(jnp.int32, (tm, tm), 1)
    tri = (row <= col).astype(BF16)
    hi, mid, lo = _split3(log_f)
    c = (_dot(hi, tri) + _dot(mid, tri) + _dot(lo, tri)) + carry_ref[:, 0:1]
    v_ref[0] = _dot(u, wvg_ref[:, :D_MODEL]).astype(BF16)
    cT_ref[0] = c * LOG2E
    carry_ref[...] = jnp.broadcast_to(c[:, tm - 1:tm], carry_ref.shape)


def _inproj_a(x, g, wqkT, wvg, wfT, bf, gq, gk):
    b, s, d = x.shape
    tm = ROW_TILE
    full = lambda shape: pl.BlockSpec(shape, lambda bi, i: (0,) * len(shape))
    tok = pl.BlockSpec((1, tm, d), lambda bi, i: (bi, i, 0))
    return pl.pallas_call(
        _inproj_a_kernel,
        grid=(b, s // tm),
        in_specs=[tok, full(g.shape), full(wqkT.shape), full(wvg.shape), full(wfT.shape),
                  full(bf.shape), full(gq.shape), full(gk.shape)],
        out_specs=[tok,
                   pl.BlockSpec((1, d, tm), lambda bi, i: (bi, 0, i)),
                   tok, tok,
                   pl.BlockSpec((1, N_HEADS, tm), lambda bi, i: (bi, 0, i))],
        out_shape=[jax.ShapeDtypeStruct((b, s, d), BF16),
                   jax.ShapeDtypeStruct((b, d, s), BF16),
                   jax.ShapeDtypeStruct((b, s, d), BF16),
                   jax.ShapeDtypeStruct((b, s, d), BF16),
                   jax.ShapeDtypeStruct((b, N_HEADS, s), F32)],
        scratch_shapes=[pltpu.VMEM((N_HEADS, 128), F32)],
        compiler_params=pltpu.CompilerParams(
            dimension_semantics=("parallel", "arbitrary"), vmem_limit_bytes=VMEM_LIMIT),
        name="inproj_a",
    )(x, g, wqkT, wvg, wfT, bf, gq, gk)


BIAS_ROWS = 16


def _fox_kernel(q_ref, kT_ref, v_ref, c_ref, o_ref, bias_ref, vext_ref):
    s_len = q_ref.shape[1]
    n_pairs = q_ref.shape[2] // 128
    tq = FOX_Q_TILE
    lane = lax.broadcasted_iota(jnp.int32, (tq, 128), 1)
    first = lane < HEAD_DIM
    ones_a = jnp.where(lane < 3, 1.0, 0.0).astype(BF16)
    ones_b = jnp.where(jnp.logical_and(lane >= 3, lane < 6), 1.0, 0.0).astype(BF16)
    row = lax.broadcasted_iota(jnp.int32, (2 * tq, tq), 0) % tq
    col = lax.broadcasted_iota(jnp.int32, (2 * tq, tq), 1)
    causal = col <= row

    brow = lax.broadcasted_iota(jnp.int32, (BIAS_ROWS, s_len), 0)
    for p in range(n_pairs):
        parts = _split3(-c_ref[0, p, 0:1, :]) + _split3(-c_ref[0, p, 1:2, :])
        bias = jnp.zeros((BIAS_ROWS, s_len), F32)
        for r, part in enumerate(parts):
            bias = jnp.where(brow == r, part.astype(F32), bias)
        bias_ref[p] = bias.astype(BF16)
        vext_ref[p, :, :128] = v_ref[0, :, p * 128:(p + 1) * 128]
        vext_ref[p, :, 128:] = jnp.ones((s_len, 128), BF16)

    def rhs(p, c0, c1):
        pad = jnp.zeros((128 - BIAS_ROWS, c1 - c0), BF16)
        return jnp.concatenate(
            [kT_ref[0, p * 128:(p + 1) * 128, c0:c1], bias_ref[p, :, c0:c1], pad], axis=0)

    def logits(item):
        p, qb = item
        r0 = qb * tq
        q2 = q_ref[0, r0:r0 + tq, p * 128:(p + 1) * 128]
        zero = jnp.zeros_like(q2)
        lhs = jnp.concatenate(
            [jnp.concatenate([jnp.where(first, q2, zero), ones_a], axis=1),
             jnp.concatenate([jnp.where(first, zero, q2), ones_b], axis=1)], axis=0)
        s_diag = jnp.where(causal, _dot(lhs, rhs(p, r0, r0 + tq)), NEG)
        s_off = _dot(lhs, rhs(p, 0, r0)) if qb > 0 else None
        return s_diag, s_off

    order = [(p, qb) for qb in reversed(range(s_len // tq)) for p in range(n_pairs)]
    pending = logits(order[0])
    for idx, (p, qb) in enumerate(order):
        r0 = qb * tq
        s_diag, s_off = pending
        if idx + 1 < len(order):
            pending = logits(order[idx + 1])
        m = jnp.max(s_diag, axis=1, keepdims=True)
        if qb > 0:
            m = jnp.maximum(m, jnp.max(s_off, axis=1, keepdims=True))
        pv = _dot(jnp.exp2(s_diag - m).astype(BF16), vext_ref[p, r0:r0 + tq, :])
        if qb > 0:
            pv = pv + _dot(jnp.exp2(s_off - m).astype(BF16), vext_ref[p, :r0, :])
        o = pv[:, :128] / pv[:, 128:]
        o_ref[0, r0:r0 + tq, p * 128:(p + 1) * 128] = jnp.where(first, o[:tq], o[tq:]).astype(BF16)


def _fox_attention(q, kT, v, c4):
    b, s, d = q.shape
    np_, wd = FOX_PAIRS, FOX_PAIRS * 128
    return pl.pallas_call(
        _fox_kernel,
        grid=(b, d // wd),
        in_specs=[pl.BlockSpec((1, s, wd), lambda bi, j: (bi, 0, j)),
                  pl.BlockSpec((1, wd, s), lambda bi, j: (bi, j, 0)),
                  pl.BlockSpec((1, s, wd), lambda bi, j: (bi, 0, j)),
                  pl.BlockSpec((1, np_, 2, s), lambda bi, j: (bi, j, 0, 0))],
        out_specs=pl.BlockSpec((1, s, wd), lambda bi, j: (bi, 0, j)),
        out_shape=jax.ShapeDtypeStruct((b, s, d), BF16),
        scratch_shapes=[pltpu.VMEM((np_, BIAS_ROWS, s), BF16), pltpu.VMEM((np_, s, 256), BF16)],
        compiler_params=pltpu.CompilerParams(
            dimension_semantics=("parallel", "parallel"), vmem_limit_bytes=VMEM_LIMIT),
        name="fox_attention",
    )(q, kT, v, c4)


def _layer_b_kernel(sinks_ref, o_ref, sg_ref, x_ref, wo_ref, gkv_ref, gb_ref, wkT_ref, wvT_ref,
                    gkn_ref, wqT_ref, wgT_ref, gqn_ref, pos_ref, invf_ref, woT_ref,
                    out_ref, kext_ref, vext_ref, qT_ref, sgT_ref, oT_ref):
    tm = x_ref.shape[1]
    w = WINDOW
    nq = GROUP * w
    first_tile = pl.program_id(1) == 0

    parts = [slice(j * SUB_TILE, (j + 1) * SUB_TILE) for j in range(tm // SUB_TILE)]
    for rows in parts:
        og = (o_ref[0, rows, :].astype(F32) * sg_ref[0, rows, :].astype(F32)).astype(BF16)
        out_ref[0, rows, :] = x_ref[0, rows, :] + _dot(og, wo_ref[...])

    @pl.when(first_tile)
    def _():
        kext_ref[0:w, :] = jnp.zeros((w, KV_WIDTH), BF16)
        vext_ref[:, 0:w] = jnp.zeros((KV_WIDTH, w), BF16)

    @pl.when(jnp.logical_not(first_tile))
    def _():
        kext_ref[0:w, :] = kext_ref[tm:tm + w, :]
        vext_ref[:, 0:w] = vext_ref[:, tm:tm + w]

    for cols in parts:
        h = out_ref[0, cols, :]
        ms = jnp.mean(h * h, axis=-1, keepdims=True)
        hn = h * lax.rsqrt(ms + EPS)
        hnT = hn.T
        u_kv = (hnT * gkv_ref[...]).astype(BF16)
        u_b = (hnT * gb_ref[...]).astype(BF16)

        ang = invf_ref[...] * pos_ref[:, cols].astype(F32)
        cos = jnp.cos(ang)
        sin = jnp.sin(ang)

        ext = slice(w + cols.start, w + cols.stop)
        kT = _head_norm_fm(_dot(wkT_ref[...], u_kv), gkn_ref[...])
        kext_ref[ext, :] = _rope_fm(kT, cos, sin).T.astype(BF16)
        vext_ref[:, ext] = _dot(wvT_ref[...], u_kv).astype(BF16)

        qT = _head_norm_fm(_dot(wqT_ref[...], u_b), gqn_ref[...])
        qT_ref[:, cols] = _rope_fm(qT, cos, sin).astype(BF16)
        gateT = _dot(wgT_ref[...], u_b)
        sgT_ref[:, cols] = (gateT * jax.nn.sigmoid(gateT)).astype(BF16)

    lane_head = lax.shift_right_logical(
        lax.broadcasted_iota(jnp.int32, (2 * w, KV_WIDTH), 1), HEAD_DIM.bit_length() - 1)
    q_head = lax.shift_right_logical(lax.broadcasted_iota(jnp.int32, (1, nq), 1), w.bit_length() - 1)
    sinks = []
    for g in range(N_KV_HEADS):
        sink = jnp.zeros((1, nq), F32)
        for i in range(GROUP):
            sink = jnp.where(q_head == i, sinks_ref[g * GROUP + i] * LOG2E, sink)
        sinks.append(sink)
    key = lax.broadcasted_iota(jnp.int32, (w, nq), 0)
    qry = jnp.bitwise_and(lax.broadcasted_iota(jnp.int32, (w, nq), 1), w - 1)
    own = key <= qry
    ones_rows = jnp.ones((SUM_ROWS, 2 * w), BF16)

    def logits(item):
        g, n = item
        k2 = kext_ref[n * w:(n + 2) * w, :]
        k_band = jnp.where(lane_head == g, k2, jnp.zeros_like(k2))
        q_rep = jnp.concatenate(
            [jnp.concatenate(
                [qT_ref[(g * GROUP + i) * HEAD_DIM:(g * GROUP + i + 1) * HEAD_DIM, n * w:(n + 1) * w]]
                * N_KV_HEADS, axis=0) for i in range(GROUP)], axis=1)
        s_all = _dot(k_band, q_rep)
        s_prev = s_all[:w]
        if n == 0:
            s_prev = jnp.where(first_tile, NEG, s_prev)
        return jnp.where(own, s_all[w:], s_prev)

    items = [(g, n) for n in range(tm // w) for g in range(N_KV_HEADS)]
    pending = [logits(it) for it in items[:SWA_LOOKAHEAD]]
    for idx, (g, n) in enumerate(items):
        sT = pending.pop(0)
        if idx + SWA_LOOKAHEAD < len(items):
            pending.append(logits(items[idx + SWA_LOOKAHEAD]))
        m = jnp.maximum(jnp.max(sT, axis=0, keepdims=True), sinks[g])
        p = jnp.exp2(sT - m).astype(BF16)
        zero = jnp.zeros_like(p)
        p2 = jnp.concatenate([jnp.where(own, zero, p), jnp.where(own, p, zero)], axis=0)
        v_ext = jnp.concatenate(
            [vext_ref[g * HEAD_DIM:(g + 1) * HEAD_DIM, n * w:(n + 2) * w], ones_rows], axis=0)
        out_ext = _dot(v_ext, p2)
        denom = out_ext[HEAD_DIM:HEAD_DIM + 1] + jnp.exp2(sinks[g] - m)
        outT = out_ext[:HEAD_DIM] * (1.0 / denom)
        for i in range(GROUP):
            r0 = (g * GROUP + i) * HEAD_DIM
            oT_ref[r0:r0 + HEAD_DIM, n * w:(n + 1) * w] = outT[:, i * w:(i + 1) * w].astype(BF16)

        if g == N_KV_HEADS - 1 and (n + 1) * w % SUB_TILE == 0:
            cols = slice((n + 1) * w - SUB_TILE, (n + 1) * w)
            ogT = (oT_ref[:, cols].astype(F32) * sgT_ref[:, cols].astype(F32)).astype(BF16)
            half = D_MODEL // 2
            outT = jnp.concatenate(
                [_dot(woT_ref[:half, :], ogT), _dot(woT_ref[half:, :], ogT)], axis=0)
            out_ref[0, cols, :] = out_ref[0, cols, :] + outT.T


def _layer_b(sinks, o, sg, x, wo, gkv, gb, wkT, wvT, gkn, wqT, wgT, gqn, pos, invf, woT):
    b, s, d = x.shape
    tm = ROW_TILE
    full = lambda a: pl.BlockSpec(a.shape, lambda bi, i: (0,) * a.ndim)
    tok = pl.BlockSpec((1, tm, d), lambda bi, i: (bi, i, 0))
    return pl.pallas_call(
        _layer_b_kernel,
        grid=(b, s // tm),
        in_specs=[pl.BlockSpec(memory_space=pltpu.SMEM),
                  tok, tok, tok, full(wo), full(gkv), full(gb), full(wkT), full(wvT), full(gkn),
                  full(wqT), full(wgT), full(gqn),
                  pl.BlockSpec((1, tm), lambda bi, i: (0, i)), full(invf), full(woT)],
        out_specs=tok,
        out_shape=jax.ShapeDtypeStruct((b, s, d), F32),
        scratch_shapes=[pltpu.VMEM((tm + WINDOW, KV_WIDTH), BF16),
                        pltpu.VMEM((KV_WIDTH, tm + WINDOW), BF16),
                        pltpu.VMEM((d, tm), BF16),
                        pltpu.VMEM((d, tm), BF16),
                        pltpu.VMEM((d, tm), BF16)],
        compiler_params=pltpu.CompilerParams(
            dimension_semantics=("parallel", "arbitrary"), vmem_limit_bytes=VMEM_LIMIT),
        name="layer_b",
    )(sinks, o, sg, x, wo, gkv, gb, wkT, wvT, gkn, wqT, wgT, gqn, pos, invf, woT)


def _gain_col(g, n_heads, scale=1.0):
    return (jnp.tile(g.astype(F32), n_heads) * scale).reshape(n_heads * HEAD_DIM, 1)


def kernel(x, positions, norm_a_g, w_in_a, b_forget, qnorm_a_g, knorm_a_g, w_out_a, kv_norm_g, w_kv,
           knorm_b_g, norm_b_g, w_in_b, qnorm_b_g, sinks, w_out_b):
    b, s, d = x.shape
    wqkT, wvg, wfT, wo_a, wkT, wvT, wqT, wgT, woT_b = _prep_weights(
        w_in_a[0].T, w_out_a, w_kv, w_in_b, w_out_b)
    q, kT, v, sg, cT = _inproj_a(
        x, norm_a_g[0].reshape(1, d), wqkT, wvg, wfT, b_forget[0].reshape(N_HEADS, 1),
        _gain_col(qnorm_a_g[0], N_HEADS, SCALE * LOG2E), _gain_col(knorm_a_g[0], N_HEADS))
    o = _fox_attention(q, kT, v, cT.reshape(b, N_HEADS // 2, 2, s))

    inv_freq = jnp.power(jnp.float32(ROPE_THETA),
                         -jnp.arange(0, ROT_DIM, 2, dtype=F32) / ROT_DIM).reshape(ROT_HALF, 1)
    return _layer_b(
        sinks[0], o, sg, x, wo_a, kv_norm_g.reshape(d, 1), norm_b_g[0].reshape(d, 1),
        wkT, wvT, _gain_col(knorm_b_g, N_KV_HEADS),
        wqT, wgT, _gain_col(qnorm_b_g[0], N_HEADS, SCALE * LOG2E),
        positions.reshape(1, s), inv_freq, woT_b)
```

```python
import jax
import jax.numpy as jnp
from jax import lax
from jax.experimental import pallas as pl
from jax.experimental.pallas import tpu as pltpu

D_MODEL = 1024
HEAD_DIM = 64
N_HEADS = 16
N_KV_HEADS = 4
GROUP = N_HEADS // N_KV_HEADS
KV_WIDTH = N_KV_HEADS * HEAD_DIM
WINDOW = 128
ROT_DIM = HEAD_DIM // 4
ROT_HALF = ROT_DIM // 2
ROPE_THETA = 500000.0
EPS = 1e-6
SCALE = HEAD_DIM ** -0.5
LOG2E = 1.4426950408889634
NEG = -1e30

F32 = jnp.float32
BF16 = jnp.bfloat16

PREP_ROWS = 128
ROW_TILE = 512
SUB_TILE = 256
FOX_Q_TILE = 256
FOX_PAIRS = 2
SWA_LOOKAHEAD = 2
SUM_ROWS = 16
VMEM_LIMIT = 56 * 1024 * 1024


def _dot(a, b):
    return jnp.dot(a, b, preferred_element_type=F32)


def _dot_nt(a, b):
    return lax.dot_general(a, b, (((1,), (1,)), ((), ())), preferred_element_type=F32)


def _head_norm_fm(t, gain_col):
    blocks = []
    for h in range(t.shape[0] // HEAD_DIM):
        blk = t[h * HEAD_DIM:(h + 1) * HEAD_DIM, :]
        ms = jnp.mean(blk * blk, axis=0, keepdims=True)
        blocks.append(blk * lax.rsqrt(ms + EPS))
    return jnp.concatenate(blocks, axis=0) * gain_col


def _rope_fm(t, cos, sin):
    blocks = []
    for h in range(t.shape[0] // HEAD_DIM):
        base = h * HEAD_DIM
        x1 = t[base:base + ROT_HALF, :]
        x2 = t[base + ROT_HALF:base + ROT_DIM, :]
        blocks.append(x1 * cos - x2 * sin)
        blocks.append(x1 * sin + x2 * cos)
        blocks.append(t[base + ROT_DIM:base + HEAD_DIM, :])
    return jnp.concatenate(blocks, axis=0)


def _split3(x):
    hi = x.astype(BF16)
    r1 = x - hi.astype(F32)
    mid = r1.astype(BF16)
    lo = (r1 - mid.astype(F32)).astype(BF16)
    return hi, mid, lo


def _prep_kernel(waT_ref, woa_ref, wkv_ref, wb_ref, wob_ref,
                 wqkT_ref, wvg_ref, wfT_ref, wo_ref, wkvT_ref, wqT_ref, wgT_ref, woT_ref):
    d = D_MODEL
    wqkT_ref[...] = waT_ref[:2 * d, :].astype(BF16)
    wvg_ref[:, :d] = waT_ref[2 * d:3 * d, :].T.astype(BF16)
    wvg_ref[:, d:] = waT_ref[3 * d + N_HEADS:, :].T.astype(BF16)
    wfT_ref[...] = waT_ref[3 * d:3 * d + N_HEADS, :].astype(BF16)
    wo_ref[...] = woa_ref[0].astype(BF16)
    wkvT_ref[...] = wkv_ref[...].T.astype(BF16)
    wb = wb_ref[0]
    wqT_ref[...] = wb[:, :d].T.astype(BF16)
    wgT_ref[...] = wb[:, d:].T.astype(BF16)
    woT_ref[...] = wob_ref[0].T.astype(BF16)


def _prep_weights(w_in_aT, w_out_a, w_kv, w_in_b, w_out_b):
    d = D_MODEL
    r = PREP_ROWS
    rows3 = lambda a: pl.BlockSpec((1, r, a.shape[2]), lambda i: (0, i, 0))
    same = lambda n: pl.BlockSpec((r, n), lambda i: (i, 0))
    trans = lambda m: pl.BlockSpec((m, r), lambda i: (0, i))
    shapes = [((2 * d, d), trans(2 * d)), ((d, 2 * d), same(2 * d)), ((N_HEADS, d), trans(N_HEADS)),
              ((d, d), same(d)), ((2 * KV_WIDTH, d), trans(2 * KV_WIDTH)),
              ((d, d), trans(d)), ((d, d), trans(d)), ((d, d), trans(d))]
    return pl.pallas_call(
        _prep_kernel,
        grid=(d // r,),
        in_specs=[pl.BlockSpec((w_in_aT.shape[0], r), lambda i: (0, i)),
                  rows3(w_out_a), pl.BlockSpec((r, w_kv.shape[1]), lambda i: (i, 0)),
                  rows3(w_in_b), rows3(w_out_b)],
        out_specs=[spec for _, spec in shapes],
        out_shape=[jax.ShapeDtypeStruct(shape, BF16) for shape, _ in shapes],
        compiler_params=pltpu.CompilerParams(
            dimension_semantics=("parallel",), vmem_limit_bytes=VMEM_LIMIT),
        name="prep_weights",
    )(w_in_aT, w_out_a, w_kv, w_in_b, w_out_b)


def _inproj_a_kernel(x_ref, g_ref, wqkT_ref, wvg_ref, wfT_ref, bf_ref, gq_ref, gk_ref,
                     q_ref, kT_ref, v_ref, sg_ref, cT_ref, carry_ref):
    tm = x_ref.shape[1]

    @pl.when(pl.program_id(1) == 0)
    def _():
        carry_ref[...] = jnp.zeros_like(carry_ref)

    u_parts = []
    for rows in [slice(j * SUB_TILE, (j + 1) * SUB_TILE) for j in range(tm // SUB_TILE)]:
        x = x_ref[0, rows, :]
        ms = jnp.mean(x * x, axis=-1, keepdims=True)
        u32 = x * lax.rsqrt(ms + EPS) * g_ref[...]
        u_parts.append(u32.astype(BF16))
        uT = u32.T.astype(BF16)
        qT = _head_norm_fm(_dot(wqkT_ref[:D_MODEL, :], uT), gq_ref[...])
        kT = _head_norm_fm(_dot(wqkT_ref[D_MODEL:, :], uT), gk_ref[...])
        q_ref[0, rows, :] = qT.T.astype(BF16)
        kT_ref[0, :, rows] = kT.astype(BF16)
    u = jnp.concatenate(u_parts, axis=0)

    f = _dot_nt(wfT_ref[...], u) + bf_ref[...]
    gate = _dot(u, wvg_ref[:, D_MODEL:])
    sg_ref[0] = (gate * jax.nn.sigmoid(gate)).astype(BF16)

    log_f = jnp.minimum(f, 0.0) - jnp.log1p(jnp.exp(-jnp.abs(f)))
    row = lax.broadcasted_iota(jnp.int32, (tm, tm), 0)
    col = lax.broadcasted_iota(jnp.int32, (tm, tm), 1)
    tri = (row <= col).astype(BF16)
    hi, mid, lo = _split3(log_f)
    c = (_dot(hi, tri) + _dot(mid, tri) + _dot(lo, tri)) + carry_ref[:, 0:1]
    v_ref[0] = _dot(u, wvg_ref[:, :D_MODEL]).astype(BF16)
    cT_ref[0] = c * LOG2E
    carry_ref[...] = jnp.broadcast_to(c[:, tm - 1:tm], carry_ref.shape)


def _inproj_a(x, g, wqkT, wvg, wfT, bf, gq, gk):
    b, s, d = x.shape
    tm = ROW_TILE
    full = lambda shape: pl.BlockSpec(shape, lambda bi, i: (0,) * len(shape))
    tok = pl.BlockSpec((1, tm, d), lambda bi, i: (bi, i, 0))
    return pl.pallas_call(
        _inproj_a_kernel,
        grid=(b, s // tm),
        in_specs=[tok, full(g.shape), full(wqkT.shape), full(wvg.shape), full(wfT.shape),
                  full(bf.shape), full(gq.shape), full(gk.shape)],
        out_specs=[tok,
                   pl.BlockSpec((1, d, tm), lambda bi, i: (bi, 0, i)),
                   tok, tok,
                   pl.BlockSpec((1, N_HEADS, tm), lambda bi, i: (bi, 0, i))],
        out_shape=[jax.ShapeDtypeStruct((b, s, d), BF16),
                   jax.ShapeDtypeStruct((b, d, s), BF16),
                   jax.ShapeDtypeStruct((b, s, d), BF16),
                   jax.ShapeDtypeStruct((b, s, d), BF16),
                   jax.ShapeDtypeStruct((b, N_HEADS, s), F32)],
        scratch_shapes=[pltpu.VMEM((N_HEADS, 128), F32)],
        compiler_params=pltpu.CompilerParams(
            dimension_semantics=("parallel", "arbitrary"), vmem_limit_bytes=VMEM_LIMIT),
        name="inproj_a",
    )(x, g, wqkT, wvg, wfT, bf, gq, gk)


BIAS_ROWS = 16


def _fox_kernel(q_ref, kT_ref, v_ref, c_ref, o_ref, bias_ref, vext_ref):
    s_len = q_ref.shape[1]
    n_pairs = q_ref.shape[2] // 128
    tq = FOX_Q_TILE
    lane = lax.broadcasted_iota(jnp.int32, (tq, 128), 1)
    first = lane < HEAD_DIM
    ones_a = jnp.where(lane < 3, 1.0, 0.0).astype(BF16)
    ones_b = jnp.where(jnp.logical_and(lane >= 3, lane < 6), 1.0, 0.0).astype(BF16)
    row = lax.broadcasted_iota(jnp.int32, (2 * tq, tq), 0) % tq
    col = lax.broadcasted_iota(jnp.int32, (2 * tq, tq), 1)
    causal = col <= row

    brow = lax.broadcasted_iota(jnp.int32, (BIAS_ROWS, s_len), 0)
    for p in range(n_pairs):
        parts = _split3(-c_ref[0, p, 0:1, :]) + _split3(-c_ref[0, p, 1:2, :])
        bias = jnp.zeros((BIAS_ROWS, s_len), F32)
        for r, part in enumerate(parts):
            bias = jnp.where(brow == r, part.astype(F32), bias)
        bias_ref[p] = bias.astype(BF16)
        vext_ref[p, :, :128] = v_ref[0, :, p * 128:(p + 1) * 128]
        vext_ref[p, :, 128:] = jnp.ones((s_len, 128), BF16)

    def rhs(p, c0, c1):
        pad = jnp.zeros((128 - BIAS_ROWS, c1 - c0), BF16)
        return jnp.concatenate(
            [kT_ref[0, p * 128:(p + 1) * 128, c0:c1], bias_ref[p, :, c0:c1], pad], axis=0)

    def logits(item):
        p, qb = item
        r0 = qb * tq
        q2 = q_ref[0, r0:r0 + tq, p * 128:(p + 1) * 128]
        zero = jnp.zeros_like(q2)
        lhs = jnp.concatenate(
            [jnp.concatenate([jnp.where(first, q2, zero), ones_a], axis=1),
             jnp.concatenate([jnp.where(first, zero, q2), ones_b], axis=1)], axis=0)
        s_diag = jnp.where(causal, _dot(lhs, rhs(p, r0, r0 + tq)), NEG)
        s_off = _dot(lhs, rhs(p, 0, r0)) if qb > 0 else None
        return s_diag, s_off

    order = [(p, qb) for qb in reversed(range(s_len // tq)) for p in range(n_pairs)]
    pending = logits(order[0])
    for idx, (p, qb) in enumerate(order):
        r0 = qb * tq
        s_diag, s_off = pending
        if idx + 1 < len(order):
            pending = logits(order[idx + 1])
        m = jnp.max(s_diag, axis=1, keepdims=True)
        if qb > 0:
            m = jnp.maximum(m, jnp.max(s_off, axis=1, keepdims=True))
        pv = _dot(jnp.exp2(s_diag - m).astype(BF16), vext_ref[p, r0:r0 + tq, :])
        if qb > 0:
            pv = pv + _dot(jnp.exp2(s_off - m).astype(BF16), vext_ref[p, :r0, :])
        o = pv[:, :128] / pv[:, 128:]
        o_ref[0, r0:r0 + tq, p * 128:(p + 1) * 128] = jnp.where(first, o[:tq], o[tq:]).astype(BF16)


def _fox_attention(q, kT, v, c4):
    b, s, d = q.shape
    np_, wd = FOX_PAIRS, FOX_PAIRS * 128
    return pl.pallas_call(
        _fox_kernel,
        grid=(b, d // wd),
        in_specs=[pl.BlockSpec((1, s, wd), lambda bi, j: (bi, 0, j)),
                  pl.BlockSpec((1, wd, s), lambda bi, j: (bi, j, 0)),
                  pl.BlockSpec((1, s, wd), lambda bi, j: (bi, 0, j)),
                  pl.BlockSpec((1, np_, 2, s), lambda bi, j: (bi, j, 0, 0))],
        out_specs=pl.BlockSpec((1, s, wd), lambda bi, j: (bi, 0, j)),
        out_shape=jax.ShapeDtypeStruct((b, s, d), BF16),
        scratch_shapes=[pltpu.VMEM((np_, BIAS_ROWS, s), BF16), pltpu.VMEM((np_, s, 256), BF16)],
        compiler_params=pltpu.CompilerParams(
            dimension_semantics=("parallel", "parallel"), vmem_limit_bytes=VMEM_LIMIT),
        name="fox_attention",
    )(q, kT, v, c4)


def _layer_b_kernel(sinks_ref, o_ref, sg_ref, x_ref, wo_ref, gkv_ref, gb_ref, wkvT_ref,
                    gkn_ref, wqT_ref, wgT_ref, gqn_ref, pos_ref, invf_ref, woT_ref,
                    out_ref, kext_ref, vext_ref, qT_ref, sgT_ref, oT_ref):
    tm = x_ref.shape[1]
    w = WINDOW
    nq = GROUP * w
    first_tile = pl.program_id(1) == 0

    lane_head = lax.shift_right_logical(
        lax.broadcasted_iota(jnp.int32, (2 * w, KV_WIDTH), 1), HEAD_DIM.bit_length() - 1)
    q_head = lax.shift_right_logical(lax.broadcasted_iota(jnp.int32, (1, nq), 1), w.bit_length() - 1)
    sinks = []
    for g in range(N_KV_HEADS):
        sink = jnp.zeros((1, nq), F32)
        for i in range(GROUP):
            sink = jnp.where(q_head == i, sinks_ref[g * GROUP + i] * LOG2E, sink)
        sinks.append(sink)
    key = lax.broadcasted_iota(jnp.int32, (w, nq), 0)
    qry = jnp.bitwise_and(lax.broadcasted_iota(jnp.int32, (w, nq), 1), w - 1)
    own = key <= qry
    ones_rows = jnp.ones((SUM_ROWS, 2 * w), BF16)

    parts = [slice(j * SUB_TILE, (j + 1) * SUB_TILE) for j in range(tm // SUB_TILE)]
    for rows in parts:
        og = (o_ref[0, rows, :].astype(F32) * sg_ref[0, rows, :].astype(F32)).astype(BF16)
        out_ref[0, rows, :] = x_ref[0, rows, :] + _dot(og, wo_ref[...])

    @pl.when(first_tile)
    def _():
        kext_ref[0:w, :] = jnp.zeros((w, KV_WIDTH), BF16)
        vext_ref[:, 0:w] = jnp.zeros((KV_WIDTH, w), BF16)

    @pl.when(jnp.logical_not(first_tile))
    def _():
        kext_ref[0:w, :] = kext_ref[tm:tm + w, :]
        vext_ref[:, 0:w] = vext_ref[:, tm:tm + w]

    def projection_stages(cols):
        st = {}
        ext = slice(w + cols.start, w + cols.stop)

        def k_stage():
            h = out_ref[0, cols, :]
            ms = jnp.mean(h * h, axis=-1, keepdims=True)
            hn = h * lax.rsqrt(ms + EPS)
            st["u_kv"] = (hn * gkv_ref[...]).astype(BF16)
            st["u_b"] = (hn * gb_ref[...]).astype(BF16)
            ang = invf_ref[...] * pos_ref[:, cols].astype(F32)
            st["cos"] = jnp.cos(ang)
            st["sin"] = jnp.sin(ang)
            kvT = _dot_nt(wkvT_ref[...], st["u_kv"])
            kT = _head_norm_fm(kvT[:KV_WIDTH], gkn_ref[...])
            kext_ref[ext, :] = _rope_fm(kT, st["cos"], st["sin"]).T.astype(BF16)
            vext_ref[:, ext] = kvT[KV_WIDTH:].astype(BF16)

        def q_stage():
            qT = _head_norm_fm(_dot_nt(wqT_ref[...], st["u_b"]), gqn_ref[...])
            qT_ref[:, cols] = _rope_fm(qT, st["cos"], st["sin"]).astype(BF16)

        def gate_stage():
            gateT = _dot_nt(wgT_ref[...], st["u_b"])
            sgT_ref[:, cols] = (gateT * jax.nn.sigmoid(gateT)).astype(BF16)

        return [k_stage, q_stage, gate_stage]

    def logits(item):
        g, n = item
        k2 = kext_ref[n * w:(n + 2) * w, :]
        k_band = jnp.where(lane_head == g, k2, jnp.zeros_like(k2))
        q_rep = jnp.concatenate(
            [jnp.concatenate(
                [qT_ref[(g * GROUP + i) * HEAD_DIM:(g * GROUP + i + 1) * HEAD_DIM, n * w:(n + 1) * w]]
                * N_KV_HEADS, axis=0) for i in range(GROUP)], axis=1)
        s_all = _dot(k_band, q_rep)
        s_prev = s_all[:w]
        if n == 0:
            s_prev = jnp.where(first_tile, NEG, s_prev)
        return jnp.where(own, s_all[w:], s_prev)

    items = [(g, n) for n in range(tm // w) for g in range(N_KV_HEADS)]
    items_per_part = (SUB_TILE // w) * N_KV_HEADS
    pending = {}

    def attend(idx):
        part_end = (idx // items_per_part + 1) * items_per_part
        for ahead in range(idx, min(idx + SWA_LOOKAHEAD + 1, part_end)):
            if ahead not in pending:
                pending[ahead] = logits(items[ahead])
        g, n = items[idx]
        sT = pending.pop(idx)
        m = jnp.maximum(jnp.max(sT, axis=0, keepdims=True), sinks[g])
        p = jnp.exp2(sT - m).astype(BF16)
        zero = jnp.zeros_like(p)
        p2 = jnp.concatenate([jnp.where(own, zero, p), jnp.where(own, p, zero)], axis=0)
        v_ext = jnp.concatenate(
            [vext_ref[g * HEAD_DIM:(g + 1) * HEAD_DIM, n * w:(n + 2) * w], ones_rows], axis=0)
        out_ext = _dot(v_ext, p2)
        denom = out_ext[HEAD_DIM:HEAD_DIM + 1] + jnp.exp2(sinks[g] - m)
        outT = out_ext[:HEAD_DIM] * (1.0 / denom)
        for i in range(GROUP):
            r0 = (g * GROUP + i) * HEAD_DIM
            oT_ref[r0:r0 + HEAD_DIM, n * w:(n + 1) * w] = outT[:, i * w:(i + 1) * w].astype(BF16)

    def output_projection(cols):
        ogT = (oT_ref[:, cols].astype(F32) * sgT_ref[:, cols].astype(F32)).astype(BF16)
        half = D_MODEL // 2
        outT = jnp.concatenate(
            [_dot(woT_ref[:half, :], ogT), _dot(woT_ref[half:, :], ogT)], axis=0)
        out_ref[0, cols, :] = out_ref[0, cols, :] + outT.T

    stages = [projection_stages(cols) for cols in parts]
    for stage in stages[0]:
        stage()
    for j, cols in enumerate(parts):
        upcoming = stages[j + 1] if j + 1 < len(parts) else []
        for k in range(items_per_part):
            if k % 2 == 0 and k // 2 < len(upcoming):
                upcoming[k // 2]()
            attend(j * items_per_part + k)
        output_projection(cols)


def _layer_b(sinks, o, sg, x, wo, gkv, gb, wkvT, gkn, wqT, wgT, gqn, pos, invf, woT):
    b, s, d = x.shape
    tm = ROW_TILE
    full = lambda a: pl.BlockSpec(a.shape, lambda bi, i: (0,) * a.ndim)
    tok = pl.BlockSpec((1, tm, d), lambda bi, i: (bi, i, 0))
    return pl.pallas_call(
        _layer_b_kernel,
        grid=(b, s // tm),
        in_specs=[pl.BlockSpec(memory_space=pltpu.SMEM),
                  tok, tok, tok, full(wo), full(gkv), full(gb), full(wkvT), full(gkn),
                  full(wqT), full(wgT), full(gqn),
                  pl.BlockSpec((1, tm), lambda bi, i: (0, i)), full(invf), full(woT)],
        out_specs=tok,
        out_shape=jax.ShapeDtypeStruct((b, s, d), F32),
        scratch_shapes=[pltpu.VMEM((tm + WINDOW, KV_WIDTH), BF16),
                        pltpu.VMEM((KV_WIDTH, tm + WINDOW), BF16),
                        pltpu.VMEM((d, tm), BF16),
                        pltpu.VMEM((d, tm), BF16),
                        pltpu.VMEM((d, tm), BF16)],
        compiler_params=pltpu.CompilerParams(
            dimension_semantics=("parallel", "arbitrary"), vmem_limit_bytes=VMEM_LIMIT),
        name="layer_b",
    )(sinks, o, sg, x, wo, gkv, gb, wkvT, gkn, wqT, wgT, gqn, pos, invf, woT)


def _gain_col(g, n_heads, scale=1.0):
    return (jnp.tile(g.astype(F32), n_heads) * scale).reshape(n_heads * HEAD_DIM, 1)


def kernel(x, positions, norm_a_g, w_in_a, b_forget, qnorm_a_g, knorm_a_g, w_out_a, kv_norm_g, w_kv,
           knorm_b_g, norm_b_g, w_in_b, qnorm_b_g, sinks, w_out_b):
    b, s, d = x.shape
    wqkT, wvg, wfT, wo_a, wkvT, wqT, wgT, woT_b = _prep_weights(
        w_in_a[0].T, w_out_a, w_kv, w_in_b, w_out_b)
    q, kT, v, sg, cT = _inproj_a(
        x, norm_a_g[0].reshape(1, d), wqkT, wvg, wfT, b_forget[0].reshape(N_HEADS, 1),
        _gain_col(qnorm_a_g[0], N_HEADS, SCALE * LOG2E), _gain_col(knorm_a_g[0], N_HEADS))
    o = _fox_attention(q, kT, v, cT.reshape(b, N_HEADS // 2, 2, s))

    inv_freq = jnp.power(jnp.float32(ROPE_THETA),
                         -jnp.arange(0, ROT_DIM, 2, dtype=F32) / ROT_DIM).reshape(ROT_HALF, 1)
    return _layer_b(
        sinks[0], o, sg, x, wo_a, kv_norm_g.reshape(1, d), norm_b_g[0].reshape(1, d),
        wkvT, _gain_col(knorm_b_g, N_KV_HEADS),
        wqT, wgT, _gain_col(qnorm_b_g[0], N_HEADS, SCALE * LOG2E),
        positions.reshape(1, s), inv_freq, woT_b)
```

```python
import jax
import jax.numpy as jnp
from jax import lax
from jax.experimental import pallas as pl
from jax.experimental.pallas import tpu as pltpu

D_MODEL = 1024
HEAD_DIM = 64
N_HEADS = 16
N_KV_HEADS = 4
GROUP = N_HEADS // N_KV_HEADS
KV_WIDTH = N_KV_HEADS * HEAD_DIM
WINDOW = 128
ROT_DIM = HEAD_DIM // 4
ROT_HALF = ROT_DIM // 2
ROPE_THETA = 500000.0
EPS = 1e-6
SCALE = HEAD_DIM ** -0.5
LOG2E = 1.4426950408889634
NEG = -1e30

F32 = jnp.float32
BF16 = jnp.bfloat16

PREP_ROWS = 128
ROW_TILE = 512
SUB_TILE = 256
FOX_Q_TILE = 256
FOX_PAIRS = 2
SWA_LOOKAHEAD = 2
SUM_ROWS = 16
VMEM_LIMIT = 56 * 1024 * 1024


def _dot(a, b):
    return jnp.dot(a, b, preferred_element_type=F32)


def _dot_nt(a, b):
    return lax.dot_general(a, b, (((1,), (1,)), ((), ())), preferred_element_type=F32)


def _head_norm_fm(t, gain_col):
    blocks = []
    for h in range(t.shape[0] // HEAD_DIM):
        blk = t[h * HEAD_DIM:(h + 1) * HEAD_DIM, :]
        ms = jnp.mean(blk * blk, axis=0, keepdims=True)
        blocks.append(blk * lax.rsqrt(ms + EPS))
    return jnp.concatenate(blocks, axis=0) * gain_col


def _rope_fm(t, cos, sin):
    blocks = []
    for h in range(t.shape[0] // HEAD_DIM):
        base = h * HEAD_DIM
        x1 = t[base:base + ROT_HALF, :]
        x2 = t[base + ROT_HALF:base + ROT_DIM, :]
        blocks.append(x1 * cos - x2 * sin)
        blocks.append(x1 * sin + x2 * cos)
        blocks.append(t[base + ROT_DIM:base + HEAD_DIM, :])
    return jnp.concatenate(blocks, axis=0)


def _split3(x):
    hi = x.astype(BF16)
    r1 = x - hi.astype(F32)
    mid = r1.astype(BF16)
    lo = (r1 - mid.astype(F32)).astype(BF16)
    return hi, mid, lo


def _prep_kernel(waT_ref, woa_ref, wkv_ref, wb_ref, wob_ref,
                 wqkfT_ref, wvg_ref, wo_ref, wkvT_ref, wqT_ref, wgT_ref, woT_ref):
    d = D_MODEL
    wqkfT_ref[:2 * d, :] = waT_ref[:2 * d, :].astype(BF16)
    wqkfT_ref[2 * d:, :] = waT_ref[3 * d:3 * d + N_HEADS, :].astype(BF16)
    wvg_ref[:, :d] = waT_ref[2 * d:3 * d, :].T.astype(BF16)
    wvg_ref[:, d:] = waT_ref[3 * d + N_HEADS:, :].T.astype(BF16)
    wo_ref[...] = woa_ref[0].astype(BF16)
    wkvT_ref[...] = wkv_ref[...].T.astype(BF16)
    wb = wb_ref[0]
    wqT_ref[...] = wb[:, :d].T.astype(BF16)
    wgT_ref[...] = wb[:, d:].T.astype(BF16)
    woT_ref[...] = wob_ref[0].T.astype(BF16)


def _prep_weights(w_in_aT, w_out_a, w_kv, w_in_b, w_out_b):
    d = D_MODEL
    r = PREP_ROWS
    rows3 = lambda a: pl.BlockSpec((1, r, a.shape[2]), lambda i: (0, i, 0))
    same = lambda n: pl.BlockSpec((r, n), lambda i: (i, 0))
    trans = lambda m: pl.BlockSpec((m, r), lambda i: (0, i))
    shapes = [((2 * d + N_HEADS, d), trans(2 * d + N_HEADS)), ((d, 2 * d), same(2 * d)),
              ((d, d), same(d)), ((2 * KV_WIDTH, d), trans(2 * KV_WIDTH)),
              ((d, d), trans(d)), ((d, d), trans(d)), ((d, d), trans(d))]
    return pl.pallas_call(
        _prep_kernel,
        grid=(d // r,),
        in_specs=[pl.BlockSpec((w_in_aT.shape[0], r), lambda i: (0, i)),
                  rows3(w_out_a), pl.BlockSpec((r, w_kv.shape[1]), lambda i: (i, 0)),
                  rows3(w_in_b), rows3(w_out_b)],
        out_specs=[spec for _, spec in shapes],
        out_shape=[jax.ShapeDtypeStruct(shape, BF16) for shape, _ in shapes],
        compiler_params=pltpu.CompilerParams(
            dimension_semantics=("parallel",), vmem_limit_bytes=VMEM_LIMIT),
        name="prep_weights",
    )(w_in_aT, w_out_a, w_kv, w_in_b, w_out_b)


def _inproj_a_kernel(x_ref, g_ref, wqkfT_ref, wvg_ref, bf_ref, gq_ref, gk_ref,
                     q_ref, kT_ref, v_ref, sg_ref, cT_ref, carry_ref):
    tm = x_ref.shape[1]

    @pl.when(pl.program_id(1) == 0)
    def _():
        carry_ref[...] = jnp.zeros_like(carry_ref)

    parts = [slice(j * SUB_TILE, (j + 1) * SUB_TILE) for j in range(tm // SUB_TILE)]
    u_parts, log_f_parts = [], []
    for rows in parts:
        x = x_ref[0, rows, :]
        ms = jnp.mean(x * x, axis=-1, keepdims=True)
        u32 = x * lax.rsqrt(ms + EPS) * g_ref[...]
        u_parts.append(u32.astype(BF16))
        uT = u32.T.astype(BF16)
        qT = _head_norm_fm(_dot(wqkfT_ref[:D_MODEL, :], uT), gq_ref[...])
        kfT = _dot(wqkfT_ref[D_MODEL:, :], uT)
        kT = _head_norm_fm(kfT[:D_MODEL], gk_ref[...])
        q_ref[0, rows, :] = qT.T.astype(BF16)
        kT_ref[0, :, rows] = kT.astype(BF16)
        f = kfT[D_MODEL:] + bf_ref[...]
        log_f_parts.append(jnp.minimum(f, 0.0) - jnp.log1p(jnp.exp(-jnp.abs(f))))
    u = jnp.concatenate(u_parts, axis=0)

    gate = _dot(u, wvg_ref[:, D_MODEL:])
    sg_ref[0] = (gate * jax.nn.sigmoid(gate)).astype(BF16)

    row = lax.broadcasted_iota(jnp.int32, (SUB_TILE, SUB_TILE), 0)
    col = lax.broadcasted_iota(jnp.int32, (SUB_TILE, SUB_TILE), 1)
    tri = (row <= col).astype(BF16)
    sums = [_dot(jnp.concatenate(_split3(log_f), axis=0), tri) for log_f in log_f_parts]
    v_ref[0] = _dot(u, wvg_ref[:, :D_MODEL]).astype(BF16)
    carry = carry_ref[:, 0:1]
    for rows, s3 in zip(parts, sums):
        c = (s3[:N_HEADS] + s3[N_HEADS:2 * N_HEADS] + s3[2 * N_HEADS:]) + carry
        cT_ref[0, :, rows] = c * LOG2E
        carry = c[:, SUB_TILE - 1:SUB_TILE]
    carry_ref[...] = jnp.broadcast_to(carry, carry_ref.shape)


def _inproj_a(x, g, wqkfT, wvg, bf, gq, gk):
    b, s, d = x.shape
    tm = ROW_TILE
    full = lambda shape: pl.BlockSpec(shape, lambda bi, i: (0,) * len(shape))
    tok = pl.BlockSpec((1, tm, d), lambda bi, i: (bi, i, 0))
    return pl.pallas_call(
        _inproj_a_kernel,
        grid=(b, s // tm),
        in_specs=[tok, full(g.shape), full(wqkfT.shape), full(wvg.shape),
                  full(bf.shape), full(gq.shape), full(gk.shape)],
        out_specs=[tok,
                   pl.BlockSpec((1, d, tm), lambda bi, i: (bi, 0, i)),
                   tok, tok,
                   pl.BlockSpec((1, N_HEADS, tm), lambda bi, i: (bi, 0, i))],
        out_shape=[jax.ShapeDtypeStruct((b, s, d), BF16),
                   jax.ShapeDtypeStruct((b, d, s), BF16),
                   jax.ShapeDtypeStruct((b, s, d), BF16),
                   jax.ShapeDtypeStruct((b, s, d), BF16),
                   jax.ShapeDtypeStruct((b, N_HEADS, s), F32)],
        scratch_shapes=[pltpu.VMEM((N_HEADS, 128), F32)],
        compiler_params=pltpu.CompilerParams(
            dimension_semantics=("parallel", "arbitrary"), vmem_limit_bytes=VMEM_LIMIT),
        name="inproj_a",
    )(x, g, wqkfT, wvg, bf, gq, gk)


BIAS_ROWS = 16


def _fox_kernel(q_ref, kT_ref, v_ref, c_ref, o_ref, bias_ref, vext_ref):
    s_len = q_ref.shape[1]
    n_pairs = q_ref.shape[2] // 128
    tq = FOX_Q_TILE
    lane = lax.broadcasted_iota(jnp.int32, (tq, 128), 1)
    first = lane < HEAD_DIM
    ones_a = jnp.where(lane < 3, 1.0, 0.0).astype(BF16)
    ones_b = jnp.where(jnp.logical_and(lane >= 3, lane < 6), 1.0, 0.0).astype(BF16)
    row = lax.broadcasted_iota(jnp.int32, (2 * tq, tq), 0) % tq
    col = lax.broadcasted_iota(jnp.int32, (2 * tq, tq), 1)
    causal = col <= row

    brow = lax.broadcasted_iota(jnp.int32, (BIAS_ROWS, s_len), 0)
    for p in range(n_pairs):
        head = 2 * (pl.program_id(1) * n_pairs + p)
        parts = _split3(-c_ref[0, pl.ds(head, 1), :]) + _split3(-c_ref[0, pl.ds(head + 1, 1), :])
        bias = jnp.zeros((BIAS_ROWS, s_len), F32)
        for r, part in enumerate(parts):
            bias = jnp.where(brow == r, part.astype(F32), bias)
        bias_ref[p] = bias.astype(BF16)
        vext_ref[p, :, :128] = v_ref[0, :, p * 128:(p + 1) * 128]
        vext_ref[p, :, 128:] = jnp.ones((s_len, 128), BF16)

    def rhs(p, c0, c1):
        pad = jnp.zeros((128 - BIAS_ROWS, c1 - c0), BF16)
        return jnp.concatenate(
            [kT_ref[0, p * 128:(p + 1) * 128, c0:c1], bias_ref[p, :, c0:c1], pad], axis=0)

    def logits(item):
        p, qb = item
        r0 = qb * tq
        q2 = q_ref[0, r0:r0 + tq, p * 128:(p + 1) * 128]
        zero = jnp.zeros_like(q2)
        lhs = jnp.concatenate(
            [jnp.concatenate([jnp.where(first, q2, zero), ones_a], axis=1),
             jnp.concatenate([jnp.where(first, zero, q2), ones_b], axis=1)], axis=0)
        s_diag = jnp.where(causal, _dot(lhs, rhs(p, r0, r0 + tq)), NEG)
        s_off = _dot(lhs, rhs(p, 0, r0)) if qb > 0 else None
        return s_diag, s_off

    order = [(p, qb) for qb in reversed(range(s_len // tq)) for p in range(n_pairs)]
    pending = logits(order[0])
    for idx, (p, qb) in enumerate(order):
        r0 = qb * tq
        s_diag, s_off = pending
        if idx + 1 < len(order):
            pending = logits(order[idx + 1])
        m = jnp.max(s_diag, axis=1, keepdims=True)
        if qb > 0:
            m = jnp.maximum(m, jnp.max(s_off, axis=1, keepdims=True))
        pv = _dot(jnp.exp2(s_diag - m).astype(BF16), vext_ref[p, r0:r0 + tq, :])
        if qb > 0:
            pv = pv + _dot(jnp.exp2(s_off - m).astype(BF16), vext_ref[p, :r0, :])
        o = pv[:, :128] / pv[:, 128:]
        o_ref[0, r0:r0 + tq, p * 128:(p + 1) * 128] = jnp.where(first, o[:tq], o[tq:]).astype(BF16)


def _fox_attention(q, kT, v, cT):
    b, s, d = q.shape
    np_, wd = FOX_PAIRS, FOX_PAIRS * 128
    return pl.pallas_call(
        _fox_kernel,
        grid=(b, d // wd),
        in_specs=[pl.BlockSpec((1, s, wd), lambda bi, j: (bi, 0, j)),
                  pl.BlockSpec((1, wd, s), lambda bi, j: (bi, j, 0)),
                  pl.BlockSpec((1, s, wd), lambda bi, j: (bi, 0, j)),
                  pl.BlockSpec((1, N_HEADS, s), lambda bi, j: (bi, 0, 0))],
        out_specs=pl.BlockSpec((1, s, wd), lambda bi, j: (bi, 0, j)),
        out_shape=jax.ShapeDtypeStruct((b, s, d), BF16),
        scratch_shapes=[pltpu.VMEM((np_, BIAS_ROWS, s), BF16), pltpu.VMEM((np_, s, 256), BF16)],
        compiler_params=pltpu.CompilerParams(
            dimension_semantics=("parallel", "parallel"), vmem_limit_bytes=VMEM_LIMIT),
        name="fox_attention",
    )(q, kT, v, cT)


def _layer_b_kernel(sinks_ref, o_ref, sg_ref, x_ref, wo_ref, gkv_ref, gb_ref, wkvT_ref,
                    gkn_ref, wqT_ref, wgT_ref, gqn_ref, pos_ref, invf_ref, woT_ref,
                    out_ref, kext_ref, vext_ref, qT_ref, sgT_ref, oT_ref):
    tm = x_ref.shape[1]
    w = WINDOW
    nq = GROUP * w
    first_tile = pl.program_id(1) == 0

    lane_head = lax.shift_right_logical(
        lax.broadcasted_iota(jnp.int32, (2 * w, KV_WIDTH), 1), HEAD_DIM.bit_length() - 1)
    q_head = lax.shift_right_logical(lax.broadcasted_iota(jnp.int32, (1, nq), 1), w.bit_length() - 1)
    sinks = []
    for g in range(N_KV_HEADS):
        sink = jnp.zeros((1, nq), F32)
        for i in range(GROUP):
            sink = jnp.where(q_head == i, sinks_ref[g * GROUP + i] * LOG2E, sink)
        sinks.append(sink)
    key = lax.broadcasted_iota(jnp.int32, (w, nq), 0)
    qry = jnp.bitwise_and(lax.broadcasted_iota(jnp.int32, (w, nq), 1), w - 1)
    own = key <= qry
    ones_rows = jnp.ones((SUM_ROWS, 2 * w), BF16)

    parts = [slice(j * SUB_TILE, (j + 1) * SUB_TILE) for j in range(tm // SUB_TILE)]
    for rows in parts:
        og = (o_ref[0, rows, :].astype(F32) * sg_ref[0, rows, :].astype(F32)).astype(BF16)
        out_ref[0, rows, :] = x_ref[0, rows, :] + _dot(og, wo_ref[...])

    @pl.when(first_tile)
    def _():
        kext_ref[0:w, :] = jnp.zeros((w, KV_WIDTH), BF16)
        vext_ref[:, 0:w] = jnp.zeros((KV_WIDTH, w), BF16)

    @pl.when(jnp.logical_not(first_tile))
    def _():
        kext_ref[0:w, :] = kext_ref[tm:tm + w, :]
        vext_ref[:, 0:w] = vext_ref[:, tm:tm + w]

    def projection_stages(cols):
        st = {}
        ext = slice(w + cols.start, w + cols.stop)

        def k_stage():
            h = out_ref[0, cols, :]
            ms = jnp.mean(h * h, axis=-1, keepdims=True)
            hn = h * lax.rsqrt(ms + EPS)
            st["u_kv"] = (hn * gkv_ref[...]).astype(BF16)
            st["u_b"] = (hn * gb_ref[...]).astype(BF16)
            ang = invf_ref[...] * pos_ref[:, cols].astype(F32)
            st["cos"] = jnp.cos(ang)
            st["sin"] = jnp.sin(ang)
            kvT = _dot_nt(wkvT_ref[...], st["u_kv"])
            kT = _head_norm_fm(kvT[:KV_WIDTH], gkn_ref[...])
            kext_ref[ext, :] = _rope_fm(kT, st["cos"], st["sin"]).T.astype(BF16)
            vext_ref[:, ext] = kvT[KV_WIDTH:].astype(BF16)

        def q_stage():
            qT = _head_norm_fm(_dot_nt(wqT_ref[...], st["u_b"]), gqn_ref[...])
            qT_ref[:, cols] = _rope_fm(qT, st["cos"], st["sin"]).astype(BF16)

        def gate_stage():
            gateT = _dot_nt(wgT_ref[...], st["u_b"])
            sgT_ref[:, cols] = (gateT * jax.nn.sigmoid(gateT)).astype(BF16)

        return [k_stage, q_stage, gate_stage]

    def logits(item):
        g, n = item
        k2 = kext_ref[n * w:(n + 2) * w, :]
        k_band = jnp.where(lane_head == g, k2, jnp.zeros_like(k2))
        q_rep = jnp.concatenate(
            [jnp.concatenate(
                [qT_ref[(g * GROUP + i) * HEAD_DIM:(g * GROUP + i + 1) * HEAD_DIM, n * w:(n + 1) * w]]
                * N_KV_HEADS, axis=0) for i in range(GROUP)], axis=1)
        s_all = _dot(k_band, q_rep)
        s_prev = s_all[:w]
        if n == 0:
            s_prev = jnp.where(first_tile, NEG, s_prev)
        return jnp.where(own, s_all[w:], s_prev)

    items = [(g, n) for n in range(tm // w) for g in range(N_KV_HEADS)]
    items_per_part = (SUB_TILE // w) * N_KV_HEADS
    pending = {}

    def attend(idx):
        part_end = (idx // items_per_part + 1) * items_per_part
        for ahead in range(idx, min(idx + SWA_LOOKAHEAD + 1, part_end)):
            if ahead not in pending:
                pending[ahead] = logits(items[ahead])
        g, n = items[idx]
        sT = pending.pop(idx)
        m = jnp.maximum(jnp.max(sT, axis=0, keepdims=True), sinks[g])
        p = jnp.exp2(sT - m).astype(BF16)
        zero = jnp.zeros_like(p)
        p2 = jnp.concatenate([jnp.where(own, zero, p), jnp.where(own, p, zero)], axis=0)
        v_ext = jnp.concatenate(
            [vext_ref[g * HEAD_DIM:(g + 1) * HEAD_DIM, n * w:(n + 2) * w], ones_rows], axis=0)
        out_ext = _dot(v_ext, p2)
        denom = out_ext[HEAD_DIM:HEAD_DIM + 1] + jnp.exp2(sinks[g] - m)
        outT = out_ext[:HEAD_DIM] * (1.0 / denom)
        for i in range(GROUP):
            r0 = (g * GROUP + i) * HEAD_DIM
            oT_ref[r0:r0 + HEAD_DIM, n * w:(n + 1) * w] = outT[:, i * w:(i + 1) * w].astype(BF16)

    def output_projection(cols):
        ogT = (oT_ref[:, cols].astype(F32) * sgT_ref[:, cols].astype(F32)).astype(BF16)
        half = D_MODEL // 2
        outT = jnp.concatenate(
            [_dot(woT_ref[:half, :], ogT), _dot(woT_ref[half:, :], ogT)], axis=0)
        out_ref[0, cols, :] = out_ref[0, cols, :] + outT.T

    stages = [projection_stages(cols) for cols in parts]
    for stage in stages[0]:
        stage()
    for j, cols in enumerate(parts):
        upcoming = stages[j + 1] if j + 1 < len(parts) else []
        for k in range(items_per_part):
            if k % 2 == 0 and k // 2 < len(upcoming):
                upcoming[k // 2]()
            attend(j * items_per_part + k)
        output_projection(cols)


def _layer_b(sinks, o, sg, x, wo, gkv, gb, wkvT, gkn, wqT, wgT, gqn, pos, invf, woT):
    b, s, d = x.shape
    tm = ROW_TILE
    full = lambda a: pl.BlockSpec(a.shape, lambda bi, i: (0,) * a.ndim)
    tok = pl.BlockSpec((1, tm, d), lambda bi, i: (bi, i, 0))
    return pl.pallas_call(
        _layer_b_kernel,
        grid=(b, s // tm),
        in_specs=[pl.BlockSpec(memory_space=pltpu.SMEM),
                  tok, tok, tok, full(wo), full(gkv), full(gb), full(wkvT), full(gkn),
                  full(wqT), full(wgT), full(gqn),
                  pl.BlockSpec((1, tm), lambda bi, i: (0, i)), full(invf), full(woT)],
        out_specs=tok,
        out_shape=jax.ShapeDtypeStruct((b, s, d), F32),
        scratch_shapes=[pltpu.VMEM((tm + WINDOW, KV_WIDTH), BF16),
                        pltpu.VMEM((KV_WIDTH, tm + WINDOW), BF16),
                        pltpu.VMEM((d, tm), BF16),
                        pltpu.VMEM((d, tm), BF16),
                        pltpu.VMEM((d, tm), BF16)],
        compiler_params=pltpu.CompilerParams(
            dimension_semantics=("parallel", "arbitrary"), vmem_limit_bytes=VMEM_LIMIT),
        name="layer_b",
    )(sinks, o, sg, x, wo, gkv, gb, wkvT, gkn, wqT, wgT, gqn, pos, invf, woT)


def _gain_col(g, n_heads, scale=1.0):
    return (jnp.tile(g.astype(F32), n_heads) * scale).reshape(n_heads * HEAD_DIM, 1)


def kernel(x, positions, norm_a_g, w_in_a, b_forget, qnorm_a_g, knorm_a_g, w_out_a, kv_norm_g, w_kv,
           knorm_b_g, norm_b_g, w_in_b, qnorm_b_g, sinks, w_out_b):
    b, s, d = x.shape
    wqkfT, wvg, wo_a, wkvT, wqT, wgT, woT_b = _prep_weights(
        w_in_a[0].T, w_out_a, w_kv, w_in_b, w_out_b)
    q, kT, v, sg, cT = _inproj_a(
        x, norm_a_g[0].reshape(1, d), wqkfT, wvg, b_forget[0].reshape(N_HEADS, 1),
        _gain_col(qnorm_a_g[0], N_HEADS, SCALE * LOG2E), _gain_col(knorm_a_g[0], N_HEADS))
    o = _fox_attention(q, kT, v, cT)

    inv_freq = jnp.power(jnp.float32(ROPE_THETA),
                         -jnp.arange(0, ROT_DIM, 2, dtype=F32) / ROT_DIM).reshape(ROT_HALF, 1)
    return _layer_b(
        sinks[0], o, sg, x, wo_a, kv_norm_g.reshape(1, d), norm_b_g[0].reshape(1, d),
        wkvT, _gain_col(knorm_b_g, N_KV_HEADS),
        wqT, wgT, _gain_col(qnorm_b_g[0], N_HEADS, SCALE * LOG2E),
        positions.reshape(1, s), inv_freq, woT_b)
```

```python
import jax
import jax.numpy as jnp
from jax import lax
from jax.experimental import pallas as pl
from jax.experimental.pallas import tpu as pltpu

D_MODEL = 1024
HEAD_DIM = 64
N_HEADS = 16
N_KV_HEADS = 4
GROUP = N_HEADS // N_KV_HEADS
KV_WIDTH = N_KV_HEADS * HEAD_DIM
WINDOW = 128
ROT_DIM = HEAD_DIM // 4
ROT_HALF = ROT_DIM // 2
ROPE_THETA = 500000.0
EPS = 1e-6
SCALE = HEAD_DIM ** -0.5
LOG2E = 1.4426950408889634
NEG = -1e30

F32 = jnp.float32
BF16 = jnp.bfloat16

PREP_ROWS = 128
ROW_TILE = 1024
SUB_TILE = 256
FOX_Q_TILE = 256
FOX_PAIRS = 2
SWA_LOOKAHEAD = 2
SUM_ROWS = 16
VMEM_LIMIT = 56 * 1024 * 1024


def _dot(a, b):
    return jnp.dot(a, b, preferred_element_type=F32)


def _dot_nt(a, b):
    return lax.dot_general(a, b, (((1,), (1,)), ((), ())), preferred_element_type=F32)


def _head_norm_fm(t, gain_col):
    blocks = []
    for h in range(t.shape[0] // HEAD_DIM):
        blk = t[h * HEAD_DIM:(h + 1) * HEAD_DIM, :]
        ms = jnp.mean(blk * blk, axis=0, keepdims=True)
        blocks.append(blk * lax.rsqrt(ms + EPS))
    return jnp.concatenate(blocks, axis=0) * gain_col


def _rope_fm(t, cos, sin):
    blocks = []
    for h in range(t.shape[0] // HEAD_DIM):
        base = h * HEAD_DIM
        x1 = t[base:base + ROT_HALF, :]
        x2 = t[base + ROT_HALF:base + ROT_DIM, :]
        blocks.append(x1 * cos - x2 * sin)
        blocks.append(x1 * sin + x2 * cos)
        blocks.append(t[base + ROT_DIM:base + HEAD_DIM, :])
    return jnp.concatenate(blocks, axis=0)


def _split3(x):
    hi = x.astype(BF16)
    r1 = x - hi.astype(F32)
    mid = r1.astype(BF16)
    lo = (r1 - mid.astype(F32)).astype(BF16)
    return hi, mid, lo


def _prep_kernel(waT_ref, woa_ref, wkv_ref, wb_ref, wob_ref,
                 wqkfT_ref, wvg_ref, wo_ref, wkvT_ref, wqT_ref, wgT_ref, woT_ref):
    d = D_MODEL
    wqkfT_ref[:2 * d, :] = waT_ref[:2 * d, :].astype(BF16)
    wqkfT_ref[2 * d:, :] = waT_ref[3 * d:3 * d + N_HEADS, :].astype(BF16)
    wvg_ref[:, :d] = waT_ref[2 * d:3 * d, :].T.astype(BF16)
    wvg_ref[:, d:] = waT_ref[3 * d + N_HEADS:, :].T.astype(BF16)
    wo_ref[...] = woa_ref[0].astype(BF16)
    wkvT_ref[...] = wkv_ref[...].T.astype(BF16)
    wb = wb_ref[0]
    wqT_ref[...] = wb[:, :d].T.astype(BF16)
    wgT_ref[...] = wb[:, d:].T.astype(BF16)
    woT_ref[...] = wob_ref[0].T.astype(BF16)


def _prep_weights(w_in_aT, w_out_a, w_kv, w_in_b, w_out_b):
    d = D_MODEL
    r = PREP_ROWS
    rows3 = lambda a: pl.BlockSpec((1, r, a.shape[2]), lambda i: (0, i, 0))
    same = lambda n: pl.BlockSpec((r, n), lambda i: (i, 0))
    trans = lambda m: pl.BlockSpec((m, r), lambda i: (0, i))
    shapes = [((2 * d + N_HEADS, d), trans(2 * d + N_HEADS)), ((d, 2 * d), same(2 * d)),
              ((d, d), same(d)), ((2 * KV_WIDTH, d), trans(2 * KV_WIDTH)),
              ((d, d), trans(d)), ((d, d), trans(d)), ((d, d), trans(d))]
    return pl.pallas_call(
        _prep_kernel,
        grid=(d // r,),
        in_specs=[pl.BlockSpec((w_in_aT.shape[0], r), lambda i: (0, i)),
                  rows3(w_out_a), pl.BlockSpec((r, w_kv.shape[1]), lambda i: (i, 0)),
                  rows3(w_in_b), rows3(w_out_b)],
        out_specs=[spec for _, spec in shapes],
        out_shape=[jax.ShapeDtypeStruct(shape, BF16) for shape, _ in shapes],
        compiler_params=pltpu.CompilerParams(
            dimension_semantics=("parallel",), vmem_limit_bytes=VMEM_LIMIT),
        name="prep_weights",
    )(w_in_aT, w_out_a, w_kv, w_in_b, w_out_b)


def _inproj_a_kernel(x_ref, g_ref, wqkfT_ref, wvg_ref, bf_ref, gq_ref, gk_ref,
                     q_ref, kT_ref, v_ref, sg_ref, cT_ref, carry_ref):
    tm = x_ref.shape[1]

    @pl.when(pl.program_id(1) == 0)
    def _():
        carry_ref[...] = jnp.zeros_like(carry_ref)

    parts = [slice(j * SUB_TILE, (j + 1) * SUB_TILE) for j in range(tm // SUB_TILE)]
    u_parts, log_f_parts = [], []
    for rows in parts:
        x = x_ref[0, rows, :]
        ms = jnp.mean(x * x, axis=-1, keepdims=True)
        u32 = x * lax.rsqrt(ms + EPS) * g_ref[...]
        u_parts.append(u32.astype(BF16))
        uT = u32.T.astype(BF16)
        qT = _head_norm_fm(_dot(wqkfT_ref[:D_MODEL, :], uT), gq_ref[...])
        kfT = _dot(wqkfT_ref[D_MODEL:, :], uT)
        kT = _head_norm_fm(kfT[:D_MODEL], gk_ref[...])
        q_ref[0, rows, :] = qT.T.astype(BF16)
        kT_ref[0, :, rows] = kT.astype(BF16)
        f = kfT[D_MODEL:] + bf_ref[...]
        log_f_parts.append(jnp.minimum(f, 0.0) - jnp.log1p(jnp.exp(-jnp.abs(f))))
    u = jnp.concatenate(u_parts, axis=0)

    gate = _dot(u, wvg_ref[:, D_MODEL:])
    sg_ref[0] = (gate * jax.nn.sigmoid(gate)).astype(BF16)

    row = lax.broadcasted_iota(jnp.int32, (SUB_TILE, SUB_TILE), 0)
    col = lax.broadcasted_iota(jnp.int32, (SUB_TILE, SUB_TILE), 1)
    tri = (row <= col).astype(BF16)
    sums = [_dot(jnp.concatenate(_split3(log_f), axis=0), tri) for log_f in log_f_parts]
    v_ref[0] = _dot(u, wvg_ref[:, :D_MODEL]).astype(BF16)
    carry = carry_ref[:, 0:1]
    for rows, s3 in zip(parts, sums):
        c = (s3[:N_HEADS] + s3[N_HEADS:2 * N_HEADS] + s3[2 * N_HEADS:]) + carry
        cT_ref[0, :, rows] = c * LOG2E
        carry = c[:, SUB_TILE - 1:SUB_TILE]
    carry_ref[...] = jnp.broadcast_to(carry, carry_ref.shape)


def _inproj_a(x, g, wqkfT, wvg, bf, gq, gk):
    b, s, d = x.shape
    tm = ROW_TILE
    full = lambda shape: pl.BlockSpec(shape, lambda bi, i: (0,) * len(shape))
    tok = pl.BlockSpec((1, tm, d), lambda bi, i: (bi, i, 0))
    return pl.pallas_call(
        _inproj_a_kernel,
        grid=(b, s // tm),
        in_specs=[tok, full(g.shape), full(wqkfT.shape), full(wvg.shape),
                  full(bf.shape), full(gq.shape), full(gk.shape)],
        out_specs=[tok,
                   pl.BlockSpec((1, d, tm), lambda bi, i: (bi, 0, i)),
                   tok, tok,
                   pl.BlockSpec((1, N_HEADS, tm), lambda bi, i: (bi, 0, i))],
        out_shape=[jax.ShapeDtypeStruct((b, s, d), BF16),
                   jax.ShapeDtypeStruct((b, d, s), BF16),
                   jax.ShapeDtypeStruct((b, s, d), BF16),
                   jax.ShapeDtypeStruct((b, s, d), BF16),
                   jax.ShapeDtypeStruct((b, N_HEADS, s), F32)],
        scratch_shapes=[pltpu.VMEM((N_HEADS, 128), F32)],
        compiler_params=pltpu.CompilerParams(
            dimension_semantics=("parallel", "arbitrary"), vmem_limit_bytes=VMEM_LIMIT),
        name="inproj_a",
    )(x, g, wqkfT, wvg, bf, gq, gk)


BIAS_ROWS = 16


def _fox_kernel(q_ref, kT_ref, v_ref, c_ref, o_ref, bias_ref, vext_ref):
    s_len = q_ref.shape[1]
    n_pairs = q_ref.shape[2] // 128
    tq = FOX_Q_TILE
    lane = lax.broadcasted_iota(jnp.int32, (tq, 128), 1)
    first = lane < HEAD_DIM
    ones_a = jnp.where(lane < 3, 1.0, 0.0).astype(BF16)
    ones_b = jnp.where(jnp.logical_and(lane >= 3, lane < 6), 1.0, 0.0).astype(BF16)
    row = lax.broadcasted_iota(jnp.int32, (2 * tq, tq), 0) % tq
    col = lax.broadcasted_iota(jnp.int32, (2 * tq, tq), 1)
    causal = col <= row

    brow = lax.broadcasted_iota(jnp.int32, (BIAS_ROWS, s_len), 0)
    for p in range(n_pairs):
        head = 2 * (pl.program_id(1) * n_pairs + p)
        parts = _split3(-c_ref[0, pl.ds(head, 1), :]) + _split3(-c_ref[0, pl.ds(head + 1, 1), :])
        bias = jnp.zeros((BIAS_ROWS, s_len), F32)
        for r, part in enumerate(parts):
            bias = jnp.where(brow == r, part.astype(F32), bias)
        bias_ref[p] = bias.astype(BF16)
        vext_ref[p, :, :128] = v_ref[0, :, p * 128:(p + 1) * 128]
        vext_ref[p, :, 128:] = jnp.ones((s_len, 128), BF16)

    def rhs(p, c0, c1):
        pad = jnp.zeros((128 - BIAS_ROWS, c1 - c0), BF16)
        return jnp.concatenate(
            [kT_ref[0, p * 128:(p + 1) * 128, c0:c1], bias_ref[p, :, c0:c1], pad], axis=0)

    def logits(item):
        p, qb = item
        r0 = qb * tq
        q2 = q_ref[0, r0:r0 + tq, p * 128:(p + 1) * 128]
        zero = jnp.zeros_like(q2)
        lhs = jnp.concatenate(
            [jnp.concatenate([jnp.where(first, q2, zero), ones_a], axis=1),
             jnp.concatenate([jnp.where(first, zero, q2), ones_b], axis=1)], axis=0)
        s_diag = jnp.where(causal, _dot(lhs, rhs(p, r0, r0 + tq)), NEG)
        s_off = _dot(lhs, rhs(p, 0, r0)) if qb > 0 else None
        return s_diag, s_off

    order = [(p, qb) for qb in reversed(range(s_len // tq)) for p in range(n_pairs)]
    pending = logits(order[0])
    for idx, (p, qb) in enumerate(order):
        r0 = qb * tq
        s_diag, s_off = pending
        if idx + 1 < len(order):
            pending = logits(order[idx + 1])
        m = jnp.max(s_diag, axis=1, keepdims=True)
        if qb > 0:
            m = jnp.maximum(m, jnp.max(s_off, axis=1, keepdims=True))
        pv = _dot(jnp.exp2(s_diag - m).astype(BF16), vext_ref[p, r0:r0 + tq, :])
        if qb > 0:
            pv = pv + _dot(jnp.exp2(s_off - m).astype(BF16), vext_ref[p, :r0, :])
        o = pv[:, :128] / pv[:, 128:]
        o_ref[0, r0:r0 + tq, p * 128:(p + 1) * 128] = jnp.where(first, o[:tq], o[tq:]).astype(BF16)


def _fox_attention(q, kT, v, cT):
    b, s, d = q.shape
    np_, wd = FOX_PAIRS, FOX_PAIRS * 128
    return pl.pallas_call(
        _fox_kernel,
        grid=(b, d // wd),
        in_specs=[pl.BlockSpec((1, s, wd), lambda bi, j: (bi, 0, j)),
                  pl.BlockSpec((1, wd, s), lambda bi, j: (bi, j, 0)),
                  pl.BlockSpec((1, s, wd), lambda bi, j: (bi, 0, j)),
                  pl.BlockSpec((1, N_HEADS, s), lambda bi, j: (bi, 0, 0))],
        out_specs=pl.BlockSpec((1, s, wd), lambda bi, j: (bi, 0, j)),
        out_shape=jax.ShapeDtypeStruct((b, s, d), BF16),
        scratch_shapes=[pltpu.VMEM((np_, BIAS_ROWS, s), BF16), pltpu.VMEM((np_, s, 256), BF16)],
        compiler_params=pltpu.CompilerParams(
            dimension_semantics=("parallel", "parallel"), vmem_limit_bytes=VMEM_LIMIT),
        name="fox_attention",
    )(q, kT, v, cT)


def _layer_b_kernel(sinks_ref, o_ref, sg_ref, x_ref, wo_ref, gkv_ref, gb_ref, wkvT_ref,
                    gkn_ref, wqT_ref, wgT_ref, gqn_ref, pos_ref, invf_ref, woT_ref,
                    out_ref, kext_ref, vext_ref, qT_ref, sgT_ref, oT_ref):
    tm = x_ref.shape[1]
    w = WINDOW
    nq = GROUP * w
    first_tile = pl.program_id(1) == 0

    lane_head = lax.shift_right_logical(
        lax.broadcasted_iota(jnp.int32, (2 * w, KV_WIDTH), 1), HEAD_DIM.bit_length() - 1)
    q_head = lax.shift_right_logical(lax.broadcasted_iota(jnp.int32, (1, nq), 1), w.bit_length() - 1)
    sinks = []
    for g in range(N_KV_HEADS):
        sink = jnp.zeros((1, nq), F32)
        for i in range(GROUP):
            sink = jnp.where(q_head == i, sinks_ref[g * GROUP + i] * LOG2E, sink)
        sinks.append(sink)
    key = lax.broadcasted_iota(jnp.int32, (w, nq), 0)
    qry = jnp.bitwise_and(lax.broadcasted_iota(jnp.int32, (w, nq), 1), w - 1)
    own = key <= qry
    ones_rows = jnp.ones((SUM_ROWS, 2 * w), BF16)

    parts = [slice(j * SUB_TILE, (j + 1) * SUB_TILE) for j in range(tm // SUB_TILE)]
    for rows in parts:
        og = (o_ref[0, rows, :].astype(F32) * sg_ref[0, rows, :].astype(F32)).astype(BF16)
        out_ref[0, rows, :] = x_ref[0, rows, :] + _dot(og, wo_ref[...])

    @pl.when(first_tile)
    def _():
        kext_ref[0:w, :] = jnp.zeros((w, KV_WIDTH), BF16)
        vext_ref[:, 0:w] = jnp.zeros((KV_WIDTH, w), BF16)

    @pl.when(jnp.logical_not(first_tile))
    def _():
        kext_ref[0:w, :] = kext_ref[tm:tm + w, :]
        vext_ref[:, 0:w] = vext_ref[:, tm:tm + w]

    def projection_stages(cols):
        st = {}
        ext = slice(w + cols.start, w + cols.stop)

        def k_stage():
            h = out_ref[0, cols, :]
            ms = jnp.mean(h * h, axis=-1, keepdims=True)
            hn = h * lax.rsqrt(ms + EPS)
            st["u_kv"] = (hn * gkv_ref[...]).astype(BF16)
            st["u_b"] = (hn * gb_ref[...]).astype(BF16)
            ang = invf_ref[...] * pos_ref[:, cols].astype(F32)
            st["cos"] = jnp.cos(ang)
            st["sin"] = jnp.sin(ang)
            kvT = _dot_nt(wkvT_ref[...], st["u_kv"])
            kT = _head_norm_fm(kvT[:KV_WIDTH], gkn_ref[...])
            kext_ref[ext, :] = _rope_fm(kT, st["cos"], st["sin"]).T.astype(BF16)
            vext_ref[:, ext] = kvT[KV_WIDTH:].astype(BF16)

        def q_stage():
            qT = _head_norm_fm(_dot_nt(wqT_ref[...], st["u_b"]), gqn_ref[...])
            qT_ref[:, cols] = _rope_fm(qT, st["cos"], st["sin"]).astype(BF16)

        def gate_stage():
            gateT = _dot_nt(wgT_ref[...], st["u_b"])
            sgT_ref[:, cols] = (gateT * jax.nn.sigmoid(gateT)).astype(BF16)

        return [k_stage, q_stage, gate_stage]

    def logits(item):
        g, n = item
        k2 = kext_ref[n * w:(n + 2) * w, :]
        k_band = jnp.where(lane_head == g, k2, jnp.zeros_like(k2))
        q_rep = jnp.concatenate(
            [jnp.concatenate(
                [qT_ref[(g * GROUP + i) * HEAD_DIM:(g * GROUP + i + 1) * HEAD_DIM, n * w:(n + 1) * w]]
                * N_KV_HEADS, axis=0) for i in range(GROUP)], axis=1)
        s_all = _dot(k_band, q_rep)
        s_prev = s_all[:w]
        if n == 0:
            s_prev = jnp.where(first_tile, NEG, s_prev)
        return jnp.where(own, s_all[w:], s_prev)

    items = [(g, n) for n in range(tm // w) for g in range(N_KV_HEADS)]
    items_per_part = (SUB_TILE // w) * N_KV_HEADS
    pending = {}

    def attend(idx):
        part_end = (idx // items_per_part + 1) * items_per_part
        for ahead in range(idx, min(idx + SWA_LOOKAHEAD + 1, part_end)):
            if ahead not in pending:
                pending[ahead] = logits(items[ahead])
        g, n = items[idx]
        sT = pending.pop(idx)
        m = jnp.maximum(jnp.max(sT, axis=0, keepdims=True), sinks[g])
        p = jnp.exp2(sT - m).astype(BF16)
        zero = jnp.zeros_like(p)
        p2 = jnp.concatenate([jnp.where(own, zero, p), jnp.where(own, p, zero)], axis=0)
        v_ext = jnp.concatenate(
            [vext_ref[g * HEAD_DIM:(g + 1) * HEAD_DIM, n * w:(n + 2) * w], ones_rows], axis=0)
        out_ext = _dot(v_ext, p2)
        denom = out_ext[HEAD_DIM:HEAD_DIM + 1] + jnp.exp2(sinks[g] - m)
        outT = out_ext[:HEAD_DIM] * (1.0 / denom)
        for i in range(GROUP):
            r0 = (g * GROUP + i) * HEAD_DIM
            oT_ref[r0:r0 + HEAD_DIM, n * w:(n + 1) * w] = outT[:, i * w:(i + 1) * w].astype(BF16)

    def output_projection(cols):
        ogT = (oT_ref[:, cols].astype(F32) * sgT_ref[:, cols].astype(F32)).astype(BF16)
        half = D_MODEL // 2
        outT = jnp.concatenate(
            [_dot(woT_ref[:half, :], ogT), _dot(woT_ref[half:, :], ogT)], axis=0)
        out_ref[0, cols, :] = out_ref[0, cols, :] + outT.T

    stages = [projection_stages(cols) for cols in parts]
    for stage in stages[0]:
        stage()
    for j, cols in enumerate(parts):
        upcoming = stages[j + 1] if j + 1 < len(parts) else []
        for k in range(items_per_part):
            if k % 2 == 0 and k // 2 < len(upcoming):
                upcoming[k // 2]()
            attend(j * items_per_part + k)
        output_projection(cols)


def _layer_b(sinks, o, sg, x, wo, gkv, gb, wkvT, gkn, wqT, wgT, gqn, pos, invf, woT):
    b, s, d = x.shape
    tm = ROW_TILE
    full = lambda a: pl.BlockSpec(a.shape, lambda bi, i: (0,) * a.ndim)
    tok = pl.BlockSpec((1, tm, d), lambda bi, i: (bi, i, 0))
    return pl.pallas_call(
        _layer_b_kernel,
        grid=(b, s // tm),
        in_specs=[pl.BlockSpec(memory_space=pltpu.SMEM),
                  tok, tok, tok, full(wo), full(gkv), full(gb), full(wkvT), full(gkn),
                  full(wqT), full(wgT), full(gqn),
                  pl.BlockSpec((1, tm), lambda bi, i: (0, i)), full(invf), full(woT)],
        out_specs=tok,
        out_shape=jax.ShapeDtypeStruct((b, s, d), F32),
        scratch_shapes=[pltpu.VMEM((tm + WINDOW, KV_WIDTH), BF16),
                        pltpu.VMEM((KV_WIDTH, tm + WINDOW), BF16),
                        pltpu.VMEM((d, tm), BF16),
                        pltpu.VMEM((d, tm), BF16),
                        pltpu.VMEM((d, tm), BF16)],
        compiler_params=pltpu.CompilerParams(
            dimension_semantics=("parallel", "arbitrary"), vmem_limit_bytes=VMEM_LIMIT),
        name="layer_b",
    )(sinks, o, sg, x, wo, gkv, gb, wkvT, gkn, wqT, wgT, gqn, pos, invf, woT)


def _gain_col(g, n_heads, scale=1.0):
    return (jnp.tile(g.astype(F32), n_heads) * scale).reshape(n_heads * HEAD_DIM, 1)


def kernel(x, positions, norm_a_g, w_in_a, b_forget, qnorm_a_g, knorm_a_g, w_out_a, kv_norm_g, w_kv,
           knorm_b_g, norm_b_g, w_in_b, qnorm_b_g, sinks, w_out_b):
    b, s, d = x.shape
    wqkfT, wvg, wo_a, wkvT, wqT, wgT, woT_b = _prep_weights(
        w_in_a[0].T, w_out_a, w_kv, w_in_b, w_out_b)
    q, kT, v, sg, cT = _inproj_a(
        x, norm_a_g[0].reshape(1, d), wqkfT, wvg, b_forget[0].reshape(N_HEADS, 1),
        _gain_col(qnorm_a_g[0], N_HEADS, SCALE * LOG2E), _gain_col(knorm_a_g[0], N_HEADS))
    o = _fox_attention(q, kT, v, cT)

    inv_freq = jnp.power(jnp.float32(ROPE_THETA),
                         -jnp.arange(0, ROT_DIM, 2, dtype=F32) / ROT_DIM).reshape(ROT_HALF, 1)
    return _layer_b(
        sinks[0], o, sg, x, wo_a, kv_norm_g.reshape(1, d), norm_b_g[0].reshape(1, d),
        wkvT, _gain_col(knorm_b_g, N_KV_HEADS),
        wqT, wgT, _gain_col(qnorm_b_g[0], N_HEADS, SCALE * LOG2E),
        positions.reshape(1, s), inv_freq, woT_b)
```

```python
import jax
import jax.numpy as jnp
from jax import lax
from jax.experimental import pallas as pl
from jax.experimental.pallas import tpu as pltpu

D_MODEL = 1024
HEAD_DIM = 64
N_HEADS = 16
N_KV_HEADS = 4
GROUP = N_HEADS // N_KV_HEADS
KV_WIDTH = N_KV_HEADS * HEAD_DIM
WINDOW = 128
ROT_DIM = HEAD_DIM // 4
ROT_HALF = ROT_DIM // 2
ROPE_THETA = 500000.0
EPS = 1e-6
SCALE = HEAD_DIM ** -0.5
LOG2E = 1.4426950408889634
NEG = -1e30

F32 = jnp.float32
BF16 = jnp.bfloat16

PREP_ROWS = 128
ROW_TILE = 1024
SUB_TILE = 256
FOX_Q_TILE = 256
FOX_PAIRS = 2
FOX_LOOKAHEAD = 2
SWA_LOOKAHEAD = 2
SUM_ROWS = 16
VMEM_LIMIT = 56 * 1024 * 1024


def _dot(a, b):
    return jnp.dot(a, b, preferred_element_type=F32)


def _dot_nt(a, b):
    return lax.dot_general(a, b, (((1,), (1,)), ((), ())), preferred_element_type=F32)


def _head_norm_fm(t, gain_col):
    blocks = []
    for h in range(t.shape[0] // HEAD_DIM):
        blk = t[h * HEAD_DIM:(h + 1) * HEAD_DIM, :]
        ms = jnp.mean(blk * blk, axis=0, keepdims=True)
        blocks.append(blk * lax.rsqrt(ms + EPS))
    return jnp.concatenate(blocks, axis=0) * gain_col


def _rope_fm(t, cos, sin):
    blocks = []
    for h in range(t.shape[0] // HEAD_DIM):
        base = h * HEAD_DIM
        x1 = t[base:base + ROT_HALF, :]
        x2 = t[base + ROT_HALF:base + ROT_DIM, :]
        blocks.append(x1 * cos - x2 * sin)
        blocks.append(x1 * sin + x2 * cos)
        blocks.append(t[base + ROT_DIM:base + HEAD_DIM, :])
    return jnp.concatenate(blocks, axis=0)


def _split3(x):
    hi = x.astype(BF16)
    r1 = x - hi.astype(F32)
    mid = r1.astype(BF16)
    lo = (r1 - mid.astype(F32)).astype(BF16)
    return hi, mid, lo


def _prep_kernel(waT_ref, woa_ref, wkv_ref, wb_ref, wob_ref,
                 wqkfT_ref, wvg_ref, wo_ref, wkvT_ref, wqT_ref, wgT_ref, woT_ref):
    d = D_MODEL
    wqkfT_ref[:2 * d, :] = waT_ref[:2 * d, :].astype(BF16)
    wqkfT_ref[2 * d:, :] = waT_ref[3 * d:3 * d + N_HEADS, :].astype(BF16)
    wvg_ref[:, :d] = waT_ref[2 * d:3 * d, :].T.astype(BF16)
    wvg_ref[:, d:] = waT_ref[3 * d + N_HEADS:, :].T.astype(BF16)
    wo_ref[...] = woa_ref[0].astype(BF16)
    wkvT_ref[...] = wkv_ref[...].T.astype(BF16)
    wb = wb_ref[0]
    wqT_ref[...] = wb[:, :d].T.astype(BF16)
    wgT_ref[...] = wb[:, d:].T.astype(BF16)
    woT_ref[...] = wob_ref[0].T.astype(BF16)


def _prep_weights(w_in_aT, w_out_a, w_kv, w_in_b, w_out_b):
    d = D_MODEL
    r = PREP_ROWS
    rows3 = lambda a: pl.BlockSpec((1, r, a.shape[2]), lambda i: (0, i, 0))
    same = lambda n: pl.BlockSpec((r, n), lambda i: (i, 0))
    trans = lambda m: pl.BlockSpec((m, r), lambda i: (0, i))
    shapes = [((2 * d + N_HEADS, d), trans(2 * d + N_HEADS)), ((d, 2 * d), same(2 * d)),
              ((d, d), same(d)), ((2 * KV_WIDTH, d), trans(2 * KV_WIDTH)),
              ((d, d), trans(d)), ((d, d), trans(d)), ((d, d), trans(d))]
    return pl.pallas_call(
        _prep_kernel,
        grid=(d // r,),
        in_specs=[pl.BlockSpec((w_in_aT.shape[0], r), lambda i: (0, i)),
                  rows3(w_out_a), pl.BlockSpec((r, w_kv.shape[1]), lambda i: (i, 0)),
                  rows3(w_in_b), rows3(w_out_b)],
        out_specs=[spec for _, spec in shapes],
        out_shape=[jax.ShapeDtypeStruct(shape, BF16) for shape, _ in shapes],
        compiler_params=pltpu.CompilerParams(
            dimension_semantics=("parallel",), vmem_limit_bytes=VMEM_LIMIT),
        name="prep_weights",
    )(w_in_aT, w_out_a, w_kv, w_in_b, w_out_b)


def _inproj_a_kernel(x_ref, g_ref, wqkfT_ref, wvg_ref, bf_ref, gq_ref, gk_ref,
                     q_ref, kT_ref, v_ref, sg_ref, cT_ref, carry_ref):
    tm = x_ref.shape[1]

    @pl.when(pl.program_id(1) == 0)
    def _():
        carry_ref[...] = jnp.zeros_like(carry_ref)

    parts = [slice(j * SUB_TILE, (j + 1) * SUB_TILE) for j in range(tm // SUB_TILE)]
    u_parts, log_f_parts = [], []
    for rows in parts:
        x = x_ref[0, rows, :]
        ms = jnp.mean(x * x, axis=-1, keepdims=True)
        u32 = x * lax.rsqrt(ms + EPS) * g_ref[...]
        u_parts.append(u32.astype(BF16))
        uT = u32.T.astype(BF16)
        qT = _head_norm_fm(_dot(wqkfT_ref[:D_MODEL, :], uT), gq_ref[...])
        kfT = _dot(wqkfT_ref[D_MODEL:, :], uT)
        kT = _head_norm_fm(kfT[:D_MODEL], gk_ref[...])
        q_ref[0, rows, :] = qT.T.astype(BF16)
        kT_ref[0, :, rows] = kT.astype(BF16)
        f = kfT[D_MODEL:] + bf_ref[...]
        log_f_parts.append(jnp.minimum(f, 0.0) - jnp.log1p(jnp.exp(-jnp.abs(f))))
    u = jnp.concatenate(u_parts, axis=0)

    gate = _dot(u, wvg_ref[:, D_MODEL:])
    sg_ref[0] = (gate * jax.nn.sigmoid(gate)).astype(BF16)

    row = lax.broadcasted_iota(jnp.int32, (SUB_TILE, SUB_TILE), 0)
    col = lax.broadcasted_iota(jnp.int32, (SUB_TILE, SUB_TILE), 1)
    tri = (row <= col).astype(BF16)
    sums = [_dot(jnp.concatenate(_split3(log_f), axis=0), tri) for log_f in log_f_parts]
    v_ref[0] = _dot(u, wvg_ref[:, :D_MODEL]).astype(BF16)
    carry = carry_ref[:, 0:1]
    for rows, s3 in zip(parts, sums):
        c = (s3[:N_HEADS] + s3[N_HEADS:2 * N_HEADS] + s3[2 * N_HEADS:]) + carry
        cT_ref[0, :, rows] = c * LOG2E
        carry = c[:, SUB_TILE - 1:SUB_TILE]
    carry_ref[...] = jnp.broadcast_to(carry, carry_ref.shape)


def _inproj_a(x, g, wqkfT, wvg, bf, gq, gk):
    b, s, d = x.shape
    tm = ROW_TILE
    full = lambda shape: pl.BlockSpec(shape, lambda bi, i: (0,) * len(shape))
    tok = pl.BlockSpec((1, tm, d), lambda bi, i: (bi, i, 0))
    return pl.pallas_call(
        _inproj_a_kernel,
        grid=(b, s // tm),
        in_specs=[tok, full(g.shape), full(wqkfT.shape), full(wvg.shape),
                  full(bf.shape), full(gq.shape), full(gk.shape)],
        out_specs=[tok,
                   pl.BlockSpec((1, d, tm), lambda bi, i: (bi, 0, i)),
                   tok, tok,
                   pl.BlockSpec((1, N_HEADS, tm), lambda bi, i: (bi, 0, i))],
        out_shape=[jax.ShapeDtypeStruct((b, s, d), BF16),
                   jax.ShapeDtypeStruct((b, d, s), BF16),
                   jax.ShapeDtypeStruct((b, s, d), BF16),
                   jax.ShapeDtypeStruct((b, s, d), BF16),
                   jax.ShapeDtypeStruct((b, N_HEADS, s), F32)],
        scratch_shapes=[pltpu.VMEM((N_HEADS, 128), F32)],
        compiler_params=pltpu.CompilerParams(
            dimension_semantics=("parallel", "arbitrary"), vmem_limit_bytes=VMEM_LIMIT),
        name="inproj_a",
    )(x, g, wqkfT, wvg, bf, gq, gk)


BIAS_ROWS = 16


def _fox_kernel(q_ref, kT_ref, v_ref, c_ref, o_ref, bias_ref, vext_ref):
    s_len = q_ref.shape[1]
    n_pairs = q_ref.shape[2] // 128
    tq = FOX_Q_TILE
    lane = lax.broadcasted_iota(jnp.int32, (tq, 128), 1)
    first = lane < HEAD_DIM
    ones_a = jnp.where(lane < 3, 1.0, 0.0).astype(BF16)
    ones_b = jnp.where(jnp.logical_and(lane >= 3, lane < 6), 1.0, 0.0).astype(BF16)
    row = lax.broadcasted_iota(jnp.int32, (2 * tq, tq), 0) % tq
    col = lax.broadcasted_iota(jnp.int32, (2 * tq, tq), 1)
    causal = col <= row

    brow = lax.broadcasted_iota(jnp.int32, (BIAS_ROWS, s_len), 0)
    for p in range(n_pairs):
        head = 2 * (pl.program_id(1) * n_pairs + p)
        parts = _split3(-c_ref[0, pl.ds(head, 1), :]) + _split3(-c_ref[0, pl.ds(head + 1, 1), :])
        bias = jnp.zeros((BIAS_ROWS, s_len), F32)
        for r, part in enumerate(parts):
            bias = jnp.where(brow == r, part.astype(F32), bias)
        bias_ref[p] = bias.astype(BF16)
        vext_ref[p, :, :128] = v_ref[0, :, p * 128:(p + 1) * 128]
        vext_ref[p, :, 128:] = jnp.ones((s_len, 128), BF16)

    def rhs(p, c0, c1):
        pad = jnp.zeros((128 - BIAS_ROWS, c1 - c0), BF16)
        return jnp.concatenate(
            [kT_ref[0, p * 128:(p + 1) * 128, c0:c1], bias_ref[p, :, c0:c1], pad], axis=0)

    def logits(item):
        p, qb = item
        r0 = qb * tq
        q2 = q_ref[0, r0:r0 + tq, p * 128:(p + 1) * 128]
        zero = jnp.zeros_like(q2)
        lhs = jnp.concatenate(
            [jnp.concatenate([jnp.where(first, q2, zero), ones_a], axis=1),
             jnp.concatenate([jnp.where(first, zero, q2), ones_b], axis=1)], axis=0)
        s_diag = jnp.where(causal, _dot(lhs, rhs(p, r0, r0 + tq)), NEG)
        s_off = _dot(lhs, rhs(p, 0, r0)) if qb > 0 else None
        return s_diag, s_off

    order = [(p, qb) for qb in reversed(range(s_len // tq)) for p in range(n_pairs)]
    pending = [logits(item) for item in order[:FOX_LOOKAHEAD]]
    for idx, (p, qb) in enumerate(order):
        r0 = qb * tq
        s_diag, s_off = pending.pop(0)
        if idx + FOX_LOOKAHEAD < len(order):
            pending.append(logits(order[idx + FOX_LOOKAHEAD]))
        m = jnp.max(s_diag, axis=1, keepdims=True)
        if qb > 0:
            m = jnp.maximum(m, jnp.max(s_off, axis=1, keepdims=True))
        pv = _dot(jnp.exp2((s_diag - m).astype(BF16)), vext_ref[p, r0:r0 + tq, :])
        if qb > 0:
            pv = pv + _dot(jnp.exp2((s_off - m).astype(BF16)), vext_ref[p, :r0, :])
        o = pv[:, :128] / pv[:, 128:]
        o_ref[0, r0:r0 + tq, p * 128:(p + 1) * 128] = jnp.where(first, o[:tq], o[tq:]).astype(BF16)


def _fox_attention(q, kT, v, cT):
    b, s, d = q.shape
    np_, wd = FOX_PAIRS, FOX_PAIRS * 128
    return pl.pallas_call(
        _fox_kernel,
        grid=(b, d // wd),
        in_specs=[pl.BlockSpec((1, s, wd), lambda bi, j: (bi, 0, j)),
                  pl.BlockSpec((1, wd, s), lambda bi, j: (bi, j, 0)),
                  pl.BlockSpec((1, s, wd), lambda bi, j: (bi, 0, j)),
                  pl.BlockSpec((1, N_HEADS, s), lambda bi, j: (bi, 0, 0))],
        out_specs=pl.BlockSpec((1, s, wd), lambda bi, j: (bi, 0, j)),
        out_shape=jax.ShapeDtypeStruct((b, s, d), BF16),
        scratch_shapes=[pltpu.VMEM((np_, BIAS_ROWS, s), BF16), pltpu.VMEM((np_, s, 256), BF16)],
        compiler_params=pltpu.CompilerParams(
            dimension_semantics=("parallel", "parallel"), vmem_limit_bytes=VMEM_LIMIT),
        name="fox_attention",
    )(q, kT, v, cT)


def _layer_b_kernel(sinks_ref, o_ref, sg_ref, x_ref, wo_ref, gkv_ref, gb_ref, wkvT_ref,
                    gkn_ref, wqT_ref, wgT_ref, gqn_ref, pos_ref, invf_ref, woT_ref,
                    out_ref, kext_ref, vext_ref, qT_ref, sgT_ref, oT_ref):
    tm = x_ref.shape[1]
    w = WINDOW
    nq = GROUP * w
    first_tile = pl.program_id(1) == 0

    lane_head = lax.shift_right_logical(
        lax.broadcasted_iota(jnp.int32, (2 * w, KV_WIDTH), 1), HEAD_DIM.bit_length() - 1)
    q_head = lax.shift_right_logical(lax.broadcasted_iota(jnp.int32, (1, nq), 1), w.bit_length() - 1)
    sinks = []
    for g in range(N_KV_HEADS):
        sink = jnp.zeros((1, nq), F32)
        for i in range(GROUP):
            sink = jnp.where(q_head == i, sinks_ref[g * GROUP + i] * LOG2E, sink)
        sinks.append(sink)
    key = lax.broadcasted_iota(jnp.int32, (w, nq), 0)
    qry = jnp.bitwise_and(lax.broadcasted_iota(jnp.int32, (w, nq), 1), w - 1)
    own = key <= qry
    ones_rows = jnp.ones((SUM_ROWS, 2 * w), BF16)

    parts = [slice(j * SUB_TILE, (j + 1) * SUB_TILE) for j in range(tm // SUB_TILE)]
    for rows in parts:
        og = (o_ref[0, rows, :].astype(F32) * sg_ref[0, rows, :].astype(F32)).astype(BF16)
        out_ref[0, rows, :] = x_ref[0, rows, :] + _dot(og, wo_ref[...])

    @pl.when(first_tile)
    def _():
        kext_ref[0:w, :] = jnp.zeros((w, KV_WIDTH), BF16)
        vext_ref[:, 0:w] = jnp.zeros((KV_WIDTH, w), BF16)

    @pl.when(jnp.logical_not(first_tile))
    def _():
        kext_ref[0:w, :] = kext_ref[tm:tm + w, :]
        vext_ref[:, 0:w] = vext_ref[:, tm:tm + w]

    def projection_stages(cols):
        st = {}
        ext = slice(w + cols.start, w + cols.stop)

        def k_stage():
            h = out_ref[0, cols, :]
            ms = jnp.mean(h * h, axis=-1, keepdims=True)
            hn = h * lax.rsqrt(ms + EPS)
            st["u_kv"] = (hn * gkv_ref[...]).astype(BF16)
            st["u_b"] = (hn * gb_ref[...]).astype(BF16)
            ang = invf_ref[...] * pos_ref[:, cols].astype(F32)
            st["cos"] = jnp.cos(ang)
            st["sin"] = jnp.sin(ang)
            kvT = _dot_nt(wkvT_ref[...], st["u_kv"])
            kT = _head_norm_fm(kvT[:KV_WIDTH], gkn_ref[...])
            kext_ref[ext, :] = _rope_fm(kT, st["cos"], st["sin"]).T.astype(BF16)
            vext_ref[:, ext] = kvT[KV_WIDTH:].astype(BF16)

        def q_stage():
            qT = _head_norm_fm(_dot_nt(wqT_ref[...], st["u_b"]), gqn_ref[...])
            qT_ref[:, cols] = _rope_fm(qT, st["cos"], st["sin"]).astype(BF16)

        def gate_stage():
            gateT = _dot_nt(wgT_ref[...], st["u_b"])
            sgT_ref[:, cols] = (gateT * jax.nn.sigmoid(gateT)).astype(BF16)

        return [k_stage, q_stage, gate_stage]

    def logits(item):
        g, n = item
        k2 = kext_ref[n * w:(n + 2) * w, :]
        k_band = jnp.where(lane_head == g, k2, jnp.zeros_like(k2))
        q_rep = jnp.concatenate(
            [jnp.concatenate(
                [qT_ref[(g * GROUP + i) * HEAD_DIM:(g * GROUP + i + 1) * HEAD_DIM, n * w:(n + 1) * w]]
                * N_KV_HEADS, axis=0) for i in range(GROUP)], axis=1)
        s_all = _dot(k_band, q_rep)
        s_prev = s_all[:w]
        if n == 0:
            s_prev = jnp.where(first_tile, NEG, s_prev)
        return jnp.where(own, s_all[w:], s_prev)

    items = [(g, n) for n in range(tm // w) for g in range(N_KV_HEADS)]
    items_per_part = (SUB_TILE // w) * N_KV_HEADS
    pending = {}

    def attend(idx):
        part_end = (idx // items_per_part + 1) * items_per_part
        for ahead in range(idx, min(idx + SWA_LOOKAHEAD + 1, part_end)):
            if ahead not in pending:
                pending[ahead] = logits(items[ahead])
        g, n = items[idx]
        sT = pending.pop(idx)
        m = jnp.maximum(jnp.max(sT, axis=0, keepdims=True), sinks[g])
        p = jnp.exp2(sT - m).astype(BF16)
        zero = jnp.zeros_like(p)
        p2 = jnp.concatenate([jnp.where(own, zero, p), jnp.where(own, p, zero)], axis=0)
        v_ext = jnp.concatenate(
            [vext_ref[g * HEAD_DIM:(g + 1) * HEAD_DIM, n * w:(n + 2) * w], ones_rows], axis=0)
        out_ext = _dot(v_ext, p2)
        denom = out_ext[HEAD_DIM:HEAD_DIM + 1] + jnp.exp2(sinks[g] - m)
        outT = out_ext[:HEAD_DIM] * (1.0 / denom)
        for i in range(GROUP):
            r0 = (g * GROUP + i) * HEAD_DIM
            oT_ref[r0:r0 + HEAD_DIM, n * w:(n + 1) * w] = outT[:, i * w:(i + 1) * w].astype(BF16)

    def output_projection(cols):
        ogT = (oT_ref[:, cols].astype(F32) * sgT_ref[:, cols].astype(F32)).astype(BF16)
        half = D_MODEL // 2
        outT = jnp.concatenate(
            [_dot(woT_ref[:half, :], ogT), _dot(woT_ref[half:, :], ogT)], axis=0)
        out_ref[0, cols, :] = out_ref[0, cols, :] + outT.T

    stages = [projection_stages(cols) for cols in parts]
    for stage in stages[0]:
        stage()
    for j, cols in enumerate(parts):
        upcoming = stages[j + 1] if j + 1 < len(parts) else []
        for k in range(items_per_part):
            if k % 2 == 0 and k // 2 < len(upcoming):
                upcoming[k // 2]()
            attend(j * items_per_part + k)
        output_projection(cols)


def _layer_b(sinks, o, sg, x, wo, gkv, gb, wkvT, gkn, wqT, wgT, gqn, pos, invf, woT):
    b, s, d = x.shape
    tm = ROW_TILE
    full = lambda a: pl.BlockSpec(a.shape, lambda bi, i: (0,) * a.ndim)
    tok = pl.BlockSpec((1, tm, d), lambda bi, i: (bi, i, 0))
    return pl.pallas_call(
        _layer_b_kernel,
        grid=(b, s // tm),
        in_specs=[pl.BlockSpec(memory_space=pltpu.SMEM),
                  tok, tok, tok, full(wo), full(gkv), full(gb), full(wkvT), full(gkn),
                  full(wqT), full(wgT), full(gqn),
                  pl.BlockSpec((1, tm), lambda bi, i: (0, i)), full(invf), full(woT)],
        out_specs=tok,
        out_shape=jax.ShapeDtypeStruct((b, s, d), F32),
        scratch_shapes=[pltpu.VMEM((tm + WINDOW, KV_WIDTH), BF16),
                        pltpu.VMEM((KV_WIDTH, tm + WINDOW), BF16),
                        pltpu.VMEM((d, tm), BF16),
                        pltpu.VMEM((d, tm), BF16),
                        pltpu.VMEM((d, tm), BF16)],
        compiler_params=pltpu.CompilerParams(
            dimension_semantics=("parallel", "arbitrary"), vmem_limit_bytes=VMEM_LIMIT),
        name="layer_b",
    )(sinks, o, sg, x, wo, gkv, gb, wkvT, gkn, wqT, wgT, gqn, pos, invf, woT)


def _gain_col(g, n_heads, scale=1.0):
    return (jnp.tile(g.astype(F32), n_heads) * scale).reshape(n_heads * HEAD_DIM, 1)


def kernel(x, positions, norm_a_g, w_in_a, b_forget, qnorm_a_g, knorm_a_g, w_out_a, kv_norm_g, w_kv,
           knorm_b_g, norm_b_g, w_in_b, qnorm_b_g, sinks, w_out_b):
    b, s, d = x.shape
    wqkfT, wvg, wo_a, wkvT, wqT, wgT, woT_b = _prep_weights(
        w_in_a[0].T, w_out_a, w_kv, w_in_b, w_out_b)
    q, kT, v, sg, cT = _inproj_a(
        x, norm_a_g[0].reshape(1, d), wqkfT, wvg, b_forget[0].reshape(N_HEADS, 1),
        _gain_col(qnorm_a_g[0], N_HEADS, SCALE * LOG2E), _gain_col(knorm_a_g[0], N_HEADS))
    o = _fox_attention(q, kT, v, cT)

    inv_freq = jnp.power(jnp.float32(ROPE_THETA),
                         -jnp.arange(0, ROT_DIM, 2, dtype=F32) / ROT_DIM).reshape(ROT_HALF, 1)
    return _layer_b(
        sinks[0], o, sg, x, wo_a, kv_norm_g.reshape(1, d), norm_b_g[0].reshape(1, d),
        wkvT, _gain_col(knorm_b_g, N_KV_HEADS),
        wqT, wgT, _gain_col(qnorm_b_g[0], N_HEADS, SCALE * LOG2E),
        positions.reshape(1, s), inv_freq, woT_b)
```

```python
import jax
import jax.numpy as jnp
from jax import lax
from jax.experimental import pallas as pl
from jax.experimental.pallas import tpu as pltpu

D_MODEL = 1024
HEAD_DIM = 64
N_HEADS = 16
N_KV_HEADS = 4
GROUP = N_HEADS // N_KV_HEADS
KV_WIDTH = N_KV_HEADS * HEAD_DIM
WINDOW = 128
ROT_DIM = HEAD_DIM // 4
ROT_HALF = ROT_DIM // 2
ROPE_THETA = 500000.0
EPS = 1e-6
SCALE = HEAD_DIM ** -0.5
LOG2E = 1.4426950408889634
NEG = -1e30

F32 = jnp.float32
BF16 = jnp.bfloat16

PREP_ROWS = 128
ROW_TILE = 1024
SUB_TILE = 256
FOX_Q_TILE = 256
FOX_PAIRS = 2
FOX_LOOKAHEAD = 2
SWA_LOOKAHEAD = 4
STAGE_SPACING = 2
Q_STAGE_SLOT = STAGE_SPACING
SUM_ROWS = 16
VMEM_LIMIT = 56 * 1024 * 1024


def _dot(a, b):
    return jnp.dot(a, b, preferred_element_type=F32)


def _dot_nt(a, b):
    return lax.dot_general(a, b, (((1,), (1,)), ((), ())), preferred_element_type=F32)


def _head_norm_fm(t, gain_col):
    blocks = []
    for h in range(t.shape[0] // HEAD_DIM):
        blk = t[h * HEAD_DIM:(h + 1) * HEAD_DIM, :]
        ms = jnp.mean(blk * blk, axis=0, keepdims=True)
        blocks.append(blk * lax.rsqrt(ms + EPS))
    return jnp.concatenate(blocks, axis=0) * gain_col


def _rope_fm(t, cos, sin):
    blocks = []
    for h in range(t.shape[0] // HEAD_DIM):
        base = h * HEAD_DIM
        x1 = t[base:base + ROT_HALF, :]
        x2 = t[base + ROT_HALF:base + ROT_DIM, :]
        blocks.append(x1 * cos - x2 * sin)
        blocks.append(x1 * sin + x2 * cos)
        blocks.append(t[base + ROT_DIM:base + HEAD_DIM, :])
    return jnp.concatenate(blocks, axis=0)


def _split3(x):
    hi = x.astype(BF16)
    r1 = x - hi.astype(F32)
    mid = r1.astype(BF16)
    lo = (r1 - mid.astype(F32)).astype(BF16)
    return hi, mid, lo


def _prep_kernel(waT_ref, woa_ref, wkv_ref, wb_ref, wob_ref,
                 wqkfT_ref, wvg_ref, wo_ref, wkvT_ref, wqT_ref, wgT_ref, woT_ref):
    d = D_MODEL
    wqkfT_ref[:2 * d, :] = waT_ref[:2 * d, :].astype(BF16)
    wqkfT_ref[2 * d:, :] = waT_ref[3 * d:3 * d + N_HEADS, :].astype(BF16)
    wvg_ref[:, :d] = waT_ref[2 * d:3 * d, :].T.astype(BF16)
    wvg_ref[:, d:] = waT_ref[3 * d + N_HEADS:, :].T.astype(BF16)
    wo_ref[...] = woa_ref[0].astype(BF16)
    wkvT_ref[...] = wkv_ref[...].T.astype(BF16)
    wb = wb_ref[0]
    wqT_ref[...] = wb[:, :d].T.astype(BF16)
    wgT_ref[...] = wb[:, d:].T.astype(BF16)
    woT_ref[...] = wob_ref[0].T.astype(BF16)


def _prep_weights(w_in_aT, w_out_a, w_kv, w_in_b, w_out_b):
    d = D_MODEL
    r = PREP_ROWS
    rows3 = lambda a: pl.BlockSpec((1, r, a.shape[2]), lambda i: (0, i, 0))
    same = lambda n: pl.BlockSpec((r, n), lambda i: (i, 0))
    trans = lambda m: pl.BlockSpec((m, r), lambda i: (0, i))
    shapes = [((2 * d + N_HEADS, d), trans(2 * d + N_HEADS)), ((d, 2 * d), same(2 * d)),
              ((d, d), same(d)), ((2 * KV_WIDTH, d), trans(2 * KV_WIDTH)),
              ((d, d), trans(d)), ((d, d), trans(d)), ((d, d), trans(d))]
    return pl.pallas_call(
        _prep_kernel,
        grid=(d // r,),
        in_specs=[pl.BlockSpec((w_in_aT.shape[0], r), lambda i: (0, i)),
                  rows3(w_out_a), pl.BlockSpec((r, w_kv.shape[1]), lambda i: (i, 0)),
                  rows3(w_in_b), rows3(w_out_b)],
        out_specs=[spec for _, spec in shapes],
        out_shape=[jax.ShapeDtypeStruct(shape, BF16) for shape, _ in shapes],
        compiler_params=pltpu.CompilerParams(
            dimension_semantics=("parallel",), vmem_limit_bytes=VMEM_LIMIT),
        name="prep_weights",
    )(w_in_aT, w_out_a, w_kv, w_in_b, w_out_b)


def _inproj_a_kernel(x_ref, g_ref, wqkfT_ref, wvg_ref, bf_ref, gq_ref, gk_ref,
                     q_ref, kT_ref, v_ref, sg_ref, cT_ref, carry_ref):
    tm = x_ref.shape[1]

    @pl.when(pl.program_id(1) == 0)
    def _():
        carry_ref[...] = jnp.zeros_like(carry_ref)

    parts = [slice(j * SUB_TILE, (j + 1) * SUB_TILE) for j in range(tm // SUB_TILE)]
    u_parts, log_f_parts = [], []
    for rows in parts:
        x = x_ref[0, rows, :]
        ms = jnp.mean(x * x, axis=-1, keepdims=True)
        u32 = x * lax.rsqrt(ms + EPS) * g_ref[...]
        u_parts.append(u32.astype(BF16))
        uT = u32.T.astype(BF16)
        qT = _head_norm_fm(_dot(wqkfT_ref[:D_MODEL, :], uT), gq_ref[...])
        kfT = _dot(wqkfT_ref[D_MODEL:, :], uT)
        kT = _head_norm_fm(kfT[:D_MODEL], gk_ref[...])
        q_ref[0, rows, :] = qT.T.astype(BF16)
        kT_ref[0, :, rows] = kT.astype(BF16)
        f = kfT[D_MODEL:] + bf_ref[...]
        log_f_parts.append(jnp.minimum(f, 0.0) - jnp.log1p(jnp.exp(-jnp.abs(f))))
    u = jnp.concatenate(u_parts, axis=0)

    gate = _dot(u, wvg_ref[:, D_MODEL:])
    sg_ref[0] = (gate * jax.nn.sigmoid(gate)).astype(BF16)

    row = lax.broadcasted_iota(jnp.int32, (SUB_TILE, SUB_TILE), 0)
    col = lax.broadcasted_iota(jnp.int32, (SUB_TILE, SUB_TILE), 1)
    tri = (row <= col).astype(BF16)
    sums = [_dot(jnp.concatenate(_split3(log_f), axis=0), tri) for log_f in log_f_parts]
    v_ref[0] = _dot(u, wvg_ref[:, :D_MODEL]).astype(BF16)
    carry = carry_ref[:, 0:1]
    for rows, s3 in zip(parts, sums):
        c = (s3[:N_HEADS] + s3[N_HEADS:2 * N_HEADS] + s3[2 * N_HEADS:]) + carry
        cT_ref[0, :, rows] = c * LOG2E
        carry = c[:, SUB_TILE - 1:SUB_TILE]
    carry_ref[...] = jnp.broadcast_to(carry, carry_ref.shape)


def _inproj_a(x, g, wqkfT, wvg, bf, gq, gk):
    b, s, d = x.shape
    tm = ROW_TILE
    full = lambda shape: pl.BlockSpec(shape, lambda bi, i: (0,) * len(shape))
    tok = pl.BlockSpec((1, tm, d), lambda bi, i: (bi, i, 0))
    return pl.pallas_call(
        _inproj_a_kernel,
        grid=(b, s // tm),
        in_specs=[tok, full(g.shape), full(wqkfT.shape), full(wvg.shape),
                  full(bf.shape), full(gq.shape), full(gk.shape)],
        out_specs=[tok,
                   pl.BlockSpec((1, d, tm), lambda bi, i: (bi, 0, i)),
                   tok, tok,
                   pl.BlockSpec((1, N_HEADS, tm), lambda bi, i: (bi, 0, i))],
        out_shape=[jax.ShapeDtypeStruct((b, s, d), BF16),
                   jax.ShapeDtypeStruct((b, d, s), BF16),
                   jax.ShapeDtypeStruct((b, s, d), BF16),
                   jax.ShapeDtypeStruct((b, s, d), BF16),
                   jax.ShapeDtypeStruct((b, N_HEADS, s), F32)],
        scratch_shapes=[pltpu.VMEM((N_HEADS, 128), F32)],
        compiler_params=pltpu.CompilerParams(
            dimension_semantics=("parallel", "arbitrary"), vmem_limit_bytes=VMEM_LIMIT),
        name="inproj_a",
    )(x, g, wqkfT, wvg, bf, gq, gk)


BIAS_ROWS = 16


def _fox_kernel(q_ref, kT_ref, v_ref, c_ref, o_ref, bias_ref, vext_ref):
    s_len = q_ref.shape[1]
    n_pairs = q_ref.shape[2] // 128
    tq = FOX_Q_TILE
    lane = lax.broadcasted_iota(jnp.int32, (tq, 128), 1)
    first = lane < HEAD_DIM
    ones_a = jnp.where(lane < 3, 1.0, 0.0).astype(BF16)
    ones_b = jnp.where(jnp.logical_and(lane >= 3, lane < 6), 1.0, 0.0).astype(BF16)
    row = lax.broadcasted_iota(jnp.int32, (2 * tq, tq), 0) % tq
    col = lax.broadcasted_iota(jnp.int32, (2 * tq, tq), 1)
    causal = col <= row

    brow = lax.broadcasted_iota(jnp.int32, (BIAS_ROWS, s_len), 0)
    for p in range(n_pairs):
        head = 2 * (pl.program_id(1) * n_pairs + p)
        parts = _split3(-c_ref[0, pl.ds(head, 1), :]) + _split3(-c_ref[0, pl.ds(head + 1, 1), :])
        bias = jnp.zeros((BIAS_ROWS, s_len), F32)
        for r, part in enumerate(parts):
            bias = jnp.where(brow == r, part.astype(F32), bias)
        bias_ref[p] = bias.astype(BF16)
        vext_ref[p, :, :128] = v_ref[0, :, p * 128:(p + 1) * 128]
        vext_ref[p, :, 128:] = jnp.ones((s_len, 128), BF16)

    def rhs(p, c0, c1):
        pad = jnp.zeros((128 - BIAS_ROWS, c1 - c0), BF16)
        return jnp.concatenate(
            [kT_ref[0, p * 128:(p + 1) * 128, c0:c1], bias_ref[p, :, c0:c1], pad], axis=0)

    def logits(item):
        p, qb = item
        r0 = qb * tq
        q2 = q_ref[0, r0:r0 + tq, p * 128:(p + 1) * 128]
        zero = jnp.zeros_like(q2)
        lhs = jnp.concatenate(
            [jnp.concatenate([jnp.where(first, q2, zero), ones_a], axis=1),
             jnp.concatenate([jnp.where(first, zero, q2), ones_b], axis=1)], axis=0)
        s_diag = jnp.where(causal, _dot(lhs, rhs(p, r0, r0 + tq)), NEG)
        s_off = _dot(lhs, rhs(p, 0, r0)) if qb > 0 else None
        return s_diag, s_off

    order = [(p, qb) for qb in reversed(range(s_len // tq)) for p in range(n_pairs)]
    pending = [logits(item) for item in order[:FOX_LOOKAHEAD]]
    for idx, (p, qb) in enumerate(order):
        r0 = qb * tq
        s_diag, s_off = pending.pop(0)
        if idx + FOX_LOOKAHEAD < len(order):
            pending.append(logits(order[idx + FOX_LOOKAHEAD]))
        m = jnp.max(s_diag, axis=1, keepdims=True)
        if qb > 0:
            m = jnp.maximum(m, jnp.max(s_off, axis=1, keepdims=True))
        pv = _dot(jnp.exp2((s_diag - m).astype(BF16)), vext_ref[p, r0:r0 + tq, :])
        if qb > 0:
            pv = pv + _dot(jnp.exp2((s_off - m).astype(BF16)), vext_ref[p, :r0, :])
        o = pv[:, :128] / pv[:, 128:]
        o_ref[0, r0:r0 + tq, p * 128:(p + 1) * 128] = jnp.where(first, o[:tq], o[tq:]).astype(BF16)


def _fox_attention(q, kT, v, cT):
    b, s, d = q.shape
    np_, wd = FOX_PAIRS, FOX_PAIRS * 128
    return pl.pallas_call(
        _fox_kernel,
        grid=(b, d // wd),
        in_specs=[pl.BlockSpec((1, s, wd), lambda bi, j: (bi, 0, j)),
                  pl.BlockSpec((1, wd, s), lambda bi, j: (bi, j, 0)),
                  pl.BlockSpec((1, s, wd), lambda bi, j: (bi, 0, j)),
                  pl.BlockSpec((1, N_HEADS, s), lambda bi, j: (bi, 0, 0))],
        out_specs=pl.BlockSpec((1, s, wd), lambda bi, j: (bi, 0, j)),
        out_shape=jax.ShapeDtypeStruct((b, s, d), BF16),
        scratch_shapes=[pltpu.VMEM((np_, BIAS_ROWS, s), BF16), pltpu.VMEM((np_, s, 256), BF16)],
        compiler_params=pltpu.CompilerParams(
            dimension_semantics=("parallel", "parallel"), vmem_limit_bytes=VMEM_LIMIT),
        name="fox_attention",
    )(q, kT, v, cT)


def _layer_b_kernel(sinks_ref, o_ref, sg_ref, x_ref, wo_ref, gkv_ref, gb_ref, wkvT_ref,
                    gkn_ref, wqT_ref, wgT_ref, gqn_ref, pos_ref, invf_ref, woT_ref,
                    out_ref, kext_ref, vext_ref, qT_ref, sgT_ref, oT_ref):
    tm = x_ref.shape[1]
    w = WINDOW
    nq = GROUP * w
    first_tile = pl.program_id(1) == 0

    lane_head = lax.shift_right_logical(
        lax.broadcasted_iota(jnp.int32, (2 * w, KV_WIDTH), 1), HEAD_DIM.bit_length() - 1)
    q_head = lax.shift_right_logical(lax.broadcasted_iota(jnp.int32, (1, nq), 1), w.bit_length() - 1)
    sinks = []
    for g in range(N_KV_HEADS):
        sink = jnp.zeros((1, nq), F32)
        for i in range(GROUP):
            sink = jnp.where(q_head == i, sinks_ref[g * GROUP + i] * LOG2E, sink)
        sinks.append(sink)
    key = lax.broadcasted_iota(jnp.int32, (w, nq), 0)
    qry = jnp.bitwise_and(lax.broadcasted_iota(jnp.int32, (w, nq), 1), w - 1)
    own = key <= qry
    ones_rows = jnp.ones((SUM_ROWS, 2 * w), BF16)

    parts = [slice(j * SUB_TILE, (j + 1) * SUB_TILE) for j in range(tm // SUB_TILE)]
    for rows in parts:
        og = (o_ref[0, rows, :].astype(F32) * sg_ref[0, rows, :].astype(F32)).astype(BF16)
        out_ref[0, rows, :] = x_ref[0, rows, :] + _dot(og, wo_ref[...])

    @pl.when(first_tile)
    def _():
        kext_ref[0:w, :] = jnp.zeros((w, KV_WIDTH), BF16)
        vext_ref[:, 0:w] = jnp.zeros((KV_WIDTH, w), BF16)

    @pl.when(jnp.logical_not(first_tile))
    def _():
        kext_ref[0:w, :] = kext_ref[tm:tm + w, :]
        vext_ref[:, 0:w] = vext_ref[:, tm:tm + w]

    def projection_stages(cols):
        st = {}
        ext = slice(w + cols.start, w + cols.stop)

        def k_stage():
            h = out_ref[0, cols, :]
            ms = jnp.mean(h * h, axis=-1, keepdims=True)
            hn = h * lax.rsqrt(ms + EPS)
            st["u_kv"] = (hn * gkv_ref[...]).astype(BF16)
            st["u_b"] = (hn * gb_ref[...]).astype(BF16)
            ang = invf_ref[...] * pos_ref[:, cols].astype(F32)
            st["cos"] = jnp.cos(ang)
            st["sin"] = jnp.sin(ang)
            kvT = _dot_nt(wkvT_ref[...], st["u_kv"])
            kT = _head_norm_fm(kvT[:KV_WIDTH], gkn_ref[...])
            kext_ref[ext, :] = _rope_fm(kT, st["cos"], st["sin"]).T.astype(BF16)
            vext_ref[:, ext] = kvT[KV_WIDTH:].astype(BF16)

        def q_stage():
            qT = _head_norm_fm(_dot_nt(wqT_ref[...], st["u_b"]), gqn_ref[...])
            qT_ref[:, cols] = _rope_fm(qT, st["cos"], st["sin"]).astype(BF16)

        def gate_stage():
            gateT = _dot_nt(wgT_ref[...], st["u_b"])
            sgT_ref[:, cols] = (gateT * jax.nn.sigmoid(gateT)).astype(BF16)

        return [k_stage, q_stage, gate_stage]

    def logits(item):
        g, n = item
        k2 = kext_ref[n * w:(n + 2) * w, :]
        k_band = jnp.where(lane_head == g, k2, jnp.zeros_like(k2))
        q_rep = jnp.concatenate(
            [jnp.concatenate(
                [qT_ref[(g * GROUP + i) * HEAD_DIM:(g * GROUP + i + 1) * HEAD_DIM, n * w:(n + 1) * w]]
                * N_KV_HEADS, axis=0) for i in range(GROUP)], axis=1)
        s_all = _dot(k_band, q_rep)
        s_prev = s_all[:w]
        if n == 0:
            s_prev = jnp.where(first_tile, NEG, s_prev)
        return jnp.where(own, s_all[w:], s_prev)

    items = [(g, n) for n in range(tm // w) for g in range(N_KV_HEADS)]
    items_per_part = (SUB_TILE // w) * N_KV_HEADS
    pending = {}

    def attend(idx):
        ready_end = (idx // items_per_part + 1) * items_per_part
        if idx % items_per_part >= Q_STAGE_SLOT:
            ready_end = min(ready_end + items_per_part, len(items))
        for ahead in range(idx, min(idx + SWA_LOOKAHEAD + 1, ready_end)):
            if ahead not in pending:
                pending[ahead] = logits(items[ahead])
        g, n = items[idx]
        sT = pending.pop(idx)
        m = jnp.maximum(jnp.max(sT, axis=0, keepdims=True), sinks[g])
        p = jnp.exp2(sT - m).astype(BF16)
        zero = jnp.zeros_like(p)
        p2 = jnp.concatenate([jnp.where(own, zero, p), jnp.where(own, p, zero)], axis=0)
        v_ext = jnp.concatenate(
            [vext_ref[g * HEAD_DIM:(g + 1) * HEAD_DIM, n * w:(n + 2) * w], ones_rows], axis=0)
        out_ext = _dot(v_ext, p2)
        denom = out_ext[HEAD_DIM:HEAD_DIM + 1] + jnp.exp2(sinks[g] - m)
        outT = out_ext[:HEAD_DIM] * (1.0 / denom)
        for i in range(GROUP):
            r0 = (g * GROUP + i) * HEAD_DIM
            oT_ref[r0:r0 + HEAD_DIM, n * w:(n + 1) * w] = outT[:, i * w:(i + 1) * w].astype(BF16)

    def output_projection(cols):
        ogT = (oT_ref[:, cols].astype(F32) * sgT_ref[:, cols].astype(F32)).astype(BF16)
        half = D_MODEL // 2
        outT = jnp.concatenate(
            [_dot(woT_ref[:half, :], ogT), _dot(woT_ref[half:, :], ogT)], axis=0)
        out_ref[0, cols, :] = out_ref[0, cols, :] + outT.T

    stages = [projection_stages(cols) for cols in parts]
    for stage in stages[0]:
        stage()
    for j, cols in enumerate(parts):
        upcoming = stages[j + 1] if j + 1 < len(parts) else []
        for k in range(items_per_part):
            if k % STAGE_SPACING == 0 and k // STAGE_SPACING < len(upcoming):
                upcoming[k // STAGE_SPACING]()
            attend(j * items_per_part + k)
        output_projection(cols)


def _layer_b(sinks, o, sg, x, wo, gkv, gb, wkvT, gkn, wqT, wgT, gqn, pos, invf, woT):
    b, s, d = x.shape
    tm = ROW_TILE
    full = lambda a: pl.BlockSpec(a.shape, lambda bi, i: (0,) * a.ndim)
    tok = pl.BlockSpec((1, tm, d), lambda bi, i: (bi, i, 0))
    return pl.pallas_call(
        _layer_b_kernel,
        grid=(b, s // tm),
        in_specs=[pl.BlockSpec(memory_space=pltpu.SMEM),
                  tok, tok, tok, full(wo), full(gkv), full(gb), full(wkvT), full(gkn),
                  full(wqT), full(wgT), full(gqn),
                  pl.BlockSpec((1, tm), lambda bi, i: (0, i)), full(invf), full(woT)],
        out_specs=tok,
        out_shape=jax.ShapeDtypeStruct((b, s, d), F32),
        scratch_shapes=[pltpu.VMEM((tm + WINDOW, KV_WIDTH), BF16),
                        pltpu.VMEM((KV_WIDTH, tm + WINDOW), BF16),
                        pltpu.VMEM((d, tm), BF16),
                        pltpu.VMEM((d, tm), BF16),
                        pltpu.VMEM((d, tm), BF16)],
        compiler_params=pltpu.CompilerParams(
            dimension_semantics=("parallel", "arbitrary"), vmem_limit_bytes=VMEM_LIMIT),
        name="layer_b",
    )(sinks, o, sg, x, wo, gkv, gb, wkvT, gkn, wqT, wgT, gqn, pos, invf, woT)


def _gain_col(g, n_heads, scale=1.0):
    return (jnp.tile(g.astype(F32), n_heads) * scale).reshape(n_heads * HEAD_DIM, 1)


def kernel(x, positions, norm_a_g, w_in_a, b_forget, qnorm_a_g, knorm_a_g, w_out_a, kv_norm_g, w_kv,
           knorm_b_g, norm_b_g, w_in_b, qnorm_b_g, sinks, w_out_b):
    b, s, d = x.shape
    wqkfT, wvg, wo_a, wkvT, wqT, wgT, woT_b = _prep_weights(
        w_in_a[0].T, w_out_a, w_kv, w_in_b, w_out_b)
    q, kT, v, sg, cT = _inproj_a(
        x, norm_a_g[0].reshape(1, d), wqkfT, wvg, b_forget[0].reshape(N_HEADS, 1),
        _gain_col(qnorm_a_g[0], N_HEADS, SCALE * LOG2E), _gain_col(knorm_a_g[0], N_HEADS))
    o = _fox_attention(q, kT, v, cT)

    inv_freq = jnp.power(jnp.float32(ROPE_THETA),
                         -jnp.arange(0, ROT_DIM, 2, dtype=F32) / ROT_DIM).reshape(ROT_HALF, 1)
    return _layer_b(
        sinks[0], o, sg, x, wo_a, kv_norm_g.reshape(1, d), norm_b_g[0].reshape(1, d),
        wkvT, _gain_col(knorm_b_g, N_KV_HEADS),
        wqT, wgT, _gain_col(qnorm_b_g[0], N_HEADS, SCALE * LOG2E),
        positions.reshape(1, s), inv_freq, woT_b)
```

```python
import jax
import jax.numpy as jnp
from jax import lax
from jax.experimental import pallas as pl
from jax.experimental.pallas import tpu as pltpu

D_MODEL = 1024
HEAD_DIM = 64
N_HEADS = 16
N_KV_HEADS = 4
GROUP = N_HEADS // N_KV_HEADS
KV_WIDTH = N_KV_HEADS * HEAD_DIM
WINDOW = 128
ROT_DIM = HEAD_DIM // 4
ROT_HALF = ROT_DIM // 2
ROPE_THETA = 500000.0
EPS = 1e-6
SCALE = HEAD_DIM ** -0.5
LOG2E = 1.4426950408889634
NEG = -1e30

F32 = jnp.float32
BF16 = jnp.bfloat16

PREP_ROWS = 128
ROW_TILE = 1024
SUB_TILE = 256
FOX_Q_TILE = 256
FOX_PAIRS = 2
FOX_LOOKAHEAD = 2
SWA_LOOKAHEAD = 4
STAGE_SPACING = 2
Q_STAGE_SLOT = STAGE_SPACING
SUM_ROWS = 16
VMEM_LIMIT = 56 * 1024 * 1024


def _dot(a, b):
    return jnp.dot(a, b, preferred_element_type=F32)


def _dot_nt(a, b):
    return lax.dot_general(a, b, (((1,), (1,)), ((), ())), preferred_element_type=F32)


def _head_norm_fm(t, gain_col):
    blocks = []
    for h in range(t.shape[0] // HEAD_DIM):
        blk = t[h * HEAD_DIM:(h + 1) * HEAD_DIM, :]
        ms = jnp.mean(blk * blk, axis=0, keepdims=True)
        blocks.append(blk * lax.rsqrt(ms + EPS))
    return jnp.concatenate(blocks, axis=0) * gain_col


def _rope_fm(t, cos, sin):
    blocks = []
    for h in range(t.shape[0] // HEAD_DIM):
        base = h * HEAD_DIM
        x1 = t[base:base + ROT_HALF, :]
        x2 = t[base + ROT_HALF:base + ROT_DIM, :]
        blocks.append(x1 * cos - x2 * sin)
        blocks.append(x1 * sin + x2 * cos)
        blocks.append(t[base + ROT_DIM:base + HEAD_DIM, :])
    return jnp.concatenate(blocks, axis=0)


def _split3(x):
    hi = x.astype(BF16)
    r1 = x - hi.astype(F32)
    mid = r1.astype(BF16)
    lo = (r1 - mid.astype(F32)).astype(BF16)
    return hi, mid, lo


def _prep_kernel(waT_ref, wqkfT_ref, wvg_ref):
    d = D_MODEL
    wqkfT_ref[:2 * d, :] = waT_ref[:2 * d, :].astype(BF16)
    wqkfT_ref[2 * d:, :] = waT_ref[3 * d:3 * d + N_HEADS, :].astype(BF16)
    wvg_ref[:, :d] = waT_ref[2 * d:3 * d, :].T.astype(BF16)
    wvg_ref[:, d:] = waT_ref[3 * d + N_HEADS:, :].T.astype(BF16)


def _prep_weights_a(w_in_aT):
    d = D_MODEL
    r = PREP_ROWS
    return pl.pallas_call(
        _prep_kernel,
        grid=(d // r,),
        in_specs=[pl.BlockSpec((w_in_aT.shape[0], r), lambda i: (0, i))],
        out_specs=[pl.BlockSpec((2 * d + N_HEADS, r), lambda i: (0, i)),
                   pl.BlockSpec((r, 2 * d), lambda i: (i, 0))],
        out_shape=[jax.ShapeDtypeStruct((2 * d + N_HEADS, d), BF16),
                   jax.ShapeDtypeStruct((d, 2 * d), BF16)],
        compiler_params=pltpu.CompilerParams(
            dimension_semantics=("parallel",), vmem_limit_bytes=VMEM_LIMIT),
        name="prep_weights_a",
    )(w_in_aT)


def _prep_layer_b_rows(woa_ref, wkv_ref, wb_ref, wob_ref, wo_ref, wkvT_ref, wqT_ref, wgT_ref, woT_ref):
    d = D_MODEL
    wo_ref[...] = woa_ref[0].astype(BF16)
    wkvT_ref[...] = wkv_ref[...].T.astype(BF16)
    wb = wb_ref[0]
    wqT_ref[...] = wb[:, :d].T.astype(BF16)
    wgT_ref[...] = wb[:, d:].T.astype(BF16)
    woT_ref[...] = wob_ref[0].T.astype(BF16)


def _inproj_a_kernel(x_ref, g_ref, wqkfT_ref, wvg_ref, bf_ref, gq_ref, gk_ref,
                     q_ref, kT_ref, v_ref, sg_ref, cT_ref, carry_ref):
    tm = x_ref.shape[1]

    @pl.when(pl.program_id(1) == 0)
    def _():
        carry_ref[...] = jnp.zeros_like(carry_ref)

    parts = [slice(j * SUB_TILE, (j + 1) * SUB_TILE) for j in range(tm // SUB_TILE)]
    u_parts, log_f_parts = [], []
    for rows in parts:
        x = x_ref[0, rows, :]
        ms = jnp.mean(x * x, axis=-1, keepdims=True)
        u32 = x * lax.rsqrt(ms + EPS) * g_ref[...]
        u_parts.append(u32.astype(BF16))
        uT = u32.T.astype(BF16)
        qT = _head_norm_fm(_dot(wqkfT_ref[:D_MODEL, :], uT), gq_ref[...])
        kfT = _dot(wqkfT_ref[D_MODEL:, :], uT)
        kT = _head_norm_fm(kfT[:D_MODEL], gk_ref[...])
        q_ref[0, rows, :] = qT.T.astype(BF16)
        kT_ref[0, :, rows] = kT.astype(BF16)
        f = kfT[D_MODEL:] + bf_ref[...]
        log_f_parts.append(jnp.minimum(f, 0.0) - jnp.log1p(jnp.exp(-jnp.abs(f))))
    u = jnp.concatenate(u_parts, axis=0)

    gate = _dot(u, wvg_ref[:, D_MODEL:])
    sg_ref[0] = (gate * jax.nn.sigmoid(gate)).astype(BF16)

    row = lax.broadcasted_iota(jnp.int32, (SUB_TILE, SUB_TILE), 0)
    col = lax.broadcasted_iota(jnp.int32, (SUB_TILE, SUB_TILE), 1)
    tri = (row <= col).astype(BF16)
    sums = [_dot(jnp.concatenate(_split3(log_f), axis=0), tri) for log_f in log_f_parts]
    v_ref[0] = _dot(u, wvg_ref[:, :D_MODEL]).astype(BF16)
    carry = carry_ref[:, 0:1]
    for rows, s3 in zip(parts, sums):
        c = (s3[:N_HEADS] + s3[N_HEADS:2 * N_HEADS] + s3[2 * N_HEADS:]) + carry
        cT_ref[0, :, rows] = c * LOG2E
        carry = c[:, SUB_TILE - 1:SUB_TILE]
    carry_ref[...] = jnp.broadcast_to(carry, carry_ref.shape)


def _inproj_a(x, g, wqkfT, wvg, bf, gq, gk):
    b, s, d = x.shape
    tm = ROW_TILE
    full = lambda shape: pl.BlockSpec(shape, lambda bi, i: (0,) * len(shape))
    tok = pl.BlockSpec((1, tm, d), lambda bi, i: (bi, i, 0))
    return pl.pallas_call(
        _inproj_a_kernel,
        grid=(b, s // tm),
        in_specs=[tok, full(g.shape), full(wqkfT.shape), full(wvg.shape),
                  full(bf.shape), full(gq.shape), full(gk.shape)],
        out_specs=[tok,
                   pl.BlockSpec((1, d, tm), lambda bi, i: (bi, 0, i)),
                   tok, tok,
                   pl.BlockSpec((1, N_HEADS, tm), lambda bi, i: (bi, 0, i))],
        out_shape=[jax.ShapeDtypeStruct((b, s, d), BF16),
                   jax.ShapeDtypeStruct((b, d, s), BF16),
                   jax.ShapeDtypeStruct((b, s, d), BF16),
                   jax.ShapeDtypeStruct((b, s, d), BF16),
                   jax.ShapeDtypeStruct((b, N_HEADS, s), F32)],
        scratch_shapes=[pltpu.VMEM((N_HEADS, 128), F32)],
        compiler_params=pltpu.CompilerParams(
            dimension_semantics=("parallel", "arbitrary"), vmem_limit_bytes=VMEM_LIMIT),
        name="inproj_a",
    )(x, g, wqkfT, wvg, bf, gq, gk)


BIAS_ROWS = 16


def _fox_kernel(q_ref, kT_ref, v_ref, c_ref, woa_ref, wkv_ref, wb_ref, wob_ref,
                o_ref, wo_ref, wkvT_ref, wqT_ref, wgT_ref, woT_ref, bias_ref, vext_ref):
    @pl.when(pl.program_id(1) == 0)
    def _():
        _prep_layer_b_rows(woa_ref, wkv_ref, wb_ref, wob_ref, wo_ref, wkvT_ref, wqT_ref, wgT_ref, woT_ref)

    s_len = q_ref.shape[1]
    n_pairs = q_ref.shape[2] // 128
    tq = FOX_Q_TILE
    lane = lax.broadcasted_iota(jnp.int32, (tq, 128), 1)
    first = lane < HEAD_DIM
    ones_a = jnp.where(lane < 3, 1.0, 0.0).astype(BF16)
    ones_b = jnp.where(jnp.logical_and(lane >= 3, lane < 6), 1.0, 0.0).astype(BF16)
    row = lax.broadcasted_iota(jnp.int32, (2 * tq, tq), 0) % tq
    col = lax.broadcasted_iota(jnp.int32, (2 * tq, tq), 1)
    causal = col <= row

    brow = lax.broadcasted_iota(jnp.int32, (BIAS_ROWS, s_len), 0)
    for p in range(n_pairs):
        head = 2 * (pl.program_id(1) * n_pairs + p)
        parts = _split3(-c_ref[0, pl.ds(head, 1), :]) + _split3(-c_ref[0, pl.ds(head + 1, 1), :])
        bias = jnp.zeros((BIAS_ROWS, s_len), F32)
        for r, part in enumerate(parts):
            bias = jnp.where(brow == r, part.astype(F32), bias)
        bias_ref[p] = bias.astype(BF16)
        vext_ref[p, :, :128] = v_ref[0, :, p * 128:(p + 1) * 128]
        vext_ref[p, :, 128:] = jnp.ones((s_len, 128), BF16)

    def rhs(p, c0, c1):
        pad = jnp.zeros((128 - BIAS_ROWS, c1 - c0), BF16)
        return jnp.concatenate(
            [kT_ref[0, p * 128:(p + 1) * 128, c0:c1], bias_ref[p, :, c0:c1], pad], axis=0)

    def logits(item):
        p, qb = item
        r0 = qb * tq
        q2 = q_ref[0, r0:r0 + tq, p * 128:(p + 1) * 128]
        zero = jnp.zeros_like(q2)
        lhs = jnp.concatenate(
            [jnp.concatenate([jnp.where(first, q2, zero), ones_a], axis=1),
             jnp.concatenate([jnp.where(first, zero, q2), ones_b], axis=1)], axis=0)
        s_diag = jnp.where(causal, _dot(lhs, rhs(p, r0, r0 + tq)), NEG)
        s_off = _dot(lhs, rhs(p, 0, r0)) if qb > 0 else None
        return s_diag, s_off

    order = [(p, qb) for qb in reversed(range(s_len // tq)) for p in range(n_pairs)]
    pending = [logits(item) for item in order[:FOX_LOOKAHEAD]]
    for idx, (p, qb) in enumerate(order):
        r0 = qb * tq
        s_diag, s_off = pending.pop(0)
        if idx + FOX_LOOKAHEAD < len(order):
            pending.append(logits(order[idx + FOX_LOOKAHEAD]))
        m = jnp.max(s_diag, axis=1, keepdims=True)
        if qb > 0:
            m = jnp.maximum(m, jnp.max(s_off, axis=1, keepdims=True))
        pv = _dot(jnp.exp2((s_diag - m).astype(BF16)), vext_ref[p, r0:r0 + tq, :])
        if qb > 0:
            pv = pv + _dot(jnp.exp2((s_off - m).astype(BF16)), vext_ref[p, :r0, :])
        o = pv[:, :128] / pv[:, 128:]
        o_ref[0, r0:r0 + tq, p * 128:(p + 1) * 128] = jnp.where(first, o[:tq], o[tq:]).astype(BF16)


def _fox_attention(q, kT, v, cT, w_out_a, w_kv, w_in_b, w_out_b):
    b, s, d = q.shape
    np_, wd = FOX_PAIRS, FOX_PAIRS * 128
    r = d // b
    assert r * b == d and r % 128 == 0
    rows3 = lambda a: pl.BlockSpec((1, r, a.shape[2]), lambda bi, j: (0, bi, 0))
    same = lambda n: pl.BlockSpec((r, n), lambda bi, j: (bi, 0))
    trans = lambda m: pl.BlockSpec((m, r), lambda bi, j: (0, bi))
    weights = [((d, d), same(d)), ((2 * KV_WIDTH, d), trans(2 * KV_WIDTH)),
               ((d, d), trans(d)), ((d, d), trans(d)), ((d, d), trans(d))]
    return pl.pallas_call(
        _fox_kernel,
        grid=(b, d // wd),
        in_specs=[pl.BlockSpec((1, s, wd), lambda bi, j: (bi, 0, j)),
                  pl.BlockSpec((1, wd, s), lambda bi, j: (bi, j, 0)),
                  pl.BlockSpec((1, s, wd), lambda bi, j: (bi, 0, j)),
                  pl.BlockSpec((1, N_HEADS, s), lambda bi, j: (bi, 0, 0)),
                  rows3(w_out_a), pl.BlockSpec((r, w_kv.shape[1]), lambda bi, j: (bi, 0)),
                  rows3(w_in_b), rows3(w_out_b)],
        out_specs=[pl.BlockSpec((1, s, wd), lambda bi, j: (bi, 0, j))] + [spec for _, spec in weights],
        out_shape=[jax.ShapeDtypeStruct((b, s, d), BF16)]
        + [jax.ShapeDtypeStruct(shape, BF16) for shape, _ in weights],
        scratch_shapes=[pltpu.VMEM((np_, BIAS_ROWS, s), BF16), pltpu.VMEM((np_, s, 256), BF16)],
        compiler_params=pltpu.CompilerParams(
            dimension_semantics=("parallel", "arbitrary"), vmem_limit_bytes=VMEM_LIMIT),
        name="fox_attention",
    )(q, kT, v, cT, w_out_a, w_kv, w_in_b, w_out_b)


def _layer_b_kernel(sinks_ref, o_ref, sg_ref, x_ref, wo_ref, gkv_ref, gb_ref, wkvT_ref,
                    gkn_ref, wqT_ref, wgT_ref, gqn_ref, pos_ref, invf_ref, woT_ref,
                    out_ref, kext_ref, vext_ref, qT_ref, sgT_ref, oT_ref):
    tm = x_ref.shape[1]
    w = WINDOW
    nq = GROUP * w
    first_tile = pl.program_id(1) == 0

    lane_head = lax.shift_right_logical(
        lax.broadcasted_iota(jnp.int32, (2 * w, KV_WIDTH), 1), HEAD_DIM.bit_length() - 1)
    q_head = lax.shift_right_logical(lax.broadcasted_iota(jnp.int32, (1, nq), 1), w.bit_length() - 1)
    sinks = []
    for g in range(N_KV_HEADS):
        sink = jnp.zeros((1, nq), F32)
        for i in range(GROUP):
            sink = jnp.where(q_head == i, sinks_ref[g * GROUP + i] * LOG2E, sink)
        sinks.append(sink)
    key = lax.broadcasted_iota(jnp.int32, (w, nq), 0)
    qry = jnp.bitwise_and(lax.broadcasted_iota(jnp.int32, (w, nq), 1), w - 1)
    own = key <= qry
    ones_rows = jnp.ones((SUM_ROWS, 2 * w), BF16)

    parts = [slice(j * SUB_TILE, (j + 1) * SUB_TILE) for j in range(tm // SUB_TILE)]
    for rows in parts:
        og = (o_ref[0, rows, :].astype(F32) * sg_ref[0, rows, :].astype(F32)).astype(BF16)
        out_ref[0, rows, :] = x_ref[0, rows, :] + _dot(og, wo_ref[...])

    @pl.when(first_tile)
    def _():
        kext_ref[0:w, :] = jnp.zeros((w, KV_WIDTH), BF16)
        vext_ref[:, 0:w] = jnp.zeros((KV_WIDTH, w), BF16)

    @pl.when(jnp.logical_not(first_tile))
    def _():
        kext_ref[0:w, :] = kext_ref[tm:tm + w, :]
        vext_ref[:, 0:w] = vext_ref[:, tm:tm + w]

    def projection_stages(cols):
        st = {}
        ext = slice(w + cols.start, w + cols.stop)

        def k_stage():
            h = out_ref[0, cols, :]
            ms = jnp.mean(h * h, axis=-1, keepdims=True)
            hn = h * lax.rsqrt(ms + EPS)
            st["u_kv"] = (hn * gkv_ref[...]).astype(BF16)
            st["u_b"] = (hn * gb_ref[...]).astype(BF16)
            ang = invf_ref[...] * pos_ref[:, cols].astype(F32)
            st["cos"] = jnp.cos(ang)
            st["sin"] = jnp.sin(ang)
            kvT = _dot_nt(wkvT_ref[...], st["u_kv"])
            kT = _head_norm_fm(kvT[:KV_WIDTH], gkn_ref[...])
            kext_ref[ext, :] = _rope_fm(kT, st["cos"], st["sin"]).T.astype(BF16)
            vext_ref[:, ext] = kvT[KV_WIDTH:].astype(BF16)

        def q_stage():
            qT = _head_norm_fm(_dot_nt(wqT_ref[...], st["u_b"]), gqn_ref[...])
            qT_ref[:, cols] = _rope_fm(qT, st["cos"], st["sin"]).astype(BF16)

        def gate_stage():
            gateT = _dot_nt(wgT_ref[...], st["u_b"])
            sgT_ref[:, cols] = (gateT * jax.nn.sigmoid(gateT)).astype(BF16)

        return [k_stage, q_stage, gate_stage]

    def logits(item):
        g, n = item
        k2 = kext_ref[n * w:(n + 2) * w, :]
        k_band = jnp.where(lane_head == g, k2, jnp.zeros_like(k2))
        q_rep = jnp.concatenate(
            [jnp.concatenate(
                [qT_ref[(g * GROUP + i) * HEAD_DIM:(g * GROUP + i + 1) * HEAD_DIM, n * w:(n + 1) * w]]
                * N_KV_HEADS, axis=0) for i in range(GROUP)], axis=1)
        s_all = _dot(k_band, q_rep)
        s_prev = s_all[:w]
        if n == 0:
            s_prev = jnp.where(first_tile, NEG, s_prev)
        return jnp.where(own, s_all[w:], s_prev)

    items = [(g, n) for n in range(tm // w) for g in range(N_KV_HEADS)]
    items_per_part = (SUB_TILE // w) * N_KV_HEADS
    pending = {}

    def attend(idx):
        ready_end = (idx // items_per_part + 1) * items_per_part
        if idx % items_per_part >= Q_STAGE_SLOT:
            ready_end = min(ready_end + items_per_part, len(items))
        for ahead in range(idx, min(idx + SWA_LOOKAHEAD + 1, ready_end)):
            if ahead not in pending:
                pending[ahead] = logits(items[ahead])
        g, n = items[idx]
        sT = pending.pop(idx)
        m = jnp.maximum(jnp.max(sT, axis=0, keepdims=True), sinks[g])
        p = jnp.exp2(sT - m).astype(BF16)
        zero = jnp.zeros_like(p)
        p2 = jnp.concatenate([jnp.where(own, zero, p), jnp.where(own, p, zero)], axis=0)
        v_ext = jnp.concatenate(
            [vext_ref[g * HEAD_DIM:(g + 1) * HEAD_DIM, n * w:(n + 2) * w], ones_rows], axis=0)
        out_ext = _dot(v_ext, p2)
        denom = out_ext[HEAD_DIM:HEAD_DIM + 1] + jnp.exp2(sinks[g] - m)
        outT = out_ext[:HEAD_DIM] * (1.0 / denom)
        for i in range(GROUP):
            r0 = (g * GROUP + i) * HEAD_DIM
            oT_ref[r0:r0 + HEAD_DIM, n * w:(n + 1) * w] = outT[:, i * w:(i + 1) * w].astype(BF16)

    def output_projection(cols):
        ogT = (oT_ref[:, cols].astype(F32) * sgT_ref[:, cols].astype(F32)).astype(BF16)
        half = D_MODEL // 2
        outT = jnp.concatenate(
            [_dot(woT_ref[:half, :], ogT), _dot(woT_ref[half:, :], ogT)], axis=0)
        out_ref[0, cols, :] = out_ref[0, cols, :] + outT.T

    stages = [projection_stages(cols) for cols in parts]
    for stage in stages[0]:
        stage()
    for j, cols in enumerate(parts):
        upcoming = stages[j + 1] if j + 1 < len(parts) else []
        for k in range(items_per_part):
            if k % STAGE_SPACING == 0 and k // STAGE_SPACING < len(upcoming):
                upcoming[k // STAGE_SPACING]()
            attend(j * items_per_part + k)
        output_projection(cols)


def _layer_b(sinks, o, sg, x, wo, gkv, gb, wkvT, gkn, wqT, wgT, gqn, pos, invf, woT):
    b, s, d = x.shape
    tm = ROW_TILE
    full = lambda a: pl.BlockSpec(a.shape, lambda bi, i: (0,) * a.ndim)
    tok = pl.BlockSpec((1, tm, d), lambda bi, i: (bi, i, 0))
    return pl.pallas_call(
        _layer_b_kernel,
        grid=(b, s // tm),
        in_specs=[pl.BlockSpec(memory_space=pltpu.SMEM),
                  tok, tok, tok, full(wo), full(gkv), full(gb), full(wkvT), full(gkn),
                  full(wqT), full(wgT), full(gqn),
                  pl.BlockSpec((1, tm), lambda bi, i: (0, i)), full(invf), full(woT)],
        out_specs=tok,
        out_shape=jax.ShapeDtypeStruct((b, s, d), F32),
        scratch_shapes=[pltpu.VMEM((tm + WINDOW, KV_WIDTH), BF16),
                        pltpu.VMEM((KV_WIDTH, tm + WINDOW), BF16),
                        pltpu.VMEM((d, tm), BF16),
                        pltpu.VMEM((d, tm), BF16),
                        pltpu.VMEM((d, tm), BF16)],
        compiler_params=pltpu.CompilerParams(
            dimension_semantics=("parallel", "arbitrary"), vmem_limit_bytes=VMEM_LIMIT),
        name="layer_b",
    )(sinks, o, sg, x, wo, gkv, gb, wkvT, gkn, wqT, wgT, gqn, pos, invf, woT)


def _gain_col(g, n_heads, scale=1.0):
    return (jnp.tile(g.astype(F32), n_heads) * scale).reshape(n_heads * HEAD_DIM, 1)


def kernel(x, positions, norm_a_g, w_in_a, b_forget, qnorm_a_g, knorm_a_g, w_out_a, kv_norm_g, w_kv,
           knorm_b_g, norm_b_g, w_in_b, qnorm_b_g, sinks, w_out_b):
    b, s, d = x.shape
    wqkfT, wvg = _prep_weights_a(w_in_a[0].T)
    q, kT, v, sg, cT = _inproj_a(
        x, norm_a_g[0].reshape(1, d), wqkfT, wvg, b_forget[0].reshape(N_HEADS, 1),
        _gain_col(qnorm_a_g[0], N_HEADS, SCALE * LOG2E), _gain_col(knorm_a_g[0], N_HEADS))
    o, wo_a, wkvT, wqT, wgT, woT_b = _fox_attention(q, kT, v, cT, w_out_a, w_kv, w_in_b, w_out_b)

    inv_freq = jnp.power(jnp.float32(ROPE_THETA),
                         -jnp.arange(0, ROT_DIM, 2, dtype=F32) / ROT_DIM).reshape(ROT_HALF, 1)
    return _layer_b(
        sinks[0], o, sg, x, wo_a, kv_norm_g.reshape(1, d), norm_b_g[0].reshape(1, d),
        wkvT, _gain_col(knorm_b_g, N_KV_HEADS),
        wqT, wgT, _gain_col(qnorm_b_g[0], N_HEADS, SCALE * LOG2E),
        positions.reshape(1, s), inv_freq, woT_b)
```

```python
import jax
import jax.numpy as jnp
from jax import lax
from jax.experimental import pallas as pl
from jax.experimental.pallas import tpu as pltpu

D_MODEL = 1024
HEAD_DIM = 64
N_HEADS = 16
N_KV_HEADS = 4
GROUP = N_HEADS // N_KV_HEADS
KV_WIDTH = N_KV_HEADS * HEAD_DIM
WINDOW = 128
ROT_DIM = HEAD_DIM // 4
ROT_HALF = ROT_DIM // 2
ROPE_THETA = 500000.0
EPS = 1e-6
SCALE = HEAD_DIM ** -0.5
LOG2E = 1.4426950408889634
NEG = -1e30

LANES = 128
PAIR = 2 * HEAD_DIM
assert PAIR == LANES

F32 = jnp.float32
BF16 = jnp.bfloat16

PREP_ROWS = 128
ROW_TILE = 1024
SUB_TILE = 256
FOX_Q_TILE = 256
FOX_PAIRS = 2
FOX_LOOKAHEAD = 2
SWA_LOOKAHEAD = 4
STAGE_SPACING = 2
Q_STAGE_SLOT = STAGE_SPACING
SUM_ROWS = 16
VMEM_LIMIT = 56 * 1024 * 1024


def _dot(a, b):
    return jnp.dot(a, b, preferred_element_type=F32)


def _dot_nt(a, b):
    return lax.dot_general(a, b, (((1,), (1,)), ((), ())), preferred_element_type=F32)


def _col(row_ref, scale=1.0):
    row = row_ref[...]
    n = row.shape[1]
    diag = lax.broadcasted_iota(jnp.int32, (n, n), 0) == lax.broadcasted_iota(jnp.int32, (n, n), 1)
    return jnp.sum(jnp.where(diag, row, 0.0), axis=1, keepdims=True) * scale


def _head_norm_fm(t, gain_col):
    blocks = []
    for h in range(t.shape[0] // HEAD_DIM):
        blk = t[h * HEAD_DIM:(h + 1) * HEAD_DIM, :]
        ms = jnp.mean(blk * blk, axis=0, keepdims=True)
        blocks.append(blk * lax.rsqrt(ms + EPS) * gain_col)
    return jnp.concatenate(blocks, axis=0)


def _rope_fm(t, cos, sin):
    blocks = []
    for h in range(t.shape[0] // HEAD_DIM):
        base = h * HEAD_DIM
        x1 = t[base:base + ROT_HALF, :]
        x2 = t[base + ROT_HALF:base + ROT_DIM, :]
        blocks.append(x1 * cos - x2 * sin)
        blocks.append(x1 * sin + x2 * cos)
        blocks.append(t[base + ROT_DIM:base + HEAD_DIM, :])
    return jnp.concatenate(blocks, axis=0)


def _split3(x):
    hi = x.astype(BF16)
    r1 = x - hi.astype(F32)
    mid = r1.astype(BF16)
    lo = (r1 - mid.astype(F32)).astype(BF16)
    return hi, mid, lo


def _prep_kernel(waT_ref, wqkfT_ref, wvg_ref):
    d = D_MODEL
    wqkfT_ref[:2 * d, :] = waT_ref[:2 * d, :].astype(BF16)
    wqkfT_ref[2 * d:, :] = waT_ref[3 * d:3 * d + N_HEADS, :].astype(BF16)
    wvg_ref[:, :d] = waT_ref[2 * d:3 * d, :].T.astype(BF16)
    wvg_ref[:, d:] = waT_ref[3 * d + N_HEADS:, :].T.astype(BF16)


def _prep_weights_a(w_in_aT):
    d = D_MODEL
    r = PREP_ROWS
    return pl.pallas_call(
        _prep_kernel,
        grid=(d // r,),
        in_specs=[pl.BlockSpec((w_in_aT.shape[0], r), lambda i: (0, i))],
        out_specs=[pl.BlockSpec((2 * d + N_HEADS, r), lambda i: (0, i)),
                   pl.BlockSpec((r, 2 * d), lambda i: (i, 0))],
        out_shape=[jax.ShapeDtypeStruct((2 * d + N_HEADS, d), BF16),
                   jax.ShapeDtypeStruct((d, 2 * d), BF16)],
        compiler_params=pltpu.CompilerParams(
            dimension_semantics=("parallel",), vmem_limit_bytes=VMEM_LIMIT),
        name="prep_weights_a",
    )(w_in_aT)


def _prep_layer_b_rows(woa_ref, wkv_ref, wb_ref, wob_ref, wo_ref, wkvT_ref, wqT_ref, wgT_ref, woT_ref):
    d = D_MODEL
    wo_ref[...] = woa_ref[0].astype(BF16)
    wkvT_ref[...] = wkv_ref[...].T.astype(BF16)
    wb = wb_ref[0]
    wqT_ref[...] = wb[:, :d].T.astype(BF16)
    wgT_ref[...] = wb[:, d:].T.astype(BF16)
    woT_ref[...] = wob_ref[0].T.astype(BF16)


def _inproj_a_kernel(x_ref, g_ref, wqkfT_ref, wvg_ref, bf_ref, gq_ref, gk_ref,
                     q_ref, kT_ref, v_ref, sg_ref, cT_ref, carry_ref):
    tm = x_ref.shape[1]

    @pl.when(pl.program_id(1) == 0)
    def _():
        carry_ref[...] = jnp.zeros_like(carry_ref)

    parts = [slice(j * SUB_TILE, (j + 1) * SUB_TILE) for j in range(tm // SUB_TILE)]
    gq = _col(gq_ref, SCALE * LOG2E)
    gk = _col(gk_ref)
    b_forget = _col(bf_ref)
    u_parts, log_f_parts = [], []
    for rows in parts:
        x = x_ref[0, rows, :]
        ms = jnp.mean(x * x, axis=-1, keepdims=True)
        u32 = x * lax.rsqrt(ms + EPS) * g_ref[...]
        u_parts.append(u32.astype(BF16))
        uT = u32.T.astype(BF16)
        qT = _head_norm_fm(_dot(wqkfT_ref[:D_MODEL, :], uT), gq)
        kfT = _dot(wqkfT_ref[D_MODEL:, :], uT)
        kT = _head_norm_fm(kfT[:D_MODEL], gk)
        q_ref[0, rows, :] = qT.T.astype(BF16)
        kT_ref[0, :, rows] = kT.astype(BF16)
        f = kfT[D_MODEL:] + b_forget
        log_f_parts.append(jnp.minimum(f, 0.0) - jnp.log1p(jnp.exp(-jnp.abs(f))))
    u = jnp.concatenate(u_parts, axis=0)

    gate = _dot(u, wvg_ref[:, D_MODEL:])
    sg_ref[0] = (gate * jax.nn.sigmoid(gate)).astype(BF16)

    row = lax.broadcasted_iota(jnp.int32, (SUB_TILE, SUB_TILE), 0)
    col = lax.broadcasted_iota(jnp.int32, (SUB_TILE, SUB_TILE), 1)
    tri = (row <= col).astype(BF16)
    sums = [_dot(jnp.concatenate(_split3(log_f), axis=0), tri) for log_f in log_f_parts]
    v_ref[0] = _dot(u, wvg_ref[:, :D_MODEL]).astype(BF16)
    carry = carry_ref[:, 0:1]
    for rows, s3 in zip(parts, sums):
        c = (s3[:N_HEADS] + s3[N_HEADS:2 * N_HEADS] + s3[2 * N_HEADS:]) + carry
        cT_ref[0, :, rows] = c * LOG2E
        carry = c[:, SUB_TILE - 1:SUB_TILE]
    carry_ref[...] = jnp.broadcast_to(carry, carry_ref.shape)


def _inproj_a(x, g, wqkfT, wvg, bf, gq, gk):
    b, s, d = x.shape
    tm = ROW_TILE
    full = lambda shape: pl.BlockSpec(shape, lambda bi, i: (0,) * len(shape))
    tok = pl.BlockSpec((1, tm, d), lambda bi, i: (bi, i, 0))
    return pl.pallas_call(
        _inproj_a_kernel,
        grid=(b, s // tm),
        in_specs=[tok, full(g.shape), full(wqkfT.shape), full(wvg.shape),
                  full(bf.shape), full(gq.shape), full(gk.shape)],
        out_specs=[tok,
                   pl.BlockSpec((1, d, tm), lambda bi, i: (bi, 0, i)),
                   tok, tok,
                   pl.BlockSpec((1, N_HEADS, tm), lambda bi, i: (bi, 0, i))],
        out_shape=[jax.ShapeDtypeStruct((b, s, d), BF16),
                   jax.ShapeDtypeStruct((b, d, s), BF16),
                   jax.ShapeDtypeStruct((b, s, d), BF16),
                   jax.ShapeDtypeStruct((b, s, d), BF16),
                   jax.ShapeDtypeStruct((b, N_HEADS, s), F32)],
        scratch_shapes=[pltpu.VMEM((N_HEADS, LANES), F32)],
        compiler_params=pltpu.CompilerParams(
            dimension_semantics=("parallel", "arbitrary"), vmem_limit_bytes=VMEM_LIMIT),
        name="inproj_a",
    )(x, g, wqkfT, wvg, bf, gq, gk)


BIAS_ROWS = 16


def _fox_kernel(q_ref, kT_ref, v_ref, c_ref, woa_ref, wkv_ref, wb_ref, wob_ref,
                o_ref, wo_ref, wkvT_ref, wqT_ref, wgT_ref, woT_ref, bias_ref, vext_ref):
    @pl.when(pl.program_id(1) == 0)
    def _():
        _prep_layer_b_rows(woa_ref, wkv_ref, wb_ref, wob_ref, wo_ref, wkvT_ref, wqT_ref, wgT_ref, woT_ref)

    s_len = q_ref.shape[1]
    n_pairs = q_ref.shape[2] // PAIR
    tq = FOX_Q_TILE
    lane = lax.broadcasted_iota(jnp.int32, (tq, PAIR), 1)
    first = lane < HEAD_DIM
    ones_a = jnp.where(lane < 3, 1.0, 0.0).astype(BF16)
    ones_b = jnp.where(jnp.logical_and(lane >= 3, lane < 6), 1.0, 0.0).astype(BF16)
    row = lax.broadcasted_iota(jnp.int32, (2 * tq, tq), 0) % tq
    col = lax.broadcasted_iota(jnp.int32, (2 * tq, tq), 1)
    causal = col <= row

    brow = lax.broadcasted_iota(jnp.int32, (BIAS_ROWS, s_len), 0)
    for p in range(n_pairs):
        head = 2 * (pl.program_id(1) * n_pairs + p)
        parts = _split3(-c_ref[0, pl.ds(head, 1), :]) + _split3(-c_ref[0, pl.ds(head + 1, 1), :])
        bias = jnp.zeros((BIAS_ROWS, s_len), F32)
        for r, part in enumerate(parts):
            bias = jnp.where(brow == r, part.astype(F32), bias)
        bias_ref[p] = bias.astype(BF16)
        vext_ref[p, :, :PAIR] = v_ref[0, :, p * PAIR:(p + 1) * PAIR]
        vext_ref[p, :, PAIR:] = jnp.ones((s_len, LANES), BF16)

    def rhs(p, c0, c1):
        pad = jnp.zeros((PAIR - BIAS_ROWS, c1 - c0), BF16)
        return jnp.concatenate(
            [kT_ref[0, p * PAIR:(p + 1) * PAIR, c0:c1], bias_ref[p, :, c0:c1], pad], axis=0)

    def logits(item):
        p, qb = item
        r0 = qb * tq
        q2 = q_ref[0, r0:r0 + tq, p * PAIR:(p + 1) * PAIR]
        zero = jnp.zeros_like(q2)
        lhs = jnp.concatenate(
            [jnp.concatenate([jnp.where(first, q2, zero), ones_a], axis=1),
             jnp.concatenate([jnp.where(first, zero, q2), ones_b], axis=1)], axis=0)
        s_diag = jnp.where(causal, _dot(lhs, rhs(p, r0, r0 + tq)), NEG)
        s_off = _dot(lhs, rhs(p, 0, r0)) if qb > 0 else None
        return s_diag, s_off

    order = [(p, qb) for qb in reversed(range(s_len // tq)) for p in range(n_pairs)]
    pending = [logits(item) for item in order[:FOX_LOOKAHEAD]]
    for idx, (p, qb) in enumerate(order):
        r0 = qb * tq
        s_diag, s_off = pending.pop(0)
        if idx + FOX_LOOKAHEAD < len(order):
            pending.append(logits(order[idx + FOX_LOOKAHEAD]))
        m = jnp.max(s_diag, axis=1, keepdims=True)
        if qb > 0:
            m = jnp.maximum(m, jnp.max(s_off, axis=1, keepdims=True))
        pv = _dot(jnp.exp2((s_diag - m).astype(BF16)), vext_ref[p, r0:r0 + tq, :])
        if qb > 0:
            pv = pv + _dot(jnp.exp2((s_off - m).astype(BF16)), vext_ref[p, :r0, :])
        o = pv[:, :PAIR] / pv[:, PAIR:]
        o_ref[0, r0:r0 + tq, p * PAIR:(p + 1) * PAIR] = jnp.where(first, o[:tq], o[tq:]).astype(BF16)


def _fox_attention(q, kT, v, cT, w_out_a, w_kv, w_in_b, w_out_b):
    b, s, d = q.shape
    np_, wd = FOX_PAIRS, FOX_PAIRS * PAIR
    r = d // b
    assert r * b == d and r % LANES == 0
    rows3 = lambda a: pl.BlockSpec((1, r, a.shape[2]), lambda bi, j: (0, bi, 0))
    same = lambda n: pl.BlockSpec((r, n), lambda bi, j: (bi, 0))
    trans = lambda m: pl.BlockSpec((m, r), lambda bi, j: (0, bi))
    weights = [((d, d), same(d)), ((2 * KV_WIDTH, d), trans(2 * KV_WIDTH)),
               ((d, d), trans(d)), ((d, d), trans(d)), ((d, d), trans(d))]
    return pl.pallas_call(
        _fox_kernel,
        grid=(b, d // wd),
        in_specs=[pl.BlockSpec((1, s, wd), lambda bi, j: (bi, 0, j)),
                  pl.BlockSpec((1, wd, s), lambda bi, j: (bi, j, 0)),
                  pl.BlockSpec((1, s, wd), lambda bi, j: (bi, 0, j)),
                  pl.BlockSpec((1, N_HEADS, s), lambda bi, j: (bi, 0, 0)),
                  rows3(w_out_a), pl.BlockSpec((r, w_kv.shape[1]), lambda bi, j: (bi, 0)),
                  rows3(w_in_b), rows3(w_out_b)],
        out_specs=[pl.BlockSpec((1, s, wd), lambda bi, j: (bi, 0, j))] + [spec for _, spec in weights],
        out_shape=[jax.ShapeDtypeStruct((b, s, d), BF16)]
        + [jax.ShapeDtypeStruct(shape, BF16) for shape, _ in weights],
        scratch_shapes=[pltpu.VMEM((np_, BIAS_ROWS, s), BF16), pltpu.VMEM((np_, s, PAIR + LANES), BF16)],
        compiler_params=pltpu.CompilerParams(
            dimension_semantics=("parallel", "arbitrary"), vmem_limit_bytes=VMEM_LIMIT),
        name="fox_attention",
    )(q, kT, v, cT, w_out_a, w_kv, w_in_b, w_out_b)


def _layer_b_kernel(sinks_ref, o_ref, sg_ref, x_ref, wo_ref, gkv_ref, gb_ref, wkvT_ref,
                    gkn_ref, wqT_ref, wgT_ref, gqn_ref, pos_ref, invf_ref, woT_ref,
                    out_ref, kext_ref, vext_ref, qT_ref, sgT_ref, oT_ref):
    tm = x_ref.shape[1]
    w = WINDOW
    nq = GROUP * w
    first_tile = pl.program_id(1) == 0

    lane_head = lax.shift_right_logical(
        lax.broadcasted_iota(jnp.int32, (2 * w, KV_WIDTH), 1), HEAD_DIM.bit_length() - 1)
    q_head = lax.shift_right_logical(lax.broadcasted_iota(jnp.int32, (1, nq), 1), w.bit_length() - 1)
    sinks = []
    for g in range(N_KV_HEADS):
        sink = jnp.zeros((1, nq), F32)
        for i in range(GROUP):
            sink = jnp.where(q_head == i, sinks_ref[g * GROUP + i] * LOG2E, sink)
        sinks.append(sink)
    key = lax.broadcasted_iota(jnp.int32, (w, nq), 0)
    qry = jnp.bitwise_and(lax.broadcasted_iota(jnp.int32, (w, nq), 1), w - 1)
    own = key <= qry
    ones_rows = jnp.ones((SUM_ROWS, 2 * w), BF16)
    gkn = _col(gkn_ref)
    gqn = _col(gqn_ref, SCALE * LOG2E)

    parts = [slice(j * SUB_TILE, (j + 1) * SUB_TILE) for j in range(tm // SUB_TILE)]
    for rows in parts:
        og = (o_ref[0, rows, :].astype(F32) * sg_ref[0, rows, :].astype(F32)).astype(BF16)
        out_ref[0, rows, :] = x_ref[0, rows, :] + _dot(og, wo_ref[...])

    @pl.when(first_tile)
    def _():
        kext_ref[0:w, :] = jnp.zeros((w, KV_WIDTH), BF16)
        vext_ref[:, 0:w] = jnp.zeros((KV_WIDTH, w), BF16)

    @pl.when(jnp.logical_not(first_tile))
    def _():
        kext_ref[0:w, :] = kext_ref[tm:tm + w, :]
        vext_ref[:, 0:w] = vext_ref[:, tm:tm + w]

    def projection_stages(cols):
        st = {}
        ext = slice(w + cols.start, w + cols.stop)

        def k_stage():
            h = out_ref[0, cols, :]
            ms = jnp.mean(h * h, axis=-1, keepdims=True)
            hn = h * lax.rsqrt(ms + EPS)
            st["u_kv"] = (hn * gkv_ref[...]).astype(BF16)
            st["u_b"] = (hn * gb_ref[...]).astype(BF16)
            ang = invf_ref[...] * pos_ref[:, cols].astype(F32)
            st["cos"] = jnp.cos(ang)
            st["sin"] = jnp.sin(ang)
            kvT = _dot_nt(wkvT_ref[...], st["u_kv"])
            kT = _head_norm_fm(kvT[:KV_WIDTH], gkn)
            kext_ref[ext, :] = _rope_fm(kT, st["cos"], st["sin"]).T.astype(BF16)
            vext_ref[:, ext] = kvT[KV_WIDTH:].astype(BF16)

        def q_stage():
            qT = _head_norm_fm(_dot_nt(wqT_ref[...], st["u_b"]), gqn)
            qT_ref[:, cols] = _rope_fm(qT, st["cos"], st["sin"]).astype(BF16)

        def gate_stage():
            gateT = _dot_nt(wgT_ref[...], st["u_b"])
            sgT_ref[:, cols] = (gateT * jax.nn.sigmoid(gateT)).astype(BF16)

        return [k_stage, q_stage, gate_stage]

    def logits(item):
        g, n = item
        k2 = kext_ref[n * w:(n + 2) * w, :]
        k_band = jnp.where(lane_head == g, k2, jnp.zeros_like(k2))
        q_rep = jnp.concatenate(
            [jnp.concatenate(
                [qT_ref[(g * GROUP + i) * HEAD_DIM:(g * GROUP + i + 1) * HEAD_DIM, n * w:(n + 1) * w]]
                * N_KV_HEADS, axis=0) for i in range(GROUP)], axis=1)
        s_all = _dot(k_band, q_rep)
        s_prev = s_all[:w]
        if n == 0:
            s_prev = jnp.where(first_tile, NEG, s_prev)
        return jnp.where(own, s_all[w:], s_prev)

    items = [(g, n) for n in range(tm // w) for g in range(N_KV_HEADS)]
    items_per_part = (SUB_TILE // w) * N_KV_HEADS
    pending = {}

    def attend(idx):
        ready_end = (idx // items_per_part + 1) * items_per_part
        if idx % items_per_part >= Q_STAGE_SLOT:
            ready_end = min(ready_end + items_per_part, len(items))
        for ahead in range(idx, min(idx + SWA_LOOKAHEAD + 1, ready_end)):
            if ahead not in pending:
                pending[ahead] = logits(items[ahead])
        g, n = items[idx]
        sT = pending.pop(idx)
        m = jnp.maximum(jnp.max(sT, axis=0, keepdims=True), sinks[g])
        p = jnp.exp2(sT - m).astype(BF16)
        zero = jnp.zeros_like(p)
        p2 = jnp.concatenate([jnp.where(own, zero, p), jnp.where(own, p, zero)], axis=0)
        v_ext = jnp.concatenate(
            [vext_ref[g * HEAD_DIM:(g + 1) * HEAD_DIM, n * w:(n + 2) * w], ones_rows], axis=0)
        out_ext = _dot(v_ext, p2)
        denom = out_ext[HEAD_DIM:HEAD_DIM + 1] + jnp.exp2(sinks[g] - m)
        outT = out_ext[:HEAD_DIM] * (1.0 / denom)
        for i in range(GROUP):
            r0 = (g * GROUP + i) * HEAD_DIM
            oT_ref[r0:r0 + HEAD_DIM, n * w:(n + 1) * w] = outT[:, i * w:(i + 1) * w].astype(BF16)

    def output_projection(cols):
        ogT = (oT_ref[:, cols].astype(F32) * sgT_ref[:, cols].astype(F32)).astype(BF16)
        half = D_MODEL // 2
        outT = jnp.concatenate(
            [_dot(woT_ref[:half, :], ogT), _dot(woT_ref[half:, :], ogT)], axis=0)
        out_ref[0, cols, :] = out_ref[0, cols, :] + outT.T

    stages = [projection_stages(cols) for cols in parts]
    for stage in stages[0]:
        stage()
    for j, cols in enumerate(parts):
        upcoming = stages[j + 1] if j + 1 < len(parts) else []
        for k in range(items_per_part):
            if k % STAGE_SPACING == 0 and k // STAGE_SPACING < len(upcoming):
                upcoming[k // STAGE_SPACING]()
            attend(j * items_per_part + k)
        output_projection(cols)


def _layer_b(sinks, o, sg, x, wo, gkv, gb, wkvT, gkn, wqT, wgT, gqn, pos, invf, woT):
    b, s, d = x.shape
    tm = ROW_TILE
    full = lambda a: pl.BlockSpec(a.shape, lambda bi, i: (0,) * a.ndim)
    tok = pl.BlockSpec((1, tm, d), lambda bi, i: (bi, i, 0))
    return pl.pallas_call(
        _layer_b_kernel,
        grid=(b, s // tm),
        in_specs=[pl.BlockSpec(memory_space=pltpu.SMEM),
                  tok, tok, tok, full(wo), full(gkv), full(gb), full(wkvT), full(gkn),
                  full(wqT), full(wgT), full(gqn),
                  pl.BlockSpec((1, tm), lambda bi, i: (0, i)), full(invf), full(woT)],
        out_specs=tok,
        out_shape=jax.ShapeDtypeStruct((b, s, d), F32),
        scratch_shapes=[pltpu.VMEM((tm + WINDOW, KV_WIDTH), BF16),
                        pltpu.VMEM((KV_WIDTH, tm + WINDOW), BF16),
                        pltpu.VMEM((d, tm), BF16),
                        pltpu.VMEM((d, tm), BF16),
                        pltpu.VMEM((d, tm), BF16)],
        compiler_params=pltpu.CompilerParams(
            dimension_semantics=("parallel", "arbitrary"), vmem_limit_bytes=VMEM_LIMIT),
        name="layer_b",
    )(sinks, o, sg, x, wo, gkv, gb, wkvT, gkn, wqT, wgT, gqn, pos, invf, woT)


def kernel(x, positions, norm_a_g, w_in_a, b_forget, qnorm_a_g, knorm_a_g, w_out_a, kv_norm_g, w_kv,
           knorm_b_g, norm_b_g, w_in_b, qnorm_b_g, sinks, w_out_b):
    b, s, d = x.shape
    wqkfT, wvg = _prep_weights_a(w_in_a[0].T)
    q, kT, v, sg, cT = _inproj_a(
        x, norm_a_g[0].reshape(1, d), wqkfT, wvg, b_forget[0].reshape(1, N_HEADS),
        qnorm_a_g[0].reshape(1, HEAD_DIM), knorm_a_g[0].reshape(1, HEAD_DIM))
    o, wo_a, wkvT, wqT, wgT, woT_b = _fox_attention(q, kT, v, cT, w_out_a, w_kv, w_in_b, w_out_b)

    inv_freq = jnp.power(jnp.float32(ROPE_THETA),
                         -jnp.arange(0, ROT_DIM, 2, dtype=F32) / ROT_DIM).reshape(ROT_HALF, 1)
    return _layer_b(
        sinks[0], o, sg, x, wo_a, kv_norm_g.reshape(1, d), norm_b_g[0].reshape(1, d),
        wkvT, knorm_b_g.reshape(1, HEAD_DIM),
        wqT, wgT, qnorm_b_g[0].reshape(1, HEAD_DIM),
        positions.reshape(1, s), inv_freq, woT_b)
```

```python
import jax
import jax.numpy as jnp
from jax import lax
from jax.experimental import pallas as pl
from jax.experimental.pallas import tpu as pltpu

D_MODEL = 1024
HEAD_DIM = 64
N_HEADS = 16
N_KV_HEADS = 4
GROUP = N_HEADS // N_KV_HEADS
KV_WIDTH = N_KV_HEADS * HEAD_DIM
WINDOW = 128
ROT_DIM = HEAD_DIM // 4
ROT_HALF = ROT_DIM // 2
ROPE_THETA = 500000.0
EPS = 1e-6
SCALE = HEAD_DIM ** -0.5
LOG2E = 1.4426950408889634
NEG = -1e30

LANES = 128
PAIR = 2 * HEAD_DIM
assert PAIR == LANES

F32 = jnp.float32
BF16 = jnp.bfloat16

PREP_ROWS = 128
ROW_TILE = 1024
SUB_TILE = 256
FOX_Q_TILE = 256
FOX_PAIRS = 4
FOX_LOOKAHEAD = 2
SWA_LOOKAHEAD = 4
STAGE_SPACING = 2
Q_STAGE_SLOT = STAGE_SPACING
SUM_ROWS = 16
VMEM_LIMIT = 56 * 1024 * 1024


def _dot(a, b):
    return jnp.dot(a, b, preferred_element_type=F32)


def _dot_nt(a, b):
    return lax.dot_general(a, b, (((1,), (1,)), ((), ())), preferred_element_type=F32)


def _col(row_ref, scale=1.0):
    row = row_ref[...]
    n = row.shape[1]
    diag = lax.broadcasted_iota(jnp.int32, (n, n), 0) == lax.broadcasted_iota(jnp.int32, (n, n), 1)
    return jnp.sum(jnp.where(diag, row, 0.0), axis=1, keepdims=True) * scale


def _head_norm_fm(t, gain_col):
    blocks = []
    for h in range(t.shape[0] // HEAD_DIM):
        blk = t[h * HEAD_DIM:(h + 1) * HEAD_DIM, :]
        ms = jnp.mean(blk * blk, axis=0, keepdims=True)
        blocks.append(blk * lax.rsqrt(ms + EPS) * gain_col)
    return jnp.concatenate(blocks, axis=0)


def _rope_fm(t, cos, sin):
    blocks = []
    for h in range(t.shape[0] // HEAD_DIM):
        base = h * HEAD_DIM
        x1 = t[base:base + ROT_HALF, :]
        x2 = t[base + ROT_HALF:base + ROT_DIM, :]
        blocks.append(x1 * cos - x2 * sin)
        blocks.append(x1 * sin + x2 * cos)
        blocks.append(t[base + ROT_DIM:base + HEAD_DIM, :])
    return jnp.concatenate(blocks, axis=0)


def _split3(x):
    hi = x.astype(BF16)
    r1 = x - hi.astype(F32)
    mid = r1.astype(BF16)
    lo = (r1 - mid.astype(F32)).astype(BF16)
    return hi, mid, lo


def _prep_kernel(waT_ref, wqkfT_ref, wvg_ref):
    d = D_MODEL
    wqkfT_ref[:2 * d, :] = waT_ref[:2 * d, :].astype(BF16)
    wqkfT_ref[2 * d:, :] = waT_ref[3 * d:3 * d + N_HEADS, :].astype(BF16)
    wvg_ref[:, :d] = waT_ref[2 * d:3 * d, :].T.astype(BF16)
    wvg_ref[:, d:] = waT_ref[3 * d + N_HEADS:, :].T.astype(BF16)


def _prep_weights_a(w_in_aT):
    d = D_MODEL
    r = PREP_ROWS
    return pl.pallas_call(
        _prep_kernel,
        grid=(d // r,),
        in_specs=[pl.BlockSpec((w_in_aT.shape[0], r), lambda i: (0, i))],
        out_specs=[pl.BlockSpec((2 * d + N_HEADS, r), lambda i: (0, i)),
                   pl.BlockSpec((r, 2 * d), lambda i: (i, 0))],
        out_shape=[jax.ShapeDtypeStruct((2 * d + N_HEADS, d), BF16),
                   jax.ShapeDtypeStruct((d, 2 * d), BF16)],
        compiler_params=pltpu.CompilerParams(
            dimension_semantics=("parallel",), vmem_limit_bytes=VMEM_LIMIT),
        name="prep_weights_a",
    )(w_in_aT)


def _prep_layer_b_rows(woa_ref, wkv_ref, wb_ref, wob_ref, wo_ref, wkvT_ref, wqT_ref, wgT_ref, woT_ref):
    d = D_MODEL
    wo_ref[...] = woa_ref[0].astype(BF16)
    wkvT_ref[...] = wkv_ref[...].T.astype(BF16)
    wb = wb_ref[0]
    wqT_ref[...] = wb[:, :d].T.astype(BF16)
    wgT_ref[...] = wb[:, d:].T.astype(BF16)
    woT_ref[...] = wob_ref[0].T.astype(BF16)


def _inproj_a_kernel(x_ref, g_ref, wqkfT_ref, wvg_ref, bf_ref, gq_ref, gk_ref,
                     q_ref, kT_ref, v_ref, sg_ref, cT_ref, carry_ref):
    tm = x_ref.shape[1]

    @pl.when(pl.program_id(1) == 0)
    def _():
        carry_ref[...] = jnp.zeros_like(carry_ref)

    parts = [slice(j * SUB_TILE, (j + 1) * SUB_TILE) for j in range(tm // SUB_TILE)]
    gq = _col(gq_ref, SCALE * LOG2E)
    gk = _col(gk_ref)
    b_forget = _col(bf_ref)
    u_parts, log_f_parts = [], []
    for rows in parts:
        x = x_ref[0, rows, :]
        ms = jnp.mean(x * x, axis=-1, keepdims=True)
        u32 = x * lax.rsqrt(ms + EPS) * g_ref[...]
        u_parts.append(u32.astype(BF16))
        uT = u32.T.astype(BF16)
        qT = _head_norm_fm(_dot(wqkfT_ref[:D_MODEL, :], uT), gq)
        kfT = _dot(wqkfT_ref[D_MODEL:, :], uT)
        kT = _head_norm_fm(kfT[:D_MODEL], gk)
        q_ref[0, rows, :] = qT.T.astype(BF16)
        kT_ref[0, :, rows] = kT.astype(BF16)
        f = kfT[D_MODEL:] + b_forget
        log_f_parts.append(jnp.minimum(f, 0.0) - jnp.log1p(jnp.exp(-jnp.abs(f))))
    u = jnp.concatenate(u_parts, axis=0)

    gate = _dot(u, wvg_ref[:, D_MODEL:])
    sg_ref[0] = (gate * jax.nn.sigmoid(gate)).astype(BF16)

    row = lax.broadcasted_iota(jnp.int32, (SUB_TILE, SUB_TILE), 0)
    col = lax.broadcasted_iota(jnp.int32, (SUB_TILE, SUB_TILE), 1)
    tri = (row <= col).astype(BF16)
    sums = [_dot(jnp.concatenate(_split3(log_f), axis=0), tri) for log_f in log_f_parts]
    v_ref[0] = _dot(u, wvg_ref[:, :D_MODEL]).astype(BF16)
    carry = carry_ref[:, 0:1]
    for rows, s3 in zip(parts, sums):
        c = (s3[:N_HEADS] + s3[N_HEADS:2 * N_HEADS] + s3[2 * N_HEADS:]) + carry
        cT_ref[0, :, rows] = c * LOG2E
        carry = c[:, SUB_TILE - 1:SUB_TILE]
    carry_ref[...] = jnp.broadcast_to(carry, carry_ref.shape)


def _inproj_a(x, g, wqkfT, wvg, bf, gq, gk):
    b, s, d = x.shape
    tm = ROW_TILE
    full = lambda shape: pl.BlockSpec(shape, lambda bi, i: (0,) * len(shape))
    tok = pl.BlockSpec((1, tm, d), lambda bi, i: (bi, i, 0))
    return pl.pallas_call(
        _inproj_a_kernel,
        grid=(b, s // tm),
        in_specs=[tok, full(g.shape), full(wqkfT.shape), full(wvg.shape),
                  full(bf.shape), full(gq.shape), full(gk.shape)],
        out_specs=[tok,
                   pl.BlockSpec((1, d, tm), lambda bi, i: (bi, 0, i)),
                   tok, tok,
                   pl.BlockSpec((1, N_HEADS, tm), lambda bi, i: (bi, 0, i))],
        out_shape=[jax.ShapeDtypeStruct((b, s, d), BF16),
                   jax.ShapeDtypeStruct((b, d, s), BF16),
                   jax.ShapeDtypeStruct((b, s, d), BF16),
                   jax.ShapeDtypeStruct((b, s, d), BF16),
                   jax.ShapeDtypeStruct((b, N_HEADS, s), F32)],
        scratch_shapes=[pltpu.VMEM((N_HEADS, LANES), F32)],
        compiler_params=pltpu.CompilerParams(
            dimension_semantics=("parallel", "arbitrary"), vmem_limit_bytes=VMEM_LIMIT),
        name="inproj_a",
    )(x, g, wqkfT, wvg, bf, gq, gk)


BIAS_ROWS = 16


def _fox_kernel(q_ref, kT_ref, v_ref, c_ref, woa_ref, wkv_ref, wb_ref, wob_ref,
                o_ref, wo_ref, wkvT_ref, wqT_ref, wgT_ref, woT_ref, bias_ref, vext_ref):
    @pl.when(pl.program_id(1) == 0)
    def _():
        _prep_layer_b_rows(woa_ref, wkv_ref, wb_ref, wob_ref, wo_ref, wkvT_ref, wqT_ref, wgT_ref, woT_ref)

    s_len = q_ref.shape[1]
    n_pairs = q_ref.shape[2] // PAIR
    tq = FOX_Q_TILE
    lane = lax.broadcasted_iota(jnp.int32, (tq, PAIR), 1)
    first = lane < HEAD_DIM
    ones_a = jnp.where(lane < 3, 1.0, 0.0).astype(BF16)
    ones_b = jnp.where(jnp.logical_and(lane >= 3, lane < 6), 1.0, 0.0).astype(BF16)
    row = lax.broadcasted_iota(jnp.int32, (2 * tq, tq), 0) % tq
    col = lax.broadcasted_iota(jnp.int32, (2 * tq, tq), 1)
    causal = col <= row

    brow = lax.broadcasted_iota(jnp.int32, (BIAS_ROWS, s_len), 0)
    for p in range(n_pairs):
        head = 2 * (pl.program_id(1) * n_pairs + p)
        parts = _split3(-c_ref[0, pl.ds(head, 1), :]) + _split3(-c_ref[0, pl.ds(head + 1, 1), :])
        bias = jnp.zeros((BIAS_ROWS, s_len), F32)
        for r, part in enumerate(parts):
            bias = jnp.where(brow == r, part.astype(F32), bias)
        bias_ref[p] = bias.astype(BF16)
        vext_ref[p, :, :PAIR] = v_ref[0, :, p * PAIR:(p + 1) * PAIR]
        vext_ref[p, :, PAIR:] = jnp.ones((s_len, LANES), BF16)

    def rhs(p, c0, c1):
        pad = jnp.zeros((PAIR - BIAS_ROWS, c1 - c0), BF16)
        return jnp.concatenate(
            [kT_ref[0, p * PAIR:(p + 1) * PAIR, c0:c1], bias_ref[p, :, c0:c1], pad], axis=0)

    def logits(item):
        p, qb = item
        r0 = qb * tq
        q2 = q_ref[0, r0:r0 + tq, p * PAIR:(p + 1) * PAIR]
        zero = jnp.zeros_like(q2)
        lhs = jnp.concatenate(
            [jnp.concatenate([jnp.where(first, q2, zero), ones_a], axis=1),
             jnp.concatenate([jnp.where(first, zero, q2), ones_b], axis=1)], axis=0)
        s_diag = jnp.where(causal, _dot(lhs, rhs(p, r0, r0 + tq)), NEG)
        s_off = _dot(lhs, rhs(p, 0, r0)) if qb > 0 else None
        return s_diag, s_off

    order = [(p, qb) for qb in reversed(range(s_len // tq)) for p in range(n_pairs)]
    pending = [logits(item) for item in order[:FOX_LOOKAHEAD]]
    for idx, (p, qb) in enumerate(order):
        r0 = qb * tq
        s_diag, s_off = pending.pop(0)
        if idx + FOX_LOOKAHEAD < len(order):
            pending.append(logits(order[idx + FOX_LOOKAHEAD]))
        m = jnp.max(s_diag, axis=1, keepdims=True)
        if qb > 0:
            m = jnp.maximum(m, jnp.max(s_off, axis=1, keepdims=True))
        pv = _dot(jnp.exp2((s_diag - m).astype(BF16)), vext_ref[p, r0:r0 + tq, :])
        if qb > 0:
            pv = pv + _dot(jnp.exp2((s_off - m).astype(BF16)), vext_ref[p, :r0, :])
        o = pv[:, :PAIR] / pv[:, PAIR:]
        o_ref[0, r0:r0 + tq, p * PAIR:(p + 1) * PAIR] = jnp.where(first, o[:tq], o[tq:]).astype(BF16)


def _fox_attention(q, kT, v, cT, w_out_a, w_kv, w_in_b, w_out_b):
    b, s, d = q.shape
    np_, wd = FOX_PAIRS, FOX_PAIRS * PAIR
    r = d // b
    assert r * b == d and r % LANES == 0
    rows3 = lambda a: pl.BlockSpec((1, r, a.shape[2]), lambda bi, j: (0, bi, 0))
    same = lambda n: pl.BlockSpec((r, n), lambda bi, j: (bi, 0))
    trans = lambda m: pl.BlockSpec((m, r), lambda bi, j: (0, bi))
    weights = [((d, d), same(d)), ((2 * KV_WIDTH, d), trans(2 * KV_WIDTH)),
               ((d, d), trans(d)), ((d, d), trans(d)), ((d, d), trans(d))]
    return pl.pallas_call(
        _fox_kernel,
        grid=(b, d // wd),
        in_specs=[pl.BlockSpec((1, s, wd), lambda bi, j: (bi, 0, j)),
                  pl.BlockSpec((1, wd, s), lambda bi, j: (bi, j, 0)),
                  pl.BlockSpec((1, s, wd), lambda bi, j: (bi, 0, j)),
                  pl.BlockSpec((1, N_HEADS, s), lambda bi, j: (bi, 0, 0)),
                  rows3(w_out_a), pl.BlockSpec((r, w_kv.shape[1]), lambda bi, j: (bi, 0)),
                  rows3(w_in_b), rows3(w_out_b)],
        out_specs=[pl.BlockSpec((1, s, wd), lambda bi, j: (bi, 0, j))] + [spec for _, spec in weights],
        out_shape=[jax.ShapeDtypeStruct((b, s, d), BF16)]
        + [jax.ShapeDtypeStruct(shape, BF16) for shape, _ in weights],
        scratch_shapes=[pltpu.VMEM((np_, BIAS_ROWS, s), BF16), pltpu.VMEM((np_, s, PAIR + LANES), BF16)],
        compiler_params=pltpu.CompilerParams(
            dimension_semantics=("parallel", "arbitrary"), vmem_limit_bytes=VMEM_LIMIT),
        name="fox_attention",
    )(q, kT, v, cT, w_out_a, w_kv, w_in_b, w_out_b)


def _layer_b_kernel(sinks_ref, o_ref, sg_ref, x_ref, wo_ref, gkv_ref, gb_ref, wkvT_ref,
                    gkn_ref, wqT_ref, wgT_ref, gqn_ref, pos_ref, invf_ref, woT_ref,
                    out_ref, kext_ref, vext_ref, qT_ref, sgT_ref, oT_ref):
    tm = x_ref.shape[1]
    w = WINDOW
    nq = GROUP * w
    first_tile = pl.program_id(1) == 0

    lane_head = lax.shift_right_logical(
        lax.broadcasted_iota(jnp.int32, (2 * w, KV_WIDTH), 1), HEAD_DIM.bit_length() - 1)
    q_head = lax.shift_right_logical(lax.broadcasted_iota(jnp.int32, (1, nq), 1), w.bit_length() - 1)
    sinks = []
    for g in range(N_KV_HEADS):
        sink = jnp.zeros((1, nq), F32)
        for i in range(GROUP):
            sink = jnp.where(q_head == i, sinks_ref[g * GROUP + i] * LOG2E, sink)
        sinks.append(sink)
    key = lax.broadcasted_iota(jnp.int32, (w, nq), 0)
    qry = jnp.bitwise_and(lax.broadcasted_iota(jnp.int32, (w, nq), 1), w - 1)
    own = key <= qry
    ones_rows = jnp.ones((SUM_ROWS, 2 * w), BF16)
    gkn = _col(gkn_ref)
    gqn = _col(gqn_ref, SCALE * LOG2E)

    parts = [slice(j * SUB_TILE, (j + 1) * SUB_TILE) for j in range(tm // SUB_TILE)]
    for rows in parts:
        og = (o_ref[0, rows, :].astype(F32) * sg_ref[0, rows, :].astype(F32)).astype(BF16)
        out_ref[0, rows, :] = x_ref[0, rows, :] + _dot(og, wo_ref[...])

    @pl.when(first_tile)
    def _():
        kext_ref[0:w, :] = jnp.zeros((w, KV_WIDTH), BF16)
        vext_ref[:, 0:w] = jnp.zeros((KV_WIDTH, w), BF16)

    @pl.when(jnp.logical_not(first_tile))
    def _():
        kext_ref[0:w, :] = kext_ref[tm:tm + w, :]
        vext_ref[:, 0:w] = vext_ref[:, tm:tm + w]

    def projection_stages(cols):
        st = {}
        ext = slice(w + cols.start, w + cols.stop)

        def k_stage():
            h = out_ref[0, cols, :]
            ms = jnp.mean(h * h, axis=-1, keepdims=True)
            hn = h * lax.rsqrt(ms + EPS)
            st["u_kv"] = (hn * gkv_ref[...]).astype(BF16)
            st["u_b"] = (hn * gb_ref[...]).astype(BF16)
            ang = invf_ref[...] * pos_ref[:, cols].astype(F32)
            st["cos"] = jnp.cos(ang)
            st["sin"] = jnp.sin(ang)
            kvT = _dot_nt(wkvT_ref[...], st["u_kv"])
            kT = _head_norm_fm(kvT[:KV_WIDTH], gkn)
            kext_ref[ext, :] = _rope_fm(kT, st["cos"], st["sin"]).T.astype(BF16)
            vext_ref[:, ext] = kvT[KV_WIDTH:].astype(BF16)

        def q_stage():
            qT = _head_norm_fm(_dot_nt(wqT_ref[...], st["u_b"]), gqn)
            qT_ref[:, cols] = _rope_fm(qT, st["cos"], st["sin"]).astype(BF16)

        def gate_stage():
            gateT = _dot_nt(wgT_ref[...], st["u_b"])
            sgT_ref[:, cols] = (gateT * jax.nn.sigmoid(gateT)).astype(BF16)

        return [k_stage, q_stage, gate_stage]

    def logits(item):
        g, n = item
        k2 = kext_ref[n * w:(n + 2) * w, :]
        k_band = jnp.where(lane_head == g, k2, jnp.zeros_like(k2))
        q_rep = jnp.concatenate(
            [jnp.concatenate(
                [qT_ref[(g * GROUP + i) * HEAD_DIM:(g * GROUP + i + 1) * HEAD_DIM, n * w:(n + 1) * w]]
                * N_KV_HEADS, axis=0) for i in range(GROUP)], axis=1)
        s_all = _dot(k_band, q_rep)
        s_prev = s_all[:w]
        if n == 0:
            s_prev = jnp.where(first_tile, NEG, s_prev)
        return jnp.where(own, s_all[w:], s_prev)

    items = [(g, n) for n in range(tm // w) for g in range(N_KV_HEADS)]
    items_per_part = (SUB_TILE // w) * N_KV_HEADS
    pending = {}

    def attend(idx):
        ready_end = (idx // items_per_part + 1) * items_per_part
        if idx % items_per_part >= Q_STAGE_SLOT:
            ready_end = min(ready_end + items_per_part, len(items))
        for ahead in range(idx, min(idx + SWA_LOOKAHEAD + 1, ready_end)):
            if ahead not in pending:
                pending[ahead] = logits(items[ahead])
        g, n = items[idx]
        sT = pending.pop(idx)
        m = jnp.maximum(jnp.max(sT, axis=0, keepdims=True), sinks[g])
        p = jnp.exp2(sT - m).astype(BF16)
        zero = jnp.zeros_like(p)
        p2 = jnp.concatenate([jnp.where(own, zero, p), jnp.where(own, p, zero)], axis=0)
        v_ext = jnp.concatenate(
            [vext_ref[g * HEAD_DIM:(g + 1) * HEAD_DIM, n * w:(n + 2) * w], ones_rows], axis=0)
        out_ext = _dot(v_ext, p2)
        denom = out_ext[HEAD_DIM:HEAD_DIM + 1] + jnp.exp2(sinks[g] - m)
        outT = out_ext[:HEAD_DIM] * (1.0 / denom)
        for i in range(GROUP):
            r0 = (g * GROUP + i) * HEAD_DIM
            oT_ref[r0:r0 + HEAD_DIM, n * w:(n + 1) * w] = outT[:, i * w:(i + 1) * w].astype(BF16)

    def output_projection(cols):
        ogT = (oT_ref[:, cols].astype(F32) * sgT_ref[:, cols].astype(F32)).astype(BF16)
        half = D_MODEL // 2
        outT = jnp.concatenate(
            [_dot(woT_ref[:half, :], ogT), _dot(woT_ref[half:, :], ogT)], axis=0)
        out_ref[0, cols, :] = out_ref[0, cols, :] + outT.T

    stages = [projection_stages(cols) for cols in parts]
    for stage in stages[0]:
        stage()
    for j, cols in enumerate(parts):
        upcoming = stages[j + 1] if j + 1 < len(parts) else []
        for k in range(items_per_part):
            if k % STAGE_SPACING == 0 and k // STAGE_SPACING < len(upcoming):
                upcoming[k // STAGE_SPACING]()
            attend(j * items_per_part + k)
        output_projection(cols)


def _layer_b(sinks, o, sg, x, wo, gkv, gb, wkvT, gkn, wqT, wgT, gqn, pos, invf, woT):
    b, s, d = x.shape
    tm = ROW_TILE
    full = lambda a: pl.BlockSpec(a.shape, lambda bi, i: (0,) * a.ndim)
    tok = pl.BlockSpec((1, tm, d), lambda bi, i: (bi, i, 0))
    return pl.pallas_call(
        _layer_b_kernel,
        grid=(b, s // tm),
        in_specs=[pl.BlockSpec(memory_space=pltpu.SMEM),
                  tok, tok, tok, full(wo), full(gkv), full(gb), full(wkvT), full(gkn),
                  full(wqT), full(wgT), full(gqn),
                  pl.BlockSpec((1, tm), lambda bi, i: (0, i)), full(invf), full(woT)],
        out_specs=tok,
        out_shape=jax.ShapeDtypeStruct((b, s, d), F32),
        scratch_shapes=[pltpu.VMEM((tm + WINDOW, KV_WIDTH), BF16),
                        pltpu.VMEM((KV_WIDTH, tm + WINDOW), BF16),
                        pltpu.VMEM((d, tm), BF16),
                        pltpu.VMEM((d, tm), BF16),
                        pltpu.VMEM((d, tm), BF16)],
        compiler_params=pltpu.CompilerParams(
            dimension_semantics=("parallel", "arbitrary"), vmem_limit_bytes=VMEM_LIMIT),
        name="layer_b",
    )(sinks, o, sg, x, wo, gkv, gb, wkvT, gkn, wqT, wgT, gqn, pos, invf, woT)


def kernel(x, positions, norm_a_g, w_in_a, b_forget, qnorm_a_g, knorm_a_g, w_out_a, kv_norm_g, w_kv,
           knorm_b_g, norm_b_g, w_in_b, qnorm_b_g, sinks, w_out_b):
    b, s, d = x.shape
    wqkfT, wvg = _prep_weights_a(w_in_a[0].T)
    q, kT, v, sg, cT = _inproj_a(
        x, norm_a_g[0].reshape(1, d), wqkfT, wvg, b_forget[0].reshape(1, N_HEADS),
        qnorm_a_g[0].reshape(1, HEAD_DIM), knorm_a_g[0].reshape(1, HEAD_DIM))
    o, wo_a, wkvT, wqT, wgT, woT_b = _fox_attention(q, kT, v, cT, w_out_a, w_kv, w_in_b, w_out_b)

    inv_freq = jnp.power(jnp.float32(ROPE_THETA),
                         -jnp.arange(0, ROT_DIM, 2, dtype=F32) / ROT_DIM).reshape(ROT_HALF, 1)
    return _layer_b(
        sinks[0], o, sg, x, wo_a, kv_norm_g.reshape(1, d), norm_b_g[0].reshape(1, d),
        wkvT, knorm_b_g.reshape(1, HEAD_DIM),
        wqT, wgT, qnorm_b_g[0].reshape(1, HEAD_DIM),
        positions.reshape(1, s), inv_freq, woT_b)
```

```python
import jax
import jax.numpy as jnp
from jax import lax
from jax.experimental import pallas as pl
from jax.experimental.pallas import tpu as pltpu

D_MODEL = 1024
HEAD_DIM = 64
N_HEADS = 16
N_KV_HEADS = 4
GROUP = N_HEADS // N_KV_HEADS
KV_WIDTH = N_KV_HEADS * HEAD_DIM
WINDOW = 128
ROT_DIM = HEAD_DIM // 4
ROT_HALF = ROT_DIM // 2
ROPE_THETA = 500000.0
EPS = 1e-6
SCALE = HEAD_DIM ** -0.5
LOG2E = 1.4426950408889634
NEG = -1e30

LANES = 128
PAIR = 2 * HEAD_DIM
assert PAIR == LANES

F32 = jnp.float32
BF16 = jnp.bfloat16

PREP_ROWS = 128
ROW_TILE = 1024
SUB_TILE = 256
FOX_Q_TILE = 256
FOX_PAIRS = 2
FOX_LOOKAHEAD = 2
SWA_LOOKAHEAD = 4
STAGE_SPACING = 2
Q_STAGE_SLOT = STAGE_SPACING
SUM_ROWS = 16
VMEM_LIMIT = 56 * 1024 * 1024


def _dot(a, b):
    return jnp.dot(a, b, preferred_element_type=F32)


def _dot_nt(a, b):
    return lax.dot_general(a, b, (((1,), (1,)), ((), ())), preferred_element_type=F32)


def _col(row_ref, scale=1.0):
    row = row_ref[...]
    n = row.shape[1]
    diag = lax.broadcasted_iota(jnp.int32, (n, n), 0) == lax.broadcasted_iota(jnp.int32, (n, n), 1)
    return jnp.sum(jnp.where(diag, row, 0.0), axis=1, keepdims=True) * scale


def _head_norm_fm(t, gain_col):
    blocks = []
    for h in range(t.shape[0] // HEAD_DIM):
        blk = t[h * HEAD_DIM:(h + 1) * HEAD_DIM, :]
        ms = jnp.mean(blk * blk, axis=0, keepdims=True)
        blocks.append(blk * lax.rsqrt(ms + EPS) * gain_col)
    return jnp.concatenate(blocks, axis=0)


def _rope_fm(t, cos, sin):
    blocks = []
    for h in range(t.shape[0] // HEAD_DIM):
        base = h * HEAD_DIM
        x1 = t[base:base + ROT_HALF, :]
        x2 = t[base + ROT_HALF:base + ROT_DIM, :]
        blocks.append(x1 * cos - x2 * sin)
        blocks.append(x1 * sin + x2 * cos)
        blocks.append(t[base + ROT_DIM:base + HEAD_DIM, :])
    return jnp.concatenate(blocks, axis=0)


def _split3(x):
    hi = x.astype(BF16)
    r1 = x - hi.astype(F32)
    mid = r1.astype(BF16)
    lo = (r1 - mid.astype(F32)).astype(BF16)
    return hi, mid, lo


def _prep_kernel(waT_ref, wqkfT_ref, wvg_ref):
    d = D_MODEL
    wqkfT_ref[:2 * d, :] = waT_ref[:2 * d, :].astype(BF16)
    wqkfT_ref[2 * d:, :] = waT_ref[3 * d:3 * d + N_HEADS, :].astype(BF16)
    wvg_ref[:, :d] = waT_ref[2 * d:3 * d, :].T.astype(BF16)
    wvg_ref[:, d:] = waT_ref[3 * d + N_HEADS:, :].T.astype(BF16)


def _prep_weights_a(w_in_aT):
    d = D_MODEL
    r = PREP_ROWS
    return pl.pallas_call(
        _prep_kernel,
        grid=(d // r,),
        in_specs=[pl.BlockSpec((w_in_aT.shape[0], r), lambda i: (0, i))],
        out_specs=[pl.BlockSpec((2 * d + N_HEADS, r), lambda i: (0, i)),
                   pl.BlockSpec((r, 2 * d), lambda i: (i, 0))],
        out_shape=[jax.ShapeDtypeStruct((2 * d + N_HEADS, d), BF16),
                   jax.ShapeDtypeStruct((d, 2 * d), BF16)],
        compiler_params=pltpu.CompilerParams(
            dimension_semantics=("parallel",), vmem_limit_bytes=VMEM_LIMIT),
        name="prep_weights_a",
    )(w_in_aT)


def _prep_layer_b_rows(woa_ref, wkv_ref, wb_ref, wob_ref, wo_ref, wkvT_ref, wqT_ref, wgT_ref, woT_ref):
    d = D_MODEL
    wo_ref[...] = woa_ref[0].astype(BF16)
    wkvT_ref[...] = wkv_ref[...].T.astype(BF16)
    wb = wb_ref[0]
    wqT_ref[...] = wb[:, :d].T.astype(BF16)
    wgT_ref[...] = wb[:, d:].T.astype(BF16)
    woT_ref[...] = wob_ref[0].T.astype(BF16)


def _inproj_a_kernel(x_ref, g_ref, wqkfT_ref, wvg_ref, bf_ref, gq_ref, gk_ref,
                     q_ref, kT_ref, v_ref, sg_ref, cT_ref, carry_ref):
    tm = x_ref.shape[1]

    @pl.when(pl.program_id(1) == 0)
    def _():
        carry_ref[...] = jnp.zeros_like(carry_ref)

    parts = [slice(j * SUB_TILE, (j + 1) * SUB_TILE) for j in range(tm // SUB_TILE)]
    gq = _col(gq_ref, SCALE * LOG2E)
    gk = _col(gk_ref)
    b_forget = _col(bf_ref)
    u_parts, log_f_parts = [], []
    for rows in parts:
        x = x_ref[0, rows, :]
        ms = jnp.mean(x * x, axis=-1, keepdims=True)
        u32 = x * lax.rsqrt(ms + EPS) * g_ref[...]
        u_parts.append(u32.astype(BF16))
        uT = u32.T.astype(BF16)
        qT = _head_norm_fm(_dot(wqkfT_ref[:D_MODEL, :], uT), gq)
        kfT = _dot(wqkfT_ref[D_MODEL:, :], uT)
        kT = _head_norm_fm(kfT[:D_MODEL], gk)
        q_ref[0, rows, :] = qT.T.astype(BF16)
        kT_ref[0, :, rows] = kT.astype(BF16)
        f = kfT[D_MODEL:] + b_forget
        log_f_parts.append(jnp.minimum(f, 0.0) - jnp.log1p(jnp.exp(-jnp.abs(f))))
    u = jnp.concatenate(u_parts, axis=0)

    gate = _dot(u, wvg_ref[:, D_MODEL:])
    sg_ref[0] = (gate * jax.nn.sigmoid(gate)).astype(BF16)

    row = lax.broadcasted_iota(jnp.int32, (SUB_TILE, SUB_TILE), 0)
    col = lax.broadcasted_iota(jnp.int32, (SUB_TILE, SUB_TILE), 1)
    tri = (row <= col).astype(BF16)
    sums = [_dot(jnp.concatenate(_split3(log_f), axis=0), tri) for log_f in log_f_parts]
    v_ref[0] = _dot(u, wvg_ref[:, :D_MODEL]).astype(BF16)
    carry = carry_ref[:, 0:1]
    for rows, s3 in zip(parts, sums):
        c = (s3[:N_HEADS] + s3[N_HEADS:2 * N_HEADS] + s3[2 * N_HEADS:]) + carry
        cT_ref[0, :, rows] = c * LOG2E
        carry = c[:, SUB_TILE - 1:SUB_TILE]
    carry_ref[...] = jnp.broadcast_to(carry, carry_ref.shape)


def _inproj_a(x, g, wqkfT, wvg, bf, gq, gk):
    b, s, d = x.shape
    tm = ROW_TILE
    full = lambda shape: pl.BlockSpec(shape, lambda bi, i: (0,) * len(shape))
    tok = pl.BlockSpec((1, tm, d), lambda bi, i: (bi, i, 0))
    return pl.pallas_call(
        _inproj_a_kernel,
        grid=(b, s // tm),
        in_specs=[tok, full(g.shape), full(wqkfT.shape), full(wvg.shape),
                  full(bf.shape), full(gq.shape), full(gk.shape)],
        out_specs=[tok,
                   pl.BlockSpec((1, d, tm), lambda bi, i: (bi, 0, i)),
                   tok, tok,
                   pl.BlockSpec((1, N_HEADS, tm), lambda bi, i: (bi, 0, i))],
        out_shape=[jax.ShapeDtypeStruct((b, s, d), BF16),
                   jax.ShapeDtypeStruct((b, d, s), BF16),
                   jax.ShapeDtypeStruct((b, s, d), BF16),
                   jax.ShapeDtypeStruct((b, s, d), BF16),
                   jax.ShapeDtypeStruct((b, N_HEADS, s), F32)],
        scratch_shapes=[pltpu.VMEM((N_HEADS, LANES), F32)],
        compiler_params=pltpu.CompilerParams(
            dimension_semantics=("parallel", "arbitrary"), vmem_limit_bytes=VMEM_LIMIT),
        name="inproj_a",
    )(x, g, wqkfT, wvg, bf, gq, gk)


BIAS_ROWS = 16


def _fox_kernel(q_ref, kT_ref, v_ref, c_ref, woa_ref, wkv_ref, wb_ref, wob_ref,
                o_ref, wo_ref, wkvT_ref, wqT_ref, wgT_ref, woT_ref, bias_ref, vext_ref):
    @pl.when(pl.program_id(1) == 0)
    def _():
        _prep_layer_b_rows(woa_ref, wkv_ref, wb_ref, wob_ref, wo_ref, wkvT_ref, wqT_ref, wgT_ref, woT_ref)

    s_len = q_ref.shape[1]
    n_pairs = q_ref.shape[2] // PAIR
    tq = FOX_Q_TILE
    lane = lax.broadcasted_iota(jnp.int32, (tq, PAIR), 1)
    first = lane < HEAD_DIM
    ones_a = jnp.where(lane < 3, 1.0, 0.0).astype(BF16)
    ones_b = jnp.where(jnp.logical_and(lane >= 3, lane < 6), 1.0, 0.0).astype(BF16)
    row = lax.broadcasted_iota(jnp.int32, (2 * tq, tq), 0) % tq
    col = lax.broadcasted_iota(jnp.int32, (2 * tq, tq), 1)
    causal = col <= row

    brow = lax.broadcasted_iota(jnp.int32, (BIAS_ROWS, s_len), 0)
    for p in range(n_pairs):
        head = 2 * (pl.program_id(1) * n_pairs + p)
        parts = _split3(-c_ref[0, pl.ds(head, 1), :]) + _split3(-c_ref[0, pl.ds(head + 1, 1), :])
        bias = jnp.zeros((BIAS_ROWS, s_len), F32)
        for r, part in enumerate(parts):
            bias = jnp.where(brow == r, part.astype(F32), bias)
        bias_ref[p] = bias.astype(BF16)
        vext_ref[p, :, :PAIR] = v_ref[0, :, p * PAIR:(p + 1) * PAIR]
        vext_ref[p, :, PAIR:] = jnp.ones((s_len, LANES), BF16)

    def rhs(p, c0, c1):
        pad = jnp.zeros((PAIR - BIAS_ROWS, c1 - c0), BF16)
        return jnp.concatenate(
            [kT_ref[0, p * PAIR:(p + 1) * PAIR, c0:c1], bias_ref[p, :, c0:c1], pad], axis=0)

    def logits(item):
        p, qb = item
        r0 = qb * tq
        q2 = q_ref[0, r0:r0 + tq, p * PAIR:(p + 1) * PAIR]
        zero = jnp.zeros_like(q2)
        lhs = jnp.concatenate(
            [jnp.concatenate([jnp.where(first, q2, zero), ones_a], axis=1),
             jnp.concatenate([jnp.where(first, zero, q2), ones_b], axis=1)], axis=0)
        s_diag = jnp.where(causal, _dot(lhs, rhs(p, r0, r0 + tq)), NEG)
        s_off = _dot(lhs, rhs(p, 0, r0)) if qb > 0 else None
        return s_diag, s_off

    order = [(p, qb) for qb in reversed(range(s_len // tq)) for p in range(n_pairs)]
    pending = [logits(item) for item in order[:FOX_LOOKAHEAD]]
    for idx, (p, qb) in enumerate(order):
        r0 = qb * tq
        s_diag, s_off = pending.pop(0)
        if idx + FOX_LOOKAHEAD < len(order):
            pending.append(logits(order[idx + FOX_LOOKAHEAD]))
        m = jnp.max(s_diag, axis=1, keepdims=True)
        if qb > 0:
            m = jnp.maximum(m, jnp.max(s_off, axis=1, keepdims=True))
        pv = _dot(jnp.exp2((s_diag - m).astype(BF16)), vext_ref[p, r0:r0 + tq, :])
        if qb > 0:
            pv = pv + _dot(jnp.exp2((s_off - m).astype(BF16)), vext_ref[p, :r0, :])
        o = pv[:, :PAIR] / pv[:, PAIR:]
        o_ref[0, r0:r0 + tq, p * PAIR:(p + 1) * PAIR] = jnp.where(first, o[:tq], o[tq:]).astype(BF16)


def _fox_attention(q, kT, v, cT, w_out_a, w_kv, w_in_b, w_out_b):
    b, s, d = q.shape
    np_, wd = FOX_PAIRS, FOX_PAIRS * PAIR
    r = d // b
    assert r * b == d and r % LANES == 0
    rows3 = lambda a: pl.BlockSpec((1, r, a.shape[2]), lambda bi, j: (0, bi, 0))
    same = lambda n: pl.BlockSpec((r, n), lambda bi, j: (bi, 0))
    trans = lambda m: pl.BlockSpec((m, r), lambda bi, j: (0, bi))
    weights = [((d, d), same(d)), ((2 * KV_WIDTH, d), trans(2 * KV_WIDTH)),
               ((d, d), trans(d)), ((d, d), trans(d)), ((d, d), trans(d))]
    return pl.pallas_call(
        _fox_kernel,
        grid=(b, d // wd),
        in_specs=[pl.BlockSpec((1, s, wd), lambda bi, j: (bi, 0, j)),
                  pl.BlockSpec((1, wd, s), lambda bi, j: (bi, j, 0)),
                  pl.BlockSpec((1, s, wd), lambda bi, j: (bi, 0, j)),
                  pl.BlockSpec((1, N_HEADS, s), lambda bi, j: (bi, 0, 0)),
                  rows3(w_out_a), pl.BlockSpec((r, w_kv.shape[1]), lambda bi, j: (bi, 0)),
                  rows3(w_in_b), rows3(w_out_b)],
        out_specs=[pl.BlockSpec((1, s, wd), lambda bi, j: (bi, 0, j))] + [spec for _, spec in weights],
        out_shape=[jax.ShapeDtypeStruct((b, s, d), BF16)]
        + [jax.ShapeDtypeStruct(shape, BF16) for shape, _ in weights],
        scratch_shapes=[pltpu.VMEM((np_, BIAS_ROWS, s), BF16), pltpu.VMEM((np_, s, PAIR + LANES), BF16)],
        compiler_params=pltpu.CompilerParams(
            dimension_semantics=("parallel", "arbitrary"), vmem_limit_bytes=VMEM_LIMIT),
        name="fox_attention",
    )(q, kT, v, cT, w_out_a, w_kv, w_in_b, w_out_b)


def _layer_b_kernel(sinks_ref, o_ref, sg_ref, x_ref, wo_ref, gkv_ref, gb_ref, wkvT_ref,
                    gkn_ref, wqT_ref, wgT_ref, gqn_ref, pos_ref, invf_ref, woT_ref,
                    out_ref, kext_ref, vext_ref, qT_ref, sgT_ref, oT_ref):
    tm = x_ref.shape[1]
    w = WINDOW
    nq = GROUP * w
    first_tile = pl.program_id(1) == 0

    lane_head = lax.shift_right_logical(
        lax.broadcasted_iota(jnp.int32, (2 * w, KV_WIDTH), 1), HEAD_DIM.bit_length() - 1)
    q_head = lax.shift_right_logical(lax.broadcasted_iota(jnp.int32, (1, nq), 1), w.bit_length() - 1)
    sinks = []
    for g in range(N_KV_HEADS):
        sink = jnp.zeros((1, nq), F32)
        for i in range(GROUP):
            sink = jnp.where(q_head == i, sinks_ref[g * GROUP + i] * LOG2E, sink)
        sinks.append(sink)
    key = lax.broadcasted_iota(jnp.int32, (w, nq), 0)
    qry = jnp.bitwise_and(lax.broadcasted_iota(jnp.int32, (w, nq), 1), w - 1)
    own = key <= qry
    ones_rows = jnp.ones((SUM_ROWS, 2 * w), BF16)
    gkn = _col(gkn_ref)
    gqn = _col(gqn_ref, SCALE * LOG2E)

    parts = [slice(j * SUB_TILE, (j + 1) * SUB_TILE) for j in range(tm // SUB_TILE)]
    for rows in parts:
        og = (o_ref[0, rows, :].astype(F32) * sg_ref[0, rows, :].astype(F32)).astype(BF16)
        out_ref[0, rows, :] = x_ref[0, rows, :] + _dot(og, wo_ref[...])

    @pl.when(first_tile)
    def _():
        kext_ref[0:w, :] = jnp.zeros((w, KV_WIDTH), BF16)
        vext_ref[:, 0:w] = jnp.zeros((KV_WIDTH, w), BF16)

    @pl.when(jnp.logical_not(first_tile))
    def _():
        kext_ref[0:w, :] = kext_ref[tm:tm + w, :]
        vext_ref[:, 0:w] = vext_ref[:, tm:tm + w]

    def projection_stages(cols):
        st = {}
        ext = slice(w + cols.start, w + cols.stop)

        def k_stage():
            h = out_ref[0, cols, :]
            ms = jnp.mean(h * h, axis=-1, keepdims=True)
            hn = h * lax.rsqrt(ms + EPS)
            st["u_kv"] = (hn * gkv_ref[...]).astype(BF16)
            st["u_b"] = (hn * gb_ref[...]).astype(BF16)
            ang = invf_ref[...] * pos_ref[:, cols].astype(F32)
            st["cos"] = jnp.cos(ang)
            st["sin"] = jnp.sin(ang)
            kvT = _dot_nt(wkvT_ref[...], st["u_kv"])
            kT = _head_norm_fm(kvT[:KV_WIDTH], gkn)
            kext_ref[ext, :] = _rope_fm(kT, st["cos"], st["sin"]).T.astype(BF16)
            vext_ref[:, ext] = kvT[KV_WIDTH:].astype(BF16)

        def q_stage():
            qT = _head_norm_fm(_dot_nt(wqT_ref[...], st["u_b"]), gqn)
            qT_ref[:, cols] = _rope_fm(qT, st["cos"], st["sin"]).astype(BF16)

        def gate_stage():
            gateT = _dot_nt(wgT_ref[...], st["u_b"])
            sgT_ref[:, cols] = (gateT * jax.nn.sigmoid(gateT)).astype(BF16)

        return [k_stage, q_stage, gate_stage]

    def logits(item):
        g, n = item
        k2 = kext_ref[n * w:(n + 2) * w, :]
        k_band = jnp.where(lane_head == g, k2, jnp.zeros_like(k2))
        q_rep = jnp.concatenate(
            [jnp.concatenate(
                [qT_ref[(g * GROUP + i) * HEAD_DIM:(g * GROUP + i + 1) * HEAD_DIM, n * w:(n + 1) * w]]
                * N_KV_HEADS, axis=0) for i in range(GROUP)], axis=1)
        s_all = _dot(k_band, q_rep)
        s_prev = s_all[:w]
        if n == 0:
            s_prev = jnp.where(first_tile, NEG, s_prev)
        return jnp.where(own, s_all[w:], s_prev)

    items = [(g, n) for n in range(tm // w) for g in range(N_KV_HEADS)]
    items_per_part = (SUB_TILE // w) * N_KV_HEADS
    pending = {}

    def attend(idx):
        ready_end = (idx // items_per_part + 1) * items_per_part
        if idx % items_per_part >= Q_STAGE_SLOT:
            ready_end = min(ready_end + items_per_part, len(items))
        for ahead in range(idx, min(idx + SWA_LOOKAHEAD + 1, ready_end)):
            if ahead not in pending:
                pending[ahead] = logits(items[ahead])
        g, n = items[idx]
        sT = pending.pop(idx)
        m = jnp.maximum(jnp.max(sT, axis=0, keepdims=True), sinks[g])
        p = jnp.exp2(sT - m).astype(BF16)
        zero = jnp.zeros_like(p)
        p2 = jnp.concatenate([jnp.where(own, zero, p), jnp.where(own, p, zero)], axis=0)
        v_ext = jnp.concatenate(
            [vext_ref[g * HEAD_DIM:(g + 1) * HEAD_DIM, n * w:(n + 2) * w], ones_rows], axis=0)
        out_ext = _dot(v_ext, p2)
        denom = out_ext[HEAD_DIM:HEAD_DIM + 1] + jnp.exp2(sinks[g] - m)
        outT = out_ext[:HEAD_DIM] * (1.0 / denom)
        for i in range(GROUP):
            r0 = (g * GROUP + i) * HEAD_DIM
            oT_ref[r0:r0 + HEAD_DIM, n * w:(n + 1) * w] = outT[:, i * w:(i + 1) * w].astype(BF16)

    def output_projection(cols):
        ogT = (oT_ref[:, cols].astype(F32) * sgT_ref[:, cols].astype(F32)).astype(BF16)
        half = D_MODEL // 2
        outT = jnp.concatenate(
            [_dot(woT_ref[:half, :], ogT), _dot(woT_ref[half:, :], ogT)], axis=0)
        out_ref[0, cols, :] = out_ref[0, cols, :] + outT.T

    stages = [projection_stages(cols) for cols in parts]
    for stage in stages[0]:
        stage()
    for j, cols in enumerate(parts):
        upcoming = stages[j + 1] if j + 1 < len(parts) else []
        for k in range(items_per_part):
            if k % STAGE_SPACING == 0 and k // STAGE_SPACING < len(upcoming):
                upcoming[k // STAGE_SPACING]()
            attend(j * items_per_part + k)
        output_projection(cols)


def _layer_b(sinks, o, sg, x, wo, gkv, gb, wkvT, gkn, wqT, wgT, gqn, pos, invf, woT):
    b, s, d = x.shape
    tm = ROW_TILE
    full = lambda a: pl.BlockSpec(a.shape, lambda bi, i: (0,) * a.ndim)
    tok = pl.BlockSpec((1, tm, d), lambda bi, i: (bi, i, 0))
    return pl.pallas_call(
        _layer_b_kernel,
        grid=(b, s // tm),
        in_specs=[pl.BlockSpec(memory_space=pltpu.SMEM),
                  tok, tok, tok, full(wo), full(gkv), full(gb), full(wkvT), full(gkn),
                  full(wqT), full(wgT), full(gqn),
                  pl.BlockSpec((1, tm), lambda bi, i: (0, i)), full(invf), full(woT)],
        out_specs=tok,
        out_shape=jax.ShapeDtypeStruct((b, s, d), F32),
        scratch_shapes=[pltpu.VMEM((tm + WINDOW, KV_WIDTH), BF16),
                        pltpu.VMEM((KV_WIDTH, tm + WINDOW), BF16),
                        pltpu.VMEM((d, tm), BF16),
                        pltpu.VMEM((d, tm), BF16),
                        pltpu.VMEM((d, tm), BF16)],
        compiler_params=pltpu.CompilerParams(
            dimension_semantics=("parallel", "arbitrary"), vmem_limit_bytes=VMEM_LIMIT),
        name="layer_b",
    )(sinks, o, sg, x, wo, gkv, gb, wkvT, gkn, wqT, wgT, gqn, pos, invf, woT)


def kernel(x, positions, norm_a_g, w_in_a, b_forget, qnorm_a_g, knorm_a_g, w_out_a, kv_norm_g, w_kv,
           knorm_b_g, norm_b_g, w_in_b, qnorm_b_g, sinks, w_out_b):
    b, s, d = x.shape
    wqkfT, wvg = _prep_weights_a(w_in_a[0].T)
    q, kT, v, sg, cT = _inproj_a(
        x, norm_a_g[0].reshape(1, d), wqkfT, wvg, b_forget[0].reshape(1, N_HEADS),
        qnorm_a_g[0].reshape(1, HEAD_DIM), knorm_a_g[0].reshape(1, HEAD_DIM))
    o, wo_a, wkvT, wqT, wgT, woT_b = _fox_attention(q, kT, v, cT, w_out_a, w_kv, w_in_b, w_out_b)

    inv_freq = jnp.power(jnp.float32(ROPE_THETA),
                         -jnp.arange(0, ROT_DIM, 2, dtype=F32) / ROT_DIM).reshape(ROT_HALF, 1)
    return _layer_b(
        sinks[0], o, sg, x, wo_a, kv_norm_g.reshape(1, d), norm_b_g[0].reshape(1, d),
        wkvT, knorm_b_g.reshape(1, HEAD_DIM),
        wqT, wgT, qnorm_b_g[0].reshape(1, HEAD_DIM),
        positions.reshape(1, s), inv_freq, woT_b)
```

```python
import jax
import jax.numpy as jnp
from jax import lax
from jax.experimental import pallas as pl
from jax.experimental.pallas import tpu as pltpu

D_MODEL = 1024
HEAD_DIM = 64
N_HEADS = 16
N_KV_HEADS = 4
GROUP = N_HEADS // N_KV_HEADS
KV_WIDTH = N_KV_HEADS * HEAD_DIM
WINDOW = 128
ROT_DIM = HEAD_DIM // 4
ROT_HALF = ROT_DIM // 2
ROPE_THETA = 500000.0
EPS = 1e-6
SCALE = HEAD_DIM ** -0.5
LOG2E = 1.4426950408889634
NEG = -1e30

LANES = 128
PAIR = 2 * HEAD_DIM
assert PAIR == LANES

F32 = jnp.float32
BF16 = jnp.bfloat16

PREP_ROWS = 128
ROW_TILE = 1024
SUB_TILE = 256
FOX_Q_TILE = 256
FOX_PAIRS = 2
FOX_LOOKAHEAD = 2
SWA_LOOKAHEAD = 4
STAGE_SPACING = 2
Q_STAGE_SLOT = STAGE_SPACING
SUM_ROWS = 16
VMEM_LIMIT = 56 * 1024 * 1024


def _dot(a, b):
    return jnp.dot(a, b, preferred_element_type=F32)


def _dot_nt(a, b):
    return lax.dot_general(a, b, (((1,), (1,)), ((), ())), preferred_element_type=F32)


def _col(row_ref, scale=1.0):
    row = row_ref[...]
    n = row.shape[1]
    diag = lax.broadcasted_iota(jnp.int32, (n, n), 0) == lax.broadcasted_iota(jnp.int32, (n, n), 1)
    return jnp.sum(jnp.where(diag, row, 0.0), axis=1, keepdims=True) * scale


def _head_norm_fm(t, gain_col):
    blocks = []
    for h in range(t.shape[0] // HEAD_DIM):
        blk = t[h * HEAD_DIM:(h + 1) * HEAD_DIM, :]
        ms = jnp.mean(blk * blk, axis=0, keepdims=True)
        blocks.append(blk * lax.rsqrt(ms + EPS) * gain_col)
    return jnp.concatenate(blocks, axis=0)


def _rope_fm(t, cos, sin):
    blocks = []
    for h in range(t.shape[0] // HEAD_DIM):
        base = h * HEAD_DIM
        x1 = t[base:base + ROT_HALF, :]
        x2 = t[base + ROT_HALF:base + ROT_DIM, :]
        blocks.append(x1 * cos - x2 * sin)
        blocks.append(x1 * sin + x2 * cos)
        blocks.append(t[base + ROT_DIM:base + HEAD_DIM, :])
    return jnp.concatenate(blocks, axis=0)


def _split3(x):
    hi = x.astype(BF16)
    r1 = x - hi.astype(F32)
    mid = r1.astype(BF16)
    lo = (r1 - mid.astype(F32)).astype(BF16)
    return hi, mid, lo


def _prep_kernel(waT_ref, wqkfT_ref, wvg_ref):
    d = D_MODEL
    wqkfT_ref[:2 * d, :] = waT_ref[:2 * d, :].astype(BF16)
    wqkfT_ref[2 * d:, :] = waT_ref[3 * d:3 * d + N_HEADS, :].astype(BF16)
    wvg_ref[:, :d] = waT_ref[2 * d:3 * d, :].T.astype(BF16)
    wvg_ref[:, d:] = waT_ref[3 * d + N_HEADS:, :].T.astype(BF16)


def _prep_weights_a(w_in_aT):
    d = D_MODEL
    r = PREP_ROWS
    return pl.pallas_call(
        _prep_kernel,
        grid=(d // r,),
        in_specs=[pl.BlockSpec((w_in_aT.shape[0], r), lambda i: (0, i))],
        out_specs=[pl.BlockSpec((2 * d + N_HEADS, r), lambda i: (0, i)),
                   pl.BlockSpec((r, 2 * d), lambda i: (i, 0))],
        out_shape=[jax.ShapeDtypeStruct((2 * d + N_HEADS, d), BF16),
                   jax.ShapeDtypeStruct((d, 2 * d), BF16)],
        compiler_params=pltpu.CompilerParams(
            dimension_semantics=("parallel",), vmem_limit_bytes=VMEM_LIMIT),
        name="prep_weights_a",
    )(w_in_aT)


def _prep_layer_b_rows(woa_ref, wkv_ref, wb_ref, wob_ref, wo_ref, wkvT_ref, wqT_ref, wgT_ref, woT_ref):
    d = D_MODEL
    wo_ref[...] = woa_ref[0].astype(BF16)
    wkvT_ref[...] = wkv_ref[...].T.astype(BF16)
    wb = wb_ref[0]
    wqT_ref[...] = wb[:, :d].T.astype(BF16)
    wgT_ref[...] = wb[:, d:].T.astype(BF16)
    woT_ref[...] = wob_ref[0].T.astype(BF16)


def _inproj_a_kernel(x_ref, g_ref, wqkfT_ref, wvg_ref, bf_ref, gq_ref, gk_ref,
                     q_ref, kT_ref, v_ref, sg_ref, cT_ref, carry_ref):
    tm = x_ref.shape[1]

    @pl.when(pl.program_id(1) == 0)
    def _():
        carry_ref[...] = jnp.zeros_like(carry_ref)

    parts = [slice(j * SUB_TILE, (j + 1) * SUB_TILE) for j in range(tm // SUB_TILE)]
    gq = _col(gq_ref, SCALE * LOG2E)
    gk = _col(gk_ref)
    b_forget = _col(bf_ref)
    u_parts, log_f_parts = [], []
    for rows in parts:
        x = x_ref[0, rows, :]
        ms = jnp.mean(x * x, axis=-1, keepdims=True)
        u32 = x * lax.rsqrt(ms + EPS) * g_ref[...]
        u_parts.append(u32.astype(BF16))
        uT = u32.T.astype(BF16)
        qT = _head_norm_fm(_dot(wqkfT_ref[:D_MODEL, :], uT), gq)
        kfT = _dot(wqkfT_ref[D_MODEL:, :], uT)
        kT = _head_norm_fm(kfT[:D_MODEL], gk)
        q_ref[0, rows, :] = qT.T.astype(BF16)
        kT_ref[0, :, rows] = kT.astype(BF16)
        f = kfT[D_MODEL:] + b_forget
        log_f_parts.append(jnp.minimum(f, 0.0) - jnp.log1p(jnp.exp(-jnp.abs(f))))
    u = jnp.concatenate(u_parts, axis=0)

    gate = _dot(u, wvg_ref[:, D_MODEL:])
    sg_ref[0] = (gate * jax.nn.sigmoid(gate)).astype(BF16)

    row = lax.broadcasted_iota(jnp.int32, (SUB_TILE, SUB_TILE), 0)
    col = lax.broadcasted_iota(jnp.int32, (SUB_TILE, SUB_TILE), 1)
    tri = (row <= col).astype(BF16)
    sums = [_dot(jnp.concatenate(_split3(log_f), axis=0), tri) for log_f in log_f_parts]
    v_ref[0] = _dot(u, wvg_ref[:, :D_MODEL]).astype(BF16)
    carry = carry_ref[:, 0:1]
    for rows, s3 in zip(parts, sums):
        c = (s3[:N_HEADS] + s3[N_HEADS:2 * N_HEADS] + s3[2 * N_HEADS:]) + carry
        cT_ref[0, :, rows] = c * LOG2E
        carry = c[:, SUB_TILE - 1:SUB_TILE]
    carry_ref[...] = jnp.broadcast_to(carry, carry_ref.shape)


def _inproj_a(x, g, wqkfT, wvg, bf, gq, gk):
    b, s, d = x.shape
    tm = ROW_TILE
    full = lambda shape: pl.BlockSpec(shape, lambda bi, i: (0,) * len(shape))
    tok = pl.BlockSpec((1, tm, d), lambda bi, i: (bi, i, 0))
    return pl.pallas_call(
        _inproj_a_kernel,
        grid=(b, s // tm),
        in_specs=[tok, full(g.shape), full(wqkfT.shape), full(wvg.shape),
                  full(bf.shape), full(gq.shape), full(gk.shape)],
        out_specs=[tok,
                   pl.BlockSpec((1, d, tm), lambda bi, i: (bi, 0, i)),
                   tok, tok,
                   pl.BlockSpec((1, N_HEADS, tm), lambda bi, i: (bi, 0, i))],
        out_shape=[jax.ShapeDtypeStruct((b, s, d), BF16),
                   jax.ShapeDtypeStruct((b, d, s), BF16),
                   jax.ShapeDtypeStruct((b, s, d), BF16),
                   jax.ShapeDtypeStruct((b, s, d), BF16),
                   jax.ShapeDtypeStruct((b, N_HEADS, s), F32)],
        scratch_shapes=[pltpu.VMEM((N_HEADS, LANES), F32)],
        compiler_params=pltpu.CompilerParams(
            dimension_semantics=("parallel", "arbitrary"), vmem_limit_bytes=VMEM_LIMIT),
        name="inproj_a",
    )(x, g, wqkfT, wvg, bf, gq, gk)


BIAS_ROWS = 16


def _fox_kernel(q_ref, kT_ref, v_ref, c_ref, woa_ref, wkv_ref, wb_ref, wob_ref,
                o_ref, wo_ref, wkvT_ref, wqT_ref, wgT_ref, woT_ref, bias_ref, vext_ref):
    @pl.when(pl.program_id(1) == 0)
    def _():
        _prep_layer_b_rows(woa_ref, wkv_ref, wb_ref, wob_ref, wo_ref, wkvT_ref, wqT_ref, wgT_ref, woT_ref)

    s_len = q_ref.shape[1]
    n_pairs = q_ref.shape[2] // PAIR
    tq = FOX_Q_TILE
    lane = lax.broadcasted_iota(jnp.int32, (tq, PAIR), 1)
    first = lane < HEAD_DIM
    ones_a = jnp.where(lane < 3, 1.0, 0.0).astype(BF16)
    ones_b = jnp.where(jnp.logical_and(lane >= 3, lane < 6), 1.0, 0.0).astype(BF16)
    row = lax.broadcasted_iota(jnp.int32, (2 * tq, tq), 0) % tq
    col = lax.broadcasted_iota(jnp.int32, (2 * tq, tq), 1)
    causal = col <= row

    brow = lax.broadcasted_iota(jnp.int32, (BIAS_ROWS, s_len), 0)
    for p in range(n_pairs):
        head = 2 * (pl.program_id(1) * n_pairs + p)
        parts = _split3(-c_ref[0, pl.ds(head, 1), :]) + _split3(-c_ref[0, pl.ds(head + 1, 1), :])
        bias = jnp.zeros((BIAS_ROWS, s_len), F32)
        for r, part in enumerate(parts):
            bias = jnp.where(brow == r, part.astype(F32), bias)
        bias_ref[p] = bias.astype(BF16)
        vext_ref[p, :, :PAIR] = v_ref[0, :, p * PAIR:(p + 1) * PAIR]
        vext_ref[p, :, PAIR:] = jnp.ones((s_len, LANES), BF16)

    def rhs(p, c0, c1):
        pad = jnp.zeros((PAIR - BIAS_ROWS, c1 - c0), BF16)
        return jnp.concatenate(
            [kT_ref[0, p * PAIR:(p + 1) * PAIR, c0:c1], bias_ref[p, :, c0:c1], pad], axis=0)

    def logits(item):
        p, qb = item
        r0 = qb * tq
        q2 = q_ref[0, r0:r0 + tq, p * PAIR:(p + 1) * PAIR]
        zero = jnp.zeros_like(q2)
        lhs = jnp.concatenate(
            [jnp.concatenate([jnp.where(first, q2, zero), ones_a], axis=1),
             jnp.concatenate([jnp.where(first, zero, q2), ones_b], axis=1)], axis=0)
        s_all = _dot(lhs, rhs(p, 0, r0 + tq))
        s_diag = jnp.where(causal, s_all[:, r0:], NEG)
        s_off = s_all[:, :r0] if qb > 0 else None
        return s_diag, s_off

    order = [(p, qb) for qb in reversed(range(s_len // tq)) for p in range(n_pairs)]
    pending = [logits(item) for item in order[:FOX_LOOKAHEAD]]
    for idx, (p, qb) in enumerate(order):
        r0 = qb * tq
        s_diag, s_off = pending.pop(0)
        if idx + FOX_LOOKAHEAD < len(order):
            pending.append(logits(order[idx + FOX_LOOKAHEAD]))
        m = jnp.max(s_diag, axis=1, keepdims=True)
        if qb > 0:
            m = jnp.maximum(m, jnp.max(s_off, axis=1, keepdims=True))
        probs = jnp.exp2((s_diag - m).astype(BF16))
        if qb > 0:
            probs = jnp.concatenate([jnp.exp2((s_off - m).astype(BF16)), probs], axis=1)
        pv = _dot(probs, vext_ref[p, :r0 + tq, :])
        o = pv[:, :PAIR] / pv[:, PAIR:]
        o_ref[0, r0:r0 + tq, p * PAIR:(p + 1) * PAIR] = jnp.where(first, o[:tq], o[tq:]).astype(BF16)


def _fox_attention(q, kT, v, cT, w_out_a, w_kv, w_in_b, w_out_b):
    b, s, d = q.shape
    np_, wd = FOX_PAIRS, FOX_PAIRS * PAIR
    r = d // b
    assert r * b == d and r % LANES == 0
    rows3 = lambda a: pl.BlockSpec((1, r, a.shape[2]), lambda bi, j: (0, bi, 0))
    same = lambda n: pl.BlockSpec((r, n), lambda bi, j: (bi, 0))
    trans = lambda m: pl.BlockSpec((m, r), lambda bi, j: (0, bi))
    weights = [((d, d), same(d)), ((2 * KV_WIDTH, d), trans(2 * KV_WIDTH)),
               ((d, d), trans(d)), ((d, d), trans(d)), ((d, d), trans(d))]
    return pl.pallas_call(
        _fox_kernel,
        grid=(b, d // wd),
        in_specs=[pl.BlockSpec((1, s, wd), lambda bi, j: (bi, 0, j)),
                  pl.BlockSpec((1, wd, s), lambda bi, j: (bi, j, 0)),
                  pl.BlockSpec((1, s, wd), lambda bi, j: (bi, 0, j)),
                  pl.BlockSpec((1, N_HEADS, s), lambda bi, j: (bi, 0, 0)),
                  rows3(w_out_a), pl.BlockSpec((r, w_kv.shape[1]), lambda bi, j: (bi, 0)),
                  rows3(w_in_b), rows3(w_out_b)],
        out_specs=[pl.BlockSpec((1, s, wd), lambda bi, j: (bi, 0, j))] + [spec for _, spec in weights],
        out_shape=[jax.ShapeDtypeStruct((b, s, d), BF16)]
        + [jax.ShapeDtypeStruct(shape, BF16) for shape, _ in weights],
        scratch_shapes=[pltpu.VMEM((np_, BIAS_ROWS, s), BF16), pltpu.VMEM((np_, s, PAIR + LANES), BF16)],
        compiler_params=pltpu.CompilerParams(
            dimension_semantics=("parallel", "arbitrary"), vmem_limit_bytes=VMEM_LIMIT),
        name="fox_attention",
    )(q, kT, v, cT, w_out_a, w_kv, w_in_b, w_out_b)


def _layer_b_kernel(sinks_ref, o_ref, sg_ref, x_ref, wo_ref, gkv_ref, gb_ref, wkvT_ref,
                    gkn_ref, wqT_ref, wgT_ref, gqn_ref, pos_ref, invf_ref, woT_ref,
                    out_ref, kext_ref, vext_ref, qT_ref, sgT_ref, oT_ref):
    tm = x_ref.shape[1]
    w = WINDOW
    nq = GROUP * w
    first_tile = pl.program_id(1) == 0

    lane_head = lax.shift_right_logical(
        lax.broadcasted_iota(jnp.int32, (2 * w, KV_WIDTH), 1), HEAD_DIM.bit_length() - 1)
    q_head = lax.shift_right_logical(lax.broadcasted_iota(jnp.int32, (1, nq), 1), w.bit_length() - 1)
    sinks = []
    for g in range(N_KV_HEADS):
        sink = jnp.zeros((1, nq), F32)
        for i in range(GROUP):
            sink = jnp.where(q_head == i, sinks_ref[g * GROUP + i] * LOG2E, sink)
        sinks.append(sink)
    key = lax.broadcasted_iota(jnp.int32, (w, nq), 0)
    qry = jnp.bitwise_and(lax.broadcasted_iota(jnp.int32, (w, nq), 1), w - 1)
    own = key <= qry
    ones_rows = jnp.ones((SUM_ROWS, 2 * w), BF16)
    gkn = _col(gkn_ref)
    gqn = _col(gqn_ref, SCALE * LOG2E)

    parts = [slice(j * SUB_TILE, (j + 1) * SUB_TILE) for j in range(tm // SUB_TILE)]
    for rows in parts:
        og = (o_ref[0, rows, :].astype(F32) * sg_ref[0, rows, :].astype(F32)).astype(BF16)
        out_ref[0, rows, :] = x_ref[0, rows, :] + _dot(og, wo_ref[...])

    @pl.when(first_tile)
    def _():
        kext_ref[0:w, :] = jnp.zeros((w, KV_WIDTH), BF16)
        vext_ref[:, 0:w] = jnp.zeros((KV_WIDTH, w), BF16)

    @pl.when(jnp.logical_not(first_tile))
    def _():
        kext_ref[0:w, :] = kext_ref[tm:tm + w, :]
        vext_ref[:, 0:w] = vext_ref[:, tm:tm + w]

    def projection_stages(cols):
        st = {}
        ext = slice(w + cols.start, w + cols.stop)

        def k_stage():
            h = out_ref[0, cols, :]
            ms = jnp.mean(h * h, axis=-1, keepdims=True)
            hn = h * lax.rsqrt(ms + EPS)
            st["u_kv"] = (hn * gkv_ref[...]).astype(BF16)
            st["u_b"] = (hn * gb_ref[...]).astype(BF16)
            ang = invf_ref[...] * pos_ref[:, cols].astype(F32)
            st["cos"] = jnp.cos(ang)
            st["sin"] = jnp.sin(ang)
            kvT = _dot_nt(wkvT_ref[...], st["u_kv"])
            kT = _head_norm_fm(kvT[:KV_WIDTH], gkn)
            kext_ref[ext, :] = _rope_fm(kT, st["cos"], st["sin"]).T.astype(BF16)
            vext_ref[:, ext] = kvT[KV_WIDTH:].astype(BF16)

        def q_stage():
            qT = _head_norm_fm(_dot_nt(wqT_ref[...], st["u_b"]), gqn)
            qT_ref[:, cols] = _rope_fm(qT, st["cos"], st["sin"]).astype(BF16)

        def gate_stage():
            gateT = _dot_nt(wgT_ref[...], st["u_b"])
            sgT_ref[:, cols] = (gateT * jax.nn.sigmoid(gateT)).astype(BF16)

        return [k_stage, q_stage, gate_stage]

    def logits(item):
        g, n = item
        k2 = kext_ref[n * w:(n + 2) * w, :]
        k_band = jnp.where(lane_head == g, k2, jnp.zeros_like(k2))
        q_rep = jnp.concatenate(
            [jnp.concatenate(
                [qT_ref[(g * GROUP + i) * HEAD_DIM:(g * GROUP + i + 1) * HEAD_DIM, n * w:(n + 1) * w]]
                * N_KV_HEADS, axis=0) for i in range(GROUP)], axis=1)
        s_all = _dot(k_band, q_rep)
        s_prev = s_all[:w]
        if n == 0:
            s_prev = jnp.where(first_tile, NEG, s_prev)
        return jnp.where(own, s_all[w:], s_prev)

    items = [(g, n) for n in range(tm // w) for g in range(N_KV_HEADS)]
    items_per_part = (SUB_TILE // w) * N_KV_HEADS
    pending = {}

    def attend(idx):
        ready_end = (idx // items_per_part + 1) * items_per_part
        if idx % items_per_part >= Q_STAGE_SLOT:
            ready_end = min(ready_end + items_per_part, len(items))
        for ahead in range(idx, min(idx + SWA_LOOKAHEAD + 1, ready_end)):
            if ahead not in pending:
                pending[ahead] = logits(items[ahead])
        g, n = items[idx]
        sT = pending.pop(idx)
        m = jnp.maximum(jnp.max(sT, axis=0, keepdims=True), sinks[g])
        p = jnp.exp2(sT - m).astype(BF16)
        zero = jnp.zeros_like(p)
        p2 = jnp.concatenate([jnp.where(own, zero, p), jnp.where(own, p, zero)], axis=0)
        v_ext = jnp.concatenate(
            [vext_ref[g * HEAD_DIM:(g + 1) * HEAD_DIM, n * w:(n + 2) * w], ones_rows], axis=0)
        out_ext = _dot(v_ext, p2)
        denom = out_ext[HEAD_DIM:HEAD_DIM + 1] + jnp.exp2(sinks[g] - m)
        outT = out_ext[:HEAD_DIM] * (1.0 / denom)
        for i in range(GROUP):
            r0 = (g * GROUP + i) * HEAD_DIM
            oT_ref[r0:r0 + HEAD_DIM, n * w:(n + 1) * w] = outT[:, i * w:(i + 1) * w].astype(BF16)

    def output_projection(cols):
        ogT = (oT_ref[:, cols].astype(F32) * sgT_ref[:, cols].astype(F32)).astype(BF16)
        half = D_MODEL // 2
        outT = jnp.concatenate(
            [_dot(woT_ref[:half, :], ogT), _dot(woT_ref[half:, :], ogT)], axis=0)
        out_ref[0, cols, :] = out_ref[0, cols, :] + outT.T

    stages = [projection_stages(cols) for cols in parts]
    for stage in stages[0]:
        stage()
    for j, cols in enumerate(parts):
        upcoming = stages[j + 1] if j + 1 < len(parts) else []
        for k in range(items_per_part):
            if k % STAGE_SPACING == 0 and k // STAGE_SPACING < len(upcoming):
                upcoming[k // STAGE_SPACING]()
            attend(j * items_per_part + k)
        output_projection(cols)


def _layer_b(sinks, o, sg, x, wo, gkv, gb, wkvT, gkn, wqT, wgT, gqn, pos, invf, woT):
    b, s, d = x.shape
    tm = ROW_TILE
    full = lambda a: pl.BlockSpec(a.shape, lambda bi, i: (0,) * a.ndim)
    tok = pl.BlockSpec((1, tm, d), lambda bi, i: (bi, i, 0))
    return pl.pallas_call(
        _layer_b_kernel,
        grid=(b, s // tm),
        in_specs=[pl.BlockSpec(memory_space=pltpu.SMEM),
                  tok, tok, tok, full(wo), full(gkv), full(gb), full(wkvT), full(gkn),
                  full(wqT), full(wgT), full(gqn),
                  pl.BlockSpec((1, tm), lambda bi, i: (0, i)), full(invf), full(woT)],
        out_specs=tok,
        out_shape=jax.ShapeDtypeStruct((b, s, d), F32),
        scratch_shapes=[pltpu.VMEM((tm + WINDOW, KV_WIDTH), BF16),
                        pltpu.VMEM((KV_WIDTH, tm + WINDOW), BF16),
                        pltpu.VMEM((d, tm), BF16),
                        pltpu.VMEM((d, tm), BF16),
                        pltpu.VMEM((d, tm), BF16)],
        compiler_params=pltpu.CompilerParams(
            dimension_semantics=("parallel", "arbitrary"), vmem_limit_bytes=VMEM_LIMIT),
        name="layer_b",
    )(sinks, o, sg, x, wo, gkv, gb, wkvT, gkn, wqT, wgT, gqn, pos, invf, woT)


def kernel(x, positions, norm_a_g, w_in_a, b_forget, qnorm_a_g, knorm_a_g, w_out_a, kv_norm_g, w_kv,
           knorm_b_g, norm_b_g, w_in_b, qnorm_b_g, sinks, w_out_b):
    b, s, d = x.shape
    wqkfT, wvg = _prep_weights_a(w_in_a[0].T)
    q, kT, v, sg, cT = _inproj_a(
        x, norm_a_g[0].reshape(1, d), wqkfT, wvg, b_forget[0].reshape(1, N_HEADS),
        qnorm_a_g[0].reshape(1, HEAD_DIM), knorm_a_g[0].reshape(1, HEAD_DIM))
    o, wo_a, wkvT, wqT, wgT, woT_b = _fox_attention(q, kT, v, cT, w_out_a, w_kv, w_in_b, w_out_b)

    inv_freq = jnp.power(jnp.float32(ROPE_THETA),
                         -jnp.arange(0, ROT_DIM, 2, dtype=F32) / ROT_DIM).reshape(ROT_HALF, 1)
    return _layer_b(
        sinks[0], o, sg, x, wo_a, kv_norm_g.reshape(1, d), norm_b_g[0].reshape(1, d),
        wkvT, knorm_b_g.reshape(1, HEAD_DIM),
        wqT, wgT, qnorm_b_g[0].reshape(1, HEAD_DIM),
        positions.reshape(1, s), inv_freq, woT_b)
```

```python
import jax
import jax.numpy as jnp
from jax import lax
from jax.experimental import pallas as pl
from jax.experimental.pallas import tpu as pltpu

D_MODEL = 1024
HEAD_DIM = 64
N_HEADS = 16
N_KV_HEADS = 4
GROUP = N_HEADS // N_KV_HEADS
KV_WIDTH = N_KV_HEADS * HEAD_DIM
WINDOW = 128
ROT_DIM = HEAD_DIM // 4
ROT_HALF = ROT_DIM // 2
ROPE_THETA = 500000.0
EPS = 1e-6
SCALE = HEAD_DIM ** -0.5
LOG2E = 1.4426950408889634
NEG = -1e30

LANES = 128
PAIR = 2 * HEAD_DIM
assert PAIR == LANES

F32 = jnp.float32
BF16 = jnp.bfloat16

PREP_ROWS = 128
ROW_TILE = 1024
SUB_TILE = 256
FOX_Q_TILE = 256
FOX_PAIRS = 2
FOX_LOOKAHEAD = 2
SWA_LOOKAHEAD = 4
STAGE_SPACING = 2
Q_STAGE_SLOT = STAGE_SPACING
SUM_ROWS = 16
VMEM_LIMIT = 56 * 1024 * 1024


def _dot(a, b):
    return jnp.dot(a, b, preferred_element_type=F32)


def _dot_nt(a, b):
    return lax.dot_general(a, b, (((1,), (1,)), ((), ())), preferred_element_type=F32)


def _col(row_ref, scale=1.0):
    row = row_ref[...]
    n = row.shape[1]
    diag = lax.broadcasted_iota(jnp.int32, (n, n), 0) == lax.broadcasted_iota(jnp.int32, (n, n), 1)
    return jnp.sum(jnp.where(diag, row, 0.0), axis=1, keepdims=True) * scale


def _head_norm_fm(t, gain_col):
    blocks = []
    for h in range(t.shape[0] // HEAD_DIM):
        blk = t[h * HEAD_DIM:(h + 1) * HEAD_DIM, :]
        ms = jnp.mean(blk * blk, axis=0, keepdims=True)
        blocks.append(blk * lax.rsqrt(ms + EPS) * gain_col)
    return jnp.concatenate(blocks, axis=0)


def _rope_fm(t, cos, sin):
    blocks = []
    for h in range(t.shape[0] // HEAD_DIM):
        base = h * HEAD_DIM
        x1 = t[base:base + ROT_HALF, :]
        x2 = t[base + ROT_HALF:base + ROT_DIM, :]
        blocks.append(x1 * cos - x2 * sin)
        blocks.append(x1 * sin + x2 * cos)
        blocks.append(t[base + ROT_DIM:base + HEAD_DIM, :])
    return jnp.concatenate(blocks, axis=0)


def _split3(x):
    hi = x.astype(BF16)
    r1 = x - hi.astype(F32)
    mid = r1.astype(BF16)
    lo = (r1 - mid.astype(F32)).astype(BF16)
    return hi, mid, lo


def _prep_kernel(waT_ref, wqkfT_ref, wvg_ref):
    d = D_MODEL
    wqkfT_ref[:2 * d, :] = waT_ref[:2 * d, :].astype(BF16)
    wqkfT_ref[2 * d:, :] = waT_ref[3 * d:3 * d + N_HEADS, :].astype(BF16)
    wvg_ref[:, :d] = waT_ref[2 * d:3 * d, :].astype(BF16).T
    wvg_ref[:, d:] = waT_ref[3 * d + N_HEADS:, :].astype(BF16).T


def _prep_weights_a(w_in_aT):
    d = D_MODEL
    r = PREP_ROWS
    return pl.pallas_call(
        _prep_kernel,
        grid=(d // r,),
        in_specs=[pl.BlockSpec((w_in_aT.shape[0], r), lambda i: (0, i))],
        out_specs=[pl.BlockSpec((2 * d + N_HEADS, r), lambda i: (0, i)),
                   pl.BlockSpec((r, 2 * d), lambda i: (i, 0))],
        out_shape=[jax.ShapeDtypeStruct((2 * d + N_HEADS, d), BF16),
                   jax.ShapeDtypeStruct((d, 2 * d), BF16)],
        compiler_params=pltpu.CompilerParams(
            dimension_semantics=("parallel",), vmem_limit_bytes=VMEM_LIMIT),
        name="prep_weights_a",
    )(w_in_aT)


def _prep_layer_b_rows(woa_ref, wkv_ref, wb_ref, wob_ref, wo_ref, wkvT_ref, wqT_ref, wgT_ref, woT_ref):
    d = D_MODEL
    wo_ref[...] = woa_ref[0].astype(BF16)
    wkvT_ref[...] = wkv_ref[...].astype(BF16).T
    wb = wb_ref[0]
    wqT_ref[...] = wb[:, :d].astype(BF16).T
    wgT_ref[...] = wb[:, d:].astype(BF16).T
    woT_ref[...] = wob_ref[0].astype(BF16).T


def _inproj_a_kernel(x_ref, g_ref, wqkfT_ref, wvg_ref, bf_ref, gq_ref, gk_ref,
                     q_ref, kT_ref, v_ref, sg_ref, cT_ref, carry_ref):
    tm = x_ref.shape[1]

    @pl.when(pl.program_id(1) == 0)
    def _():
        carry_ref[...] = jnp.zeros_like(carry_ref)

    parts = [slice(j * SUB_TILE, (j + 1) * SUB_TILE) for j in range(tm // SUB_TILE)]
    gq = _col(gq_ref, SCALE * LOG2E)
    gk = _col(gk_ref)
    b_forget = _col(bf_ref)
    u_parts, log_f_parts = [], []
    for rows in parts:
        x = x_ref[0, rows, :]
        ms = jnp.mean(x * x, axis=-1, keepdims=True)
        u32 = x * lax.rsqrt(ms + EPS) * g_ref[...]
        u_parts.append(u32.astype(BF16))
        uT = u_parts[-1].T
        qT = _head_norm_fm(_dot(wqkfT_ref[:D_MODEL, :], uT), gq)
        kfT = _dot(wqkfT_ref[D_MODEL:, :], uT)
        kT = _head_norm_fm(kfT[:D_MODEL], gk)
        q_ref[0, rows, :] = qT.astype(BF16).T
        kT_ref[0, :, rows] = kT.astype(BF16)
        f = kfT[D_MODEL:] + b_forget
        log_f_parts.append(jnp.minimum(f, 0.0) - jnp.log1p(jnp.exp(-jnp.abs(f))))
    u = jnp.concatenate(u_parts, axis=0)

    gate = _dot(u, wvg_ref[:, D_MODEL:])
    sg_ref[0] = (gate * jax.nn.sigmoid(gate)).astype(BF16)

    row = lax.broadcasted_iota(jnp.int32, (SUB_TILE, SUB_TILE), 0)
    col = lax.broadcasted_iota(jnp.int32, (SUB_TILE, SUB_TILE), 1)
    tri = (row <= col).astype(BF16)
    sums = [_dot(jnp.concatenate(_split3(log_f), axis=0), tri) for log_f in log_f_parts]
    v_ref[0] = _dot(u, wvg_ref[:, :D_MODEL]).astype(BF16)
    carry = carry_ref[:, 0:1]
    for rows, s3 in zip(parts, sums):
        c = (s3[:N_HEADS] + s3[N_HEADS:2 * N_HEADS] + s3[2 * N_HEADS:]) + carry
        cT_ref[0, :, rows] = c * LOG2E
        carry = c[:, SUB_TILE - 1:SUB_TILE]
    carry_ref[...] = jnp.broadcast_to(carry, carry_ref.shape)


def _inproj_a(x, g, wqkfT, wvg, bf, gq, gk):
    b, s, d = x.shape
    tm = ROW_TILE
    full = lambda shape: pl.BlockSpec(shape, lambda bi, i: (0,) * len(shape))
    tok = pl.BlockSpec((1, tm, d), lambda bi, i: (bi, i, 0))
    return pl.pallas_call(
        _inproj_a_kernel,
        grid=(b, s // tm),
        in_specs=[tok, full(g.shape), full(wqkfT.shape), full(wvg.shape),
                  full(bf.shape), full(gq.shape), full(gk.shape)],
        out_specs=[tok,
                   pl.BlockSpec((1, d, tm), lambda bi, i: (bi, 0, i)),
                   tok, tok,
                   pl.BlockSpec((1, N_HEADS, tm), lambda bi, i: (bi, 0, i))],
        out_shape=[jax.ShapeDtypeStruct((b, s, d), BF16),
                   jax.ShapeDtypeStruct((b, d, s), BF16),
                   jax.ShapeDtypeStruct((b, s, d), BF16),
                   jax.ShapeDtypeStruct((b, s, d), BF16),
                   jax.ShapeDtypeStruct((b, N_HEADS, s), F32)],
        scratch_shapes=[pltpu.VMEM((N_HEADS, LANES), F32)],
        compiler_params=pltpu.CompilerParams(
            dimension_semantics=("parallel", "arbitrary"), vmem_limit_bytes=VMEM_LIMIT),
        name="inproj_a",
    )(x, g, wqkfT, wvg, bf, gq, gk)


BIAS_ROWS = 16


def _fox_kernel(q_ref, kT_ref, v_ref, c_ref, woa_ref, wkv_ref, wb_ref, wob_ref,
                o_ref, wo_ref, wkvT_ref, wqT_ref, wgT_ref, woT_ref, bias_ref, vext_ref):
    @pl.when(pl.program_id(1) == 0)
    def _():
        _prep_layer_b_rows(woa_ref, wkv_ref, wb_ref, wob_ref, wo_ref, wkvT_ref, wqT_ref, wgT_ref, woT_ref)

    s_len = q_ref.shape[1]
    n_pairs = q_ref.shape[2] // PAIR
    tq = FOX_Q_TILE
    lane = lax.broadcasted_iota(jnp.int32, (tq, PAIR), 1)
    first = lane < HEAD_DIM
    ones_a = jnp.where(lane < 3, 1.0, 0.0).astype(BF16)
    ones_b = jnp.where(jnp.logical_and(lane >= 3, lane < 6), 1.0, 0.0).astype(BF16)
    row = lax.broadcasted_iota(jnp.int32, (2 * tq, tq), 0) % tq
    col = lax.broadcasted_iota(jnp.int32, (2 * tq, tq), 1)
    causal = col <= row

    brow = lax.broadcasted_iota(jnp.int32, (BIAS_ROWS, s_len), 0)
    for p in range(n_pairs):
        head = 2 * (pl.program_id(1) * n_pairs + p)
        parts = _split3(-c_ref[0, pl.ds(head, 1), :]) + _split3(-c_ref[0, pl.ds(head + 1, 1), :])
        bias = jnp.zeros((BIAS_ROWS, s_len), F32)
        for r, part in enumerate(parts):
            bias = jnp.where(brow == r, part.astype(F32), bias)
        bias_ref[p] = bias.astype(BF16)
        vext_ref[p, :, :PAIR] = v_ref[0, :, p * PAIR:(p + 1) * PAIR]
        vext_ref[p, :, PAIR:] = jnp.ones((s_len, LANES), BF16)

    def rhs(p, c0, c1):
        pad = jnp.zeros((PAIR - BIAS_ROWS, c1 - c0), BF16)
        return jnp.concatenate(
            [kT_ref[0, p * PAIR:(p + 1) * PAIR, c0:c1], bias_ref[p, :, c0:c1], pad], axis=0)

    def logits(item):
        p, qb = item
        r0 = qb * tq
        q2 = q_ref[0, r0:r0 + tq, p * PAIR:(p + 1) * PAIR]
        zero = jnp.zeros_like(q2)
        lhs = jnp.concatenate(
            [jnp.concatenate([jnp.where(first, q2, zero), ones_a], axis=1),
             jnp.concatenate([jnp.where(first, zero, q2), ones_b], axis=1)], axis=0)
        s_all = _dot(lhs, rhs(p, 0, r0 + tq))
        s_diag = jnp.where(causal, s_all[:, r0:], NEG)
        s_off = s_all[:, :r0] if qb > 0 else None
        return s_diag, s_off

    order = [(p, qb) for qb in reversed(range(s_len // tq)) for p in range(n_pairs)]
    pending = [logits(item) for item in order[:FOX_LOOKAHEAD]]
    for idx, (p, qb) in enumerate(order):
        r0 = qb * tq
        s_diag, s_off = pending.pop(0)
        if idx + FOX_LOOKAHEAD < len(order):
            pending.append(logits(order[idx + FOX_LOOKAHEAD]))
        m = jnp.max(s_diag, axis=1, keepdims=True)
        if qb > 0:
            m = jnp.maximum(m, jnp.max(s_off, axis=1, keepdims=True))
        probs = jnp.exp2((s_diag - m).astype(BF16))
        if qb > 0:
            probs = jnp.concatenate([jnp.exp2((s_off - m).astype(BF16)), probs], axis=1)
        pv = _dot(probs, vext_ref[p, :r0 + tq, :])
        o = pv[:, :PAIR] / pv[:, PAIR:]
        o_ref[0, r0:r0 + tq, p * PAIR:(p + 1) * PAIR] = jnp.where(first, o[:tq], o[tq:]).astype(BF16)


def _fox_attention(q, kT, v, cT, w_out_a, w_kv, w_in_b, w_out_b):
    b, s, d = q.shape
    np_, wd = FOX_PAIRS, FOX_PAIRS * PAIR
    r = d // b
    assert r * b == d and r % LANES == 0
    rows3 = lambda a: pl.BlockSpec((1, r, a.shape[2]), lambda bi, j: (0, bi, 0))
    same = lambda n: pl.BlockSpec((r, n), lambda bi, j: (bi, 0))
    trans = lambda m: pl.BlockSpec((m, r), lambda bi, j: (0, bi))
    weights = [((d, d), same(d)), ((2 * KV_WIDTH, d), trans(2 * KV_WIDTH)),
               ((d, d), trans(d)), ((d, d), trans(d)), ((d, d), trans(d))]
    return pl.pallas_call(
        _fox_kernel,
        grid=(b, d // wd),
        in_specs=[pl.BlockSpec((1, s, wd), lambda bi, j: (bi, 0, j)),
                  pl.BlockSpec((1, wd, s), lambda bi, j: (bi, j, 0)),
                  pl.BlockSpec((1, s, wd), lambda bi, j: (bi, 0, j)),
                  pl.BlockSpec((1, N_HEADS, s), lambda bi, j: (bi, 0, 0)),
                  rows3(w_out_a), pl.BlockSpec((r, w_kv.shape[1]), lambda bi, j: (bi, 0)),
                  rows3(w_in_b), rows3(w_out_b)],
        out_specs=[pl.BlockSpec((1, s, wd), lambda bi, j: (bi, 0, j))] + [spec for _, spec in weights],
        out_shape=[jax.ShapeDtypeStruct((b, s, d), BF16)]
        + [jax.ShapeDtypeStruct(shape, BF16) for shape, _ in weights],
        scratch_shapes=[pltpu.VMEM((np_, BIAS_ROWS, s), BF16), pltpu.VMEM((np_, s, PAIR + LANES), BF16)],
        compiler_params=pltpu.CompilerParams(
            dimension_semantics=("parallel", "arbitrary"), vmem_limit_bytes=VMEM_LIMIT),
        name="fox_attention",
    )(q, kT, v, cT, w_out_a, w_kv, w_in_b, w_out_b)


def _layer_b_kernel(sinks_ref, o_ref, sg_ref, x_ref, wo_ref, gkv_ref, gb_ref, wkvT_ref,
                    gkn_ref, wqT_ref, wgT_ref, gqn_ref, pos_ref, invf_ref, woT_ref,
                    out_ref, kext_ref, vext_ref, qT_ref, sgT_ref, oT_ref):
    tm = x_ref.shape[1]
    w = WINDOW
    nq = GROUP * w
    first_tile = pl.program_id(1) == 0

    lane_head = lax.shift_right_logical(
        lax.broadcasted_iota(jnp.int32, (2 * w, KV_WIDTH), 1), HEAD_DIM.bit_length() - 1)
    q_head = lax.shift_right_logical(lax.broadcasted_iota(jnp.int32, (1, nq), 1), w.bit_length() - 1)
    sinks = []
    for g in range(N_KV_HEADS):
        sink = jnp.zeros((1, nq), F32)
        for i in range(GROUP):
            sink = jnp.where(q_head == i, sinks_ref[g * GROUP + i] * LOG2E, sink)
        sinks.append(sink)
    key = lax.broadcasted_iota(jnp.int32, (w, nq), 0)
    qry = jnp.bitwise_and(lax.broadcasted_iota(jnp.int32, (w, nq), 1), w - 1)
    own = key <= qry
    ones_rows = jnp.ones((SUM_ROWS, 2 * w), BF16)
    gkn = _col(gkn_ref)
    gqn = _col(gqn_ref, SCALE * LOG2E)

    parts = [slice(j * SUB_TILE, (j + 1) * SUB_TILE) for j in range(tm // SUB_TILE)]
    for rows in parts:
        og = (o_ref[0, rows, :].astype(F32) * sg_ref[0, rows, :].astype(F32)).astype(BF16)
        out_ref[0, rows, :] = x_ref[0, rows, :] + _dot(og, wo_ref[...])

    @pl.when(first_tile)
    def _():
        kext_ref[0:w, :] = jnp.zeros((w, KV_WIDTH), BF16)
        vext_ref[:, 0:w] = jnp.zeros((KV_WIDTH, w), BF16)

    @pl.when(jnp.logical_not(first_tile))
    def _():
        kext_ref[0:w, :] = kext_ref[tm:tm + w, :]
        vext_ref[:, 0:w] = vext_ref[:, tm:tm + w]

    def projection_stages(cols):
        st = {}
        ext = slice(w + cols.start, w + cols.stop)

        def k_stage():
            h = out_ref[0, cols, :]
            ms = jnp.mean(h * h, axis=-1, keepdims=True)
            hn = h * lax.rsqrt(ms + EPS)
            st["u_kv"] = (hn * gkv_ref[...]).astype(BF16)
            st["u_b"] = (hn * gb_ref[...]).astype(BF16)
            ang = invf_ref[...] * pos_ref[:, cols].astype(F32)
            st["cos"] = jnp.cos(ang)
            st["sin"] = jnp.sin(ang)
            kvT = _dot_nt(wkvT_ref[...], st["u_kv"])
            kT = _head_norm_fm(kvT[:KV_WIDTH], gkn)
            kext_ref[ext, :] = _rope_fm(kT, st["cos"], st["sin"]).astype(BF16).T
            vext_ref[:, ext] = kvT[KV_WIDTH:].astype(BF16)

        def q_stage():
            qT = _head_norm_fm(_dot_nt(wqT_ref[...], st["u_b"]), gqn)
            qT_ref[:, cols] = _rope_fm(qT, st["cos"], st["sin"]).astype(BF16)

        def gate_stage():
            gateT = _dot_nt(wgT_ref[...], st["u_b"])
            sgT_ref[:, cols] = (gateT * jax.nn.sigmoid(gateT)).astype(BF16)

        return [k_stage, q_stage, gate_stage]

    def logits(item):
        g, n = item
        k2 = kext_ref[n * w:(n + 2) * w, :]
        k_band = jnp.where(lane_head == g, k2, jnp.zeros_like(k2))
        q_rep = jnp.concatenate(
            [jnp.concatenate(
                [qT_ref[(g * GROUP + i) * HEAD_DIM:(g * GROUP + i + 1) * HEAD_DIM, n * w:(n + 1) * w]]
                * N_KV_HEADS, axis=0) for i in range(GROUP)], axis=1)
        s_all = _dot(k_band, q_rep)
        s_prev = s_all[:w]
        if n == 0:
            s_prev = jnp.where(first_tile, NEG, s_prev)
        return jnp.where(own, s_all[w:], s_prev)

    items = [(g, n) for n in range(tm // w) for g in range(N_KV_HEADS)]
    items_per_part = (SUB_TILE // w) * N_KV_HEADS
    pending = {}

    def attend(idx):
        ready_end = (idx // items_per_part + 1) * items_per_part
        if idx % items_per_part >= Q_STAGE_SLOT:
            ready_end = min(ready_end + items_per_part, len(items))
        for ahead in range(idx, min(idx + SWA_LOOKAHEAD + 1, ready_end)):
            if ahead not in pending:
                pending[ahead] = logits(items[ahead])
        g, n = items[idx]
        sT = pending.pop(idx)
        m = jnp.maximum(jnp.max(sT, axis=0, keepdims=True), sinks[g])
        p = jnp.exp2(sT - m).astype(BF16)
        zero = jnp.zeros_like(p)
        p2 = jnp.concatenate([jnp.where(own, zero, p), jnp.where(own, p, zero)], axis=0)
        v_ext = jnp.concatenate(
            [vext_ref[g * HEAD_DIM:(g + 1) * HEAD_DIM, n * w:(n + 2) * w], ones_rows], axis=0)
        out_ext = _dot(v_ext, p2)
        denom = out_ext[HEAD_DIM:HEAD_DIM + 1] + jnp.exp2(sinks[g] - m)
        outT = out_ext[:HEAD_DIM] * (1.0 / denom)
        for i in range(GROUP):
            r0 = (g * GROUP + i) * HEAD_DIM
            oT_ref[r0:r0 + HEAD_DIM, n * w:(n + 1) * w] = outT[:, i * w:(i + 1) * w].astype(BF16)

    def output_projection(cols):
        ogT = (oT_ref[:, cols].astype(F32) * sgT_ref[:, cols].astype(F32)).astype(BF16)
        half = D_MODEL // 2
        outT = jnp.concatenate(
            [_dot(woT_ref[:half, :], ogT), _dot(woT_ref[half:, :], ogT)], axis=0)
        out_ref[0, cols, :] = out_ref[0, cols, :] + outT.T

    stages = [projection_stages(cols) for cols in parts]
    for stage in stages[0]:
        stage()
    for j, cols in enumerate(parts):
        upcoming = stages[j + 1] if j + 1 < len(parts) else []
        for k in range(items_per_part):
            if k % STAGE_SPACING == 0 and k // STAGE_SPACING < len(upcoming):
                upcoming[k // STAGE_SPACING]()
            attend(j * items_per_part + k)
        output_projection(cols)


def _layer_b(sinks, o, sg, x, wo, gkv, gb, wkvT, gkn, wqT, wgT, gqn, pos, invf, woT):
    b, s, d = x.shape
    tm = ROW_TILE
    full = lambda a: pl.BlockSpec(a.shape, lambda bi, i: (0,) * a.ndim)
    tok = pl.BlockSpec((1, tm, d), lambda bi, i: (bi, i, 0))
    return pl.pallas_call(
        _layer_b_kernel,
        grid=(b, s // tm),
        in_specs=[pl.BlockSpec(memory_space=pltpu.SMEM),
                  tok, tok, tok, full(wo), full(gkv), full(gb), full(wkvT), full(gkn),
                  full(wqT), full(wgT), full(gqn),
                  pl.BlockSpec((1, tm), lambda bi, i: (0, i)), full(invf), full(woT)],
        out_specs=tok,
        out_shape=jax.ShapeDtypeStruct((b, s, d), F32),
        scratch_shapes=[pltpu.VMEM((tm + WINDOW, KV_WIDTH), BF16),
                        pltpu.VMEM((KV_WIDTH, tm + WINDOW), BF16),
                        pltpu.VMEM((d, tm), BF16),
                        pltpu.VMEM((d, tm), BF16),
                        pltpu.VMEM((d, tm), BF16)],
        compiler_params=pltpu.CompilerParams(
            dimension_semantics=("parallel", "arbitrary"), vmem_limit_bytes=VMEM_LIMIT),
        name="layer_b",
    )(sinks, o, sg, x, wo, gkv, gb, wkvT, gkn, wqT, wgT, gqn, pos, invf, woT)


def kernel(x, positions, norm_a_g, w_in_a, b_forget, qnorm_a_g, knorm_a_g, w_out_a, kv_norm_g, w_kv,
           knorm_b_g, norm_b_g, w_in_b, qnorm_b_g, sinks, w_out_b):
    b, s, d = x.shape
    wqkfT, wvg = _prep_weights_a(w_in_a[0].T)
    q, kT, v, sg, cT = _inproj_a(
        x, norm_a_g[0].reshape(1, d), wqkfT, wvg, b_forget[0].reshape(1, N_HEADS),
        qnorm_a_g[0].reshape(1, HEAD_DIM), knorm_a_g[0].reshape(1, HEAD_DIM))
    o, wo_a, wkvT, wqT, wgT, woT_b = _fox_attention(q, kT, v, cT, w_out_a, w_kv, w_in_b, w_out_b)

    inv_freq = jnp.power(jnp.float32(ROPE_THETA),
                         -jnp.arange(0, ROT_DIM, 2, dtype=F32) / ROT_DIM).reshape(ROT_HALF, 1)
    return _layer_b(
        sinks[0], o, sg, x, wo_a, kv_norm_g.reshape(1, d), norm_b_g[0].reshape(1, d),
        wkvT, knorm_b_g.reshape(1, HEAD_DIM),
        wqT, wgT, qnorm_b_g[0].reshape(1, HEAD_DIM),
        positions.reshape(1, s), inv_freq, woT_b)
```

```python
import jax
import jax.numpy as jnp
from jax import lax
from jax.experimental import pallas as pl
from jax.experimental.pallas import tpu as pltpu

D_MODEL = 1024
HEAD_DIM = 64
N_HEADS = 16
N_KV_HEADS = 4
GROUP = N_HEADS // N_KV_HEADS
KV_WIDTH = N_KV_HEADS * HEAD_DIM
WINDOW = 128
ROT_DIM = HEAD_DIM // 4
ROT_HALF = ROT_DIM // 2
ROPE_THETA = 500000.0
EPS = 1e-6
SCALE = HEAD_DIM ** -0.5
LOG2E = 1.4426950408889634
NEG = -1e30

LANES = 128
PAIR = 2 * HEAD_DIM
assert PAIR == LANES

F32 = jnp.float32
BF16 = jnp.bfloat16

PREP_ROWS = 128
ROW_TILE = 1024
SUB_TILE = 256
FOX_Q_TILE = 256
FOX_PAIRS = 2
FOX_LOOKAHEAD = 2
SWA_LOOKAHEAD = 4
STAGE_SPACING = 2
Q_STAGE_SLOT = STAGE_SPACING
SUM_ROWS = 16
VMEM_LIMIT = 56 * 1024 * 1024


def _dot(a, b):
    return jnp.dot(a, b, preferred_element_type=F32)


def _dot_nt(a, b):
    return lax.dot_general(a, b, (((1,), (1,)), ((), ())), preferred_element_type=F32)


def _col(row_ref, scale=1.0):
    row = row_ref[...]
    n = row.shape[1]
    diag = lax.broadcasted_iota(jnp.int32, (n, n), 0) == lax.broadcasted_iota(jnp.int32, (n, n), 1)
    return jnp.sum(jnp.where(diag, row, 0.0), axis=1, keepdims=True) * scale


def _head_norm_fm(t, gain_col):
    blocks = []
    for h in range(t.shape[0] // HEAD_DIM):
        blk = t[h * HEAD_DIM:(h + 1) * HEAD_DIM, :]
        ms = jnp.mean(blk * blk, axis=0, keepdims=True)
        blocks.append(blk * lax.rsqrt(ms + EPS) * gain_col)
    return jnp.concatenate(blocks, axis=0)


def _rope_fm(t, cos, sin):
    blocks = []
    for h in range(t.shape[0] // HEAD_DIM):
        base = h * HEAD_DIM
        x1 = t[base:base + ROT_HALF, :]
        x2 = t[base + ROT_HALF:base + ROT_DIM, :]
        blocks.append(x1 * cos - x2 * sin)
        blocks.append(x1 * sin + x2 * cos)
        blocks.append(t[base + ROT_DIM:base + HEAD_DIM, :])
    return jnp.concatenate(blocks, axis=0)


def _split3(x):
    hi = x.astype(BF16)
    r1 = x - hi.astype(F32)
    mid = r1.astype(BF16)
    lo = (r1 - mid.astype(F32)).astype(BF16)
    return hi, mid, lo


def _prep_kernel(waT_ref, wqkfT_ref, wvg_ref):
    d = D_MODEL
    wqkfT_ref[:2 * d, :] = waT_ref[:2 * d, :].astype(BF16)
    wqkfT_ref[2 * d:, :] = waT_ref[3 * d:3 * d + N_HEADS, :].astype(BF16)
    wvg_ref[:, :d] = waT_ref[2 * d:3 * d, :].astype(BF16).T
    wvg_ref[:, d:] = waT_ref[3 * d + N_HEADS:, :].astype(BF16).T


def _prep_weights_a(w_in_aT):
    d = D_MODEL
    r = PREP_ROWS
    return pl.pallas_call(
        _prep_kernel,
        grid=(d // r,),
        in_specs=[pl.BlockSpec((w_in_aT.shape[0], r), lambda i: (0, i))],
        out_specs=[pl.BlockSpec((2 * d + N_HEADS, r), lambda i: (0, i)),
                   pl.BlockSpec((r, 2 * d), lambda i: (i, 0))],
        out_shape=[jax.ShapeDtypeStruct((2 * d + N_HEADS, d), BF16),
                   jax.ShapeDtypeStruct((d, 2 * d), BF16)],
        compiler_params=pltpu.CompilerParams(
            dimension_semantics=("parallel",), vmem_limit_bytes=VMEM_LIMIT),
        name="prep_weights_a",
    )(w_in_aT)


def _prep_layer_b_rows(woa_ref, wkv_ref, wb_ref, wob_ref, wo_ref, wkvT_ref, wqT_ref, wgT_ref, woT_ref):
    d = D_MODEL
    wo_ref[...] = woa_ref[0].astype(BF16)
    wkvT_ref[...] = wkv_ref[...].astype(BF16).T
    wb = wb_ref[0]
    wqT_ref[...] = wb[:, :d].astype(BF16).T
    wgT_ref[...] = wb[:, d:].astype(BF16).T
    woT_ref[...] = wob_ref[0].astype(BF16).T


def _inproj_a_kernel(x_ref, g_ref, wqkfT_ref, wvg_ref, bf_ref, gq_ref, gk_ref,
                     q_ref, kT_ref, v_ref, sg_ref, cT_ref, carry_ref):
    tm = x_ref.shape[1]

    @pl.when(pl.program_id(1) == 0)
    def _():
        carry_ref[...] = jnp.zeros_like(carry_ref)

    parts = [slice(j * SUB_TILE, (j + 1) * SUB_TILE) for j in range(tm // SUB_TILE)]
    gq = _col(gq_ref, SCALE * LOG2E)
    gk = _col(gk_ref)
    b_forget = _col(bf_ref)
    u_parts, log_f_parts = [], []
    for rows in parts:
        x = x_ref[0, rows, :]
        ms = jnp.mean(x * x, axis=-1, keepdims=True)
        u32 = x * lax.rsqrt(ms + EPS) * g_ref[...]
        u_parts.append(u32.astype(BF16))
        uT = u_parts[-1].T
        qT = _head_norm_fm(_dot(wqkfT_ref[:D_MODEL, :], uT), gq)
        kfT = _dot(wqkfT_ref[D_MODEL:, :], uT)
        kT = _head_norm_fm(kfT[:D_MODEL], gk)
        q_ref[0, rows, :] = qT.astype(BF16).T
        kT_ref[0, :, rows] = kT.astype(BF16)
        f = kfT[D_MODEL:] + b_forget
        log_f_parts.append(jnp.minimum(f, 0.0) - jnp.log1p(jnp.exp(-jnp.abs(f))))
    u = jnp.concatenate(u_parts, axis=0)

    gate = _dot(u, wvg_ref[:, D_MODEL:])
    sg_ref[0] = (gate * jax.nn.sigmoid(gate)).astype(BF16)

    row = lax.broadcasted_iota(jnp.int32, (SUB_TILE, SUB_TILE), 0)
    col = lax.broadcasted_iota(jnp.int32, (SUB_TILE, SUB_TILE), 1)
    tri = (row <= col).astype(BF16)
    sums = [_dot(jnp.concatenate(_split3(log_f), axis=0), tri) for log_f in log_f_parts]
    v_ref[0] = _dot(u, wvg_ref[:, :D_MODEL]).astype(BF16)
    carry = carry_ref[:, 0:1]
    for rows, s3 in zip(parts, sums):
        c = (s3[:N_HEADS] + s3[N_HEADS:2 * N_HEADS] + s3[2 * N_HEADS:]) + carry
        cT_ref[0, :, rows] = c * LOG2E
        carry = c[:, SUB_TILE - 1:SUB_TILE]
    carry_ref[...] = jnp.broadcast_to(carry, carry_ref.shape)


def _inproj_a(x, g, wqkfT, wvg, bf, gq, gk):
    b, s, d = x.shape
    tm = ROW_TILE
    full = lambda shape: pl.BlockSpec(shape, lambda bi, i: (0,) * len(shape))
    tok = pl.BlockSpec((1, tm, d), lambda bi, i: (bi, i, 0))
    return pl.pallas_call(
        _inproj_a_kernel,
        grid=(b, s // tm),
        in_specs=[tok, full(g.shape), full(wqkfT.shape), full(wvg.shape),
                  full(bf.shape), full(gq.shape), full(gk.shape)],
        out_specs=[tok,
                   pl.BlockSpec((1, d, tm), lambda bi, i: (bi, 0, i)),
                   tok, tok,
                   pl.BlockSpec((1, N_HEADS, tm), lambda bi, i: (bi, 0, i))],
        out_shape=[jax.ShapeDtypeStruct((b, s, d), BF16),
                   jax.ShapeDtypeStruct((b, d, s), BF16),
                   jax.ShapeDtypeStruct((b, s, d), BF16),
                   jax.ShapeDtypeStruct((b, s, d), BF16),
                   jax.ShapeDtypeStruct((b, N_HEADS, s), F32)],
        scratch_shapes=[pltpu.VMEM((N_HEADS, LANES), F32)],
        compiler_params=pltpu.CompilerParams(
            dimension_semantics=("parallel", "arbitrary"), vmem_limit_bytes=VMEM_LIMIT),
        name="inproj_a",
    )(x, g, wqkfT, wvg, bf, gq, gk)


BIAS_ROWS = 16


def _fox_kernel(q_ref, kT_ref, v_ref, c_ref, woa_ref, wkv_ref, wb_ref, wob_ref,
                o_ref, wo_ref, wkvT_ref, wqT_ref, wgT_ref, woT_ref, bias_ref, vext_ref):
    @pl.when(pl.program_id(1) == 0)
    def _():
        _prep_layer_b_rows(woa_ref, wkv_ref, wb_ref, wob_ref, wo_ref, wkvT_ref, wqT_ref, wgT_ref, woT_ref)

    s_len = q_ref.shape[1]
    n_pairs = q_ref.shape[2] // PAIR
    tq = FOX_Q_TILE
    lane = lax.broadcasted_iota(jnp.int32, (tq, PAIR), 1)
    first = lane < HEAD_DIM
    ones_a = jnp.where(lane < 3, 1.0, 0.0).astype(BF16)
    ones_b = jnp.where(jnp.logical_and(lane >= 3, lane < 6), 1.0, 0.0).astype(BF16)
    row = lax.broadcasted_iota(jnp.int32, (2 * tq, tq), 0) % tq
    col = lax.broadcasted_iota(jnp.int32, (2 * tq, tq), 1)
    causal = col <= row

    brow = lax.broadcasted_iota(jnp.int32, (BIAS_ROWS, s_len), 0)
    for p in range(n_pairs):
        head = 2 * (pl.program_id(1) * n_pairs + p)
        parts = _split3(-c_ref[0, pl.ds(head, 1), :]) + _split3(-c_ref[0, pl.ds(head + 1, 1), :])
        bias = jnp.zeros((BIAS_ROWS, s_len), F32)
        for r, part in enumerate(parts):
            bias = jnp.where(brow == r, part.astype(F32), bias)
        bias_ref[p] = bias.astype(BF16)
        vext_ref[p, :, :PAIR] = v_ref[0, :, p * PAIR:(p + 1) * PAIR]
        vext_ref[p, :, PAIR:] = jnp.ones((s_len, LANES), BF16)

    def rhs(p, c0, c1):
        pad = jnp.zeros((PAIR - BIAS_ROWS, c1 - c0), BF16)
        return jnp.concatenate(
            [kT_ref[0, p * PAIR:(p + 1) * PAIR, c0:c1], bias_ref[p, :, c0:c1], pad], axis=0)

    def logits(item):
        p, qb = item
        r0 = qb * tq
        q2 = q_ref[0, r0:r0 + tq, p * PAIR:(p + 1) * PAIR]
        zero = jnp.zeros_like(q2)
        lhs = jnp.concatenate(
            [jnp.concatenate([jnp.where(first, q2, zero), ones_a], axis=1),
             jnp.concatenate([jnp.where(first, zero, q2), ones_b], axis=1)], axis=0)
        s_all = _dot(lhs, rhs(p, 0, r0 + tq))
        s_diag = jnp.where(causal, s_all[:, r0:], NEG)
        s_off = s_all[:, :r0] if qb > 0 else None
        return s_diag, s_off

    order = [(p, qb) for qb in reversed(range(s_len // tq)) for p in range(n_pairs)]
    pending = [logits(item) for item in order[:FOX_LOOKAHEAD]]
    for idx, (p, qb) in enumerate(order):
        r0 = qb * tq
        s_diag, s_off = pending.pop(0)
        if idx + FOX_LOOKAHEAD < len(order):
            pending.append(logits(order[idx + FOX_LOOKAHEAD]))
        m = jnp.max(s_diag, axis=1, keepdims=True)
        if qb > 0:
            m = jnp.maximum(m, jnp.max(s_off, axis=1, keepdims=True))
        probs = jnp.exp2((s_diag - m).astype(BF16))
        if qb > 0:
            probs = jnp.concatenate([jnp.exp2((s_off - m).astype(BF16)), probs], axis=1)
        pv = _dot(probs, vext_ref[p, :r0 + tq, :])
        o = pv[:, :PAIR] / pv[:, PAIR:]
        o_ref[0, r0:r0 + tq, p * PAIR:(p + 1) * PAIR] = jnp.where(first, o[:tq], o[tq:]).astype(BF16)


def _fox_attention(q, kT, v, cT, w_out_a, w_kv, w_in_b, w_out_b):
    b, s, d = q.shape
    np_, wd = FOX_PAIRS, FOX_PAIRS * PAIR
    r = d // b
    assert r * b == d and r % LANES == 0
    rows3 = lambda a: pl.BlockSpec((1, r, a.shape[2]), lambda bi, j: (0, bi, 0))
    same = lambda n: pl.BlockSpec((r, n), lambda bi, j: (bi, 0))
    trans = lambda m: pl.BlockSpec((m, r), lambda bi, j: (0, bi))
    weights = [((d, d), same(d)), ((2 * KV_WIDTH, d), trans(2 * KV_WIDTH)),
               ((d, d), trans(d)), ((d, d), trans(d)), ((d, d), trans(d))]
    return pl.pallas_call(
        _fox_kernel,
        grid=(b, d // wd),
        in_specs=[pl.BlockSpec((1, s, wd), lambda bi, j: (bi, 0, j)),
                  pl.BlockSpec((1, wd, s), lambda bi, j: (bi, j, 0)),
                  pl.BlockSpec((1, s, wd), lambda bi, j: (bi, 0, j)),
                  pl.BlockSpec((1, N_HEADS, s), lambda bi, j: (bi, 0, 0)),
                  rows3(w_out_a), pl.BlockSpec((r, w_kv.shape[1]), lambda bi, j: (bi, 0)),
                  rows3(w_in_b), rows3(w_out_b)],
        out_specs=[pl.BlockSpec((1, s, wd), lambda bi, j: (bi, 0, j))] + [spec for _, spec in weights],
        out_shape=[jax.ShapeDtypeStruct((b, s, d), BF16)]
        + [jax.ShapeDtypeStruct(shape, BF16) for shape, _ in weights],
        scratch_shapes=[pltpu.VMEM((np_, BIAS_ROWS, s), BF16), pltpu.VMEM((np_, s, PAIR + LANES), BF16)],
        compiler_params=pltpu.CompilerParams(
            dimension_semantics=("parallel", "arbitrary"), vmem_limit_bytes=VMEM_LIMIT),
        name="fox_attention",
    )(q, kT, v, cT, w_out_a, w_kv, w_in_b, w_out_b)


def _layer_b_kernel(sinks_ref, o_ref, sg_ref, x_ref, wo_ref, gkv_ref, gb_ref, wkvT_ref,
                    gkn_ref, wqT_ref, wgT_ref, gqn_ref, pos_ref, invf_ref, woT_ref,
                    out_ref, kext_ref, vext_ref, qT_ref, sgT_ref, oT_ref):
    tm = x_ref.shape[1]
    w = WINDOW
    nq = GROUP * w
    first_tile = pl.program_id(1) == 0

    lane_head = lax.shift_right_logical(
        lax.broadcasted_iota(jnp.int32, (2 * w, KV_WIDTH), 1), HEAD_DIM.bit_length() - 1)
    q_head = lax.shift_right_logical(lax.broadcasted_iota(jnp.int32, (1, nq), 1), w.bit_length() - 1)
    sinks = []
    for g in range(N_KV_HEADS):
        sink = jnp.zeros((1, nq), F32)
        for i in range(GROUP):
            sink = jnp.where(q_head == i, sinks_ref[g * GROUP + i] * LOG2E, sink)
        sinks.append(sink)
    key = lax.broadcasted_iota(jnp.int32, (w, nq), 0)
    qry = jnp.bitwise_and(lax.broadcasted_iota(jnp.int32, (w, nq), 1), w - 1)
    own = key <= qry
    ones_rows = jnp.ones((SUM_ROWS, 2 * w), BF16)
    gkn = _col(gkn_ref)
    gqn = _col(gqn_ref, SCALE * LOG2E)

    parts = [slice(j * SUB_TILE, (j + 1) * SUB_TILE) for j in range(tm // SUB_TILE)]
    for rows in parts:
        og = (o_ref[0, rows, :].astype(F32) * sg_ref[0, rows, :].astype(F32)).astype(BF16)
        out_ref[0, rows, :] = x_ref[0, rows, :] + _dot(og, wo_ref[...])

    @pl.when(first_tile)
    def _():
        kext_ref[0:w, :] = jnp.zeros((w, KV_WIDTH), BF16)
        vext_ref[:, 0:w] = jnp.zeros((KV_WIDTH, w), BF16)

    @pl.when(jnp.logical_not(first_tile))
    def _():
        kext_ref[0:w, :] = kext_ref[tm:tm + w, :]
        vext_ref[:, 0:w] = vext_ref[:, tm:tm + w]

    def projection_stages(cols):
        st = {}
        ext = slice(w + cols.start, w + cols.stop)

        def k_stage():
            h = out_ref[0, cols, :]
            ms = jnp.mean(h * h, axis=-1, keepdims=True)
            hn = h * lax.rsqrt(ms + EPS)
            st["u_kv"] = (hn * gkv_ref[...]).astype(BF16)
            st["u_b"] = (hn * gb_ref[...]).astype(BF16)
            ang = invf_ref[...] * pos_ref[:, cols].astype(F32)
            st["cos"] = jnp.cos(ang)
            st["sin"] = jnp.sin(ang)
            kvT = _dot_nt(wkvT_ref[...], st["u_kv"])
            kT = _head_norm_fm(kvT[:KV_WIDTH], gkn)
            kext_ref[ext, :] = _rope_fm(kT, st["cos"], st["sin"]).astype(BF16).T
            vext_ref[:, ext] = kvT[KV_WIDTH:].astype(BF16)

        def q_stage():
            qT = _head_norm_fm(_dot_nt(wqT_ref[...], st["u_b"]), gqn)
            qT_ref[:, cols] = _rope_fm(qT, st["cos"], st["sin"]).astype(BF16)

        def gate_stage():
            gateT = _dot_nt(wgT_ref[...], st["u_b"])
            sgT_ref[:, cols] = (gateT * jax.nn.sigmoid(gateT)).astype(BF16)

        return [k_stage, q_stage, gate_stage]

    def logits(item):
        g, n = item
        k2 = kext_ref[n * w:(n + 2) * w, :]
        k_band = jnp.where(lane_head == g, k2, jnp.zeros_like(k2))
        q_rep = jnp.concatenate(
            [jnp.concatenate(
                [qT_ref[(g * GROUP + i) * HEAD_DIM:(g * GROUP + i + 1) * HEAD_DIM, n * w:(n + 1) * w]]
                * N_KV_HEADS, axis=0) for i in range(GROUP)], axis=1)
        s_all = _dot(k_band, q_rep)
        s_prev = s_all[:w]
        if n == 0:
            s_prev = jnp.where(first_tile, NEG, s_prev)
        return jnp.where(own, s_all[w:], s_prev)

    items = [(g, n) for n in range(tm // w) for g in range(N_KV_HEADS)]
    items_per_part = (SUB_TILE // w) * N_KV_HEADS
    pending = {}

    def attend(idx):
        ready_end = (idx // items_per_part + 1) * items_per_part
        if idx % items_per_part >= Q_STAGE_SLOT:
            ready_end = min(ready_end + items_per_part, len(items))
        for ahead in range(idx, min(idx + SWA_LOOKAHEAD + 1, ready_end)):
            if ahead not in pending:
                pending[ahead] = logits(items[ahead])
        g, n = items[idx]
        sT = pending.pop(idx)
        m = jnp.maximum(jnp.max(sT, axis=0, keepdims=True), sinks[g])
        p = jnp.exp2((sT - m).astype(BF16))
        zero = jnp.zeros_like(p)
        p2 = jnp.concatenate([jnp.where(own, zero, p), jnp.where(own, p, zero)], axis=0)
        v_ext = jnp.concatenate(
            [vext_ref[g * HEAD_DIM:(g + 1) * HEAD_DIM, n * w:(n + 2) * w], ones_rows], axis=0)
        out_ext = _dot(v_ext, p2)
        denom = out_ext[HEAD_DIM:HEAD_DIM + 1] + jnp.exp2(sinks[g] - m)
        outT = out_ext[:HEAD_DIM] * (1.0 / denom)
        for i in range(GROUP):
            r0 = (g * GROUP + i) * HEAD_DIM
            oT_ref[r0:r0 + HEAD_DIM, n * w:(n + 1) * w] = outT[:, i * w:(i + 1) * w].astype(BF16)

    def output_projection(cols):
        ogT = (oT_ref[:, cols].astype(F32) * sgT_ref[:, cols].astype(F32)).astype(BF16)
        half = D_MODEL // 2
        outT = jnp.concatenate(
            [_dot(woT_ref[:half, :], ogT), _dot(woT_ref[half:, :], ogT)], axis=0)
        out_ref[0, cols, :] = out_ref[0, cols, :] + outT.T

    stages = [projection_stages(cols) for cols in parts]
    for stage in stages[0]:
        stage()
    for j, cols in enumerate(parts):
        upcoming = stages[j + 1] if j + 1 < len(parts) else []
        for k in range(items_per_part):
            if k % STAGE_SPACING == 0 and k // STAGE_SPACING < len(upcoming):
                upcoming[k // STAGE_SPACING]()
            attend(j * items_per_part + k)
        output_projection(cols)


def _layer_b(sinks, o, sg, x, wo, gkv, gb, wkvT, gkn, wqT, wgT, gqn, pos, invf, woT):
    b, s, d = x.shape
    tm = ROW_TILE
    full = lambda a: pl.BlockSpec(a.shape, lambda bi, i: (0,) * a.ndim)
    tok = pl.BlockSpec((1, tm, d), lambda bi, i: (bi, i, 0))
    return pl.pallas_call(
        _layer_b_kernel,
        grid=(b, s // tm),
        in_specs=[pl.BlockSpec(memory_space=pltpu.SMEM),
                  tok, tok, tok, full(wo), full(gkv), full(gb), full(wkvT), full(gkn),
                  full(wqT), full(wgT), full(gqn),
                  pl.BlockSpec((1, tm), lambda bi, i: (0, i)), full(invf), full(woT)],
        out_specs=tok,
        out_shape=jax.ShapeDtypeStruct((b, s, d), F32),
        scratch_shapes=[pltpu.VMEM((tm + WINDOW, KV_WIDTH), BF16),
                        pltpu.VMEM((KV_WIDTH, tm + WINDOW), BF16),
                        pltpu.VMEM((d, tm), BF16),
                        pltpu.VMEM((d, tm), BF16),
                        pltpu.VMEM((d, tm), BF16)],
        compiler_params=pltpu.CompilerParams(
            dimension_semantics=("parallel", "arbitrary"), vmem_limit_bytes=VMEM_LIMIT),
        name="layer_b",
    )(sinks, o, sg, x, wo, gkv, gb, wkvT, gkn, wqT, wgT, gqn, pos, invf, woT)


def kernel(x, positions, norm_a_g, w_in_a, b_forget, qnorm_a_g, knorm_a_g, w_out_a, kv_norm_g, w_kv,
           knorm_b_g, norm_b_g, w_in_b, qnorm_b_g, sinks, w_out_b):
    b, s, d = x.shape
    wqkfT, wvg = _prep_weights_a(w_in_a[0].T)
    q, kT, v, sg, cT = _inproj_a(
        x, norm_a_g[0].reshape(1, d), wqkfT, wvg, b_forget[0].reshape(1, N_HEADS),
        qnorm_a_g[0].reshape(1, HEAD_DIM), knorm_a_g[0].reshape(1, HEAD_DIM))
    o, wo_a, wkvT, wqT, wgT, woT_b = _fox_attention(q, kT, v, cT, w_out_a, w_kv, w_in_b, w_out_b)

    inv_freq = jnp.power(jnp.float32(ROPE_THETA),
                         -jnp.arange(0, ROT_DIM, 2, dtype=F32) / ROT_DIM).reshape(ROT_HALF, 1)
    return _layer_b(
        sinks[0], o, sg, x, wo_a, kv_norm_g.reshape(1, d), norm_b_g[0].reshape(1, d),
        wkvT, knorm_b_g.reshape(1, HEAD_DIM),
        wqT, wgT, qnorm_b_g[0].reshape(1, HEAD_DIM),
        positions.reshape(1, s), inv_freq, woT_b)
```

```python
import jax
import jax.numpy as jnp
from jax import lax
from jax.experimental import pallas as pl
from jax.experimental.pallas import tpu as pltpu

D_MODEL = 1024
HEAD_DIM = 64
N_HEADS = 16
N_KV_HEADS = 4
GROUP = N_HEADS // N_KV_HEADS
KV_WIDTH = N_KV_HEADS * HEAD_DIM
WINDOW = 128
ROT_DIM = HEAD_DIM // 4
ROT_HALF = ROT_DIM // 2
ROPE_THETA = 500000.0
EPS = 1e-6
SCALE = HEAD_DIM ** -0.5
LOG2E = 1.4426950408889634
NEG = -1e30

LANES = 128
PAIR = 2 * HEAD_DIM
assert PAIR == LANES

F32 = jnp.float32
BF16 = jnp.bfloat16

PREP_ROWS = 128
ROW_TILE = 1024
SUB_TILE = 256
FOX_Q_TILE = 256
FOX_PAIRS = 2
FOX_LOOKAHEAD = 2
SWA_LOOKAHEAD = 4
STAGE_SPACING = 2
Q_STAGE_SLOT = STAGE_SPACING
SUM_ROWS = 16
VMEM_LIMIT = 56 * 1024 * 1024


def _dot(a, b):
    return jnp.dot(a, b, preferred_element_type=F32)


def _dot_nt(a, b):
    return lax.dot_general(a, b, (((1,), (1,)), ((), ())), preferred_element_type=F32)


def _col(row_ref, scale=1.0):
    row = row_ref[...]
    n = row.shape[1]
    diag = lax.broadcasted_iota(jnp.int32, (n, n), 0) == lax.broadcasted_iota(jnp.int32, (n, n), 1)
    return jnp.sum(jnp.where(diag, row, 0.0), axis=1, keepdims=True) * scale


def _head_norm_fm(t, gain_col):
    blocks = []
    for h in range(t.shape[0] // HEAD_DIM):
        blk = t[h * HEAD_DIM:(h + 1) * HEAD_DIM, :]
        ms = jnp.mean(blk * blk, axis=0, keepdims=True)
        blocks.append(blk * lax.rsqrt(ms + EPS) * gain_col)
    return jnp.concatenate(blocks, axis=0)


def _rope_fm(t, cos, sin):
    blocks = []
    for h in range(t.shape[0] // HEAD_DIM):
        base = h * HEAD_DIM
        x1 = t[base:base + ROT_HALF, :]
        x2 = t[base + ROT_HALF:base + ROT_DIM, :]
        blocks.append(x1 * cos - x2 * sin)
        blocks.append(x1 * sin + x2 * cos)
        blocks.append(t[base + ROT_DIM:base + HEAD_DIM, :])
    return jnp.concatenate(blocks, axis=0)


def _split3(x):
    hi = x.astype(BF16)
    r1 = x - hi.astype(F32)
    mid = r1.astype(BF16)
    lo = (r1 - mid.astype(F32)).astype(BF16)
    return hi, mid, lo


def _prep_kernel(waT_ref, wqkfT_ref, wvg_ref):
    d = D_MODEL
    wqkfT_ref[:2 * d, :] = waT_ref[:2 * d, :].astype(BF16)
    wqkfT_ref[2 * d:, :] = waT_ref[3 * d:3 * d + N_HEADS, :].astype(BF16)
    wvg_ref[:, :d] = waT_ref[2 * d:3 * d, :].astype(BF16).T
    wvg_ref[:, d:] = waT_ref[3 * d + N_HEADS:, :].astype(BF16).T


def _prep_weights_a(w_in_aT):
    d = D_MODEL
    r = PREP_ROWS
    return pl.pallas_call(
        _prep_kernel,
        grid=(d // r,),
        in_specs=[pl.BlockSpec((w_in_aT.shape[0], r), lambda i: (0, i))],
        out_specs=[pl.BlockSpec((2 * d + N_HEADS, r), lambda i: (0, i)),
                   pl.BlockSpec((r, 2 * d), lambda i: (i, 0))],
        out_shape=[jax.ShapeDtypeStruct((2 * d + N_HEADS, d), BF16),
                   jax.ShapeDtypeStruct((d, 2 * d), BF16)],
        compiler_params=pltpu.CompilerParams(
            dimension_semantics=("parallel",), vmem_limit_bytes=VMEM_LIMIT),
        name="prep_weights_a",
    )(w_in_aT)


def _prep_layer_b_rows(woa_ref, wkv_ref, wb_ref, wob_ref, wo_ref, wkvT_ref, wqT_ref, wgT_ref, woT_ref):
    d = D_MODEL
    wo_ref[...] = woa_ref[0].astype(BF16)
    wkvT_ref[...] = wkv_ref[...].astype(BF16).T
    wb = wb_ref[0]
    wqT_ref[...] = wb[:, :d].astype(BF16).T
    wgT_ref[...] = wb[:, d:].astype(BF16).T
    woT_ref[...] = wob_ref[0].astype(BF16)


def _inproj_a_kernel(x_ref, g_ref, wqkfT_ref, wvg_ref, bf_ref, gq_ref, gk_ref,
                     q_ref, kT_ref, v_ref, sg_ref, cT_ref, carry_ref):
    tm = x_ref.shape[1]

    @pl.when(pl.program_id(1) == 0)
    def _():
        carry_ref[...] = jnp.zeros_like(carry_ref)

    parts = [slice(j * SUB_TILE, (j + 1) * SUB_TILE) for j in range(tm // SUB_TILE)]
    gq = _col(gq_ref, SCALE * LOG2E)
    gk = _col(gk_ref)
    b_forget = _col(bf_ref)
    u_parts, log_f_parts = [], []
    for rows in parts:
        x = x_ref[0, rows, :]
        ms = jnp.mean(x * x, axis=-1, keepdims=True)
        u32 = x * lax.rsqrt(ms + EPS) * g_ref[...]
        u_parts.append(u32.astype(BF16))
        uT = u_parts[-1].T
        qT = _head_norm_fm(_dot(wqkfT_ref[:D_MODEL, :], uT), gq)
        kfT = _dot(wqkfT_ref[D_MODEL:, :], uT)
        kT = _head_norm_fm(kfT[:D_MODEL], gk)
        q_ref[0, rows, :] = qT.astype(BF16).T
        kT_ref[0, :, rows] = kT.astype(BF16)
        f = kfT[D_MODEL:] + b_forget
        log_f_parts.append(jnp.minimum(f, 0.0) - jnp.log1p(jnp.exp(-jnp.abs(f))))
    u = jnp.concatenate(u_parts, axis=0)

    gate = _dot(u, wvg_ref[:, D_MODEL:])
    sg_ref[0] = (gate * jax.nn.sigmoid(gate)).astype(BF16)

    row = lax.broadcasted_iota(jnp.int32, (SUB_TILE, SUB_TILE), 0)
    col = lax.broadcasted_iota(jnp.int32, (SUB_TILE, SUB_TILE), 1)
    tri = (row <= col).astype(BF16)
    sums = [_dot(jnp.concatenate(_split3(log_f), axis=0), tri) for log_f in log_f_parts]
    v_ref[0] = _dot(u, wvg_ref[:, :D_MODEL]).astype(BF16)
    carry = carry_ref[:, 0:1]
    for rows, s3 in zip(parts, sums):
        c = (s3[:N_HEADS] + s3[N_HEADS:2 * N_HEADS] + s3[2 * N_HEADS:]) + carry
        cT_ref[0, :, rows] = c * LOG2E
        carry = c[:, SUB_TILE - 1:SUB_TILE]
    carry_ref[...] = jnp.broadcast_to(carry, carry_ref.shape)


def _inproj_a(x, g, wqkfT, wvg, bf, gq, gk):
    b, s, d = x.shape
    tm = ROW_TILE
    full = lambda shape: pl.BlockSpec(shape, lambda bi, i: (0,) * len(shape))
    tok = pl.BlockSpec((1, tm, d), lambda bi, i: (bi, i, 0))
    return pl.pallas_call(
        _inproj_a_kernel,
        grid=(b, s // tm),
        in_specs=[tok, full(g.shape), full(wqkfT.shape), full(wvg.shape),
                  full(bf.shape), full(gq.shape), full(gk.shape)],
        out_specs=[tok,
                   pl.BlockSpec((1, d, tm), lambda bi, i: (bi, 0, i)),
                   tok, tok,
                   pl.BlockSpec((1, N_HEADS, tm), lambda bi, i: (bi, 0, i))],
        out_shape=[jax.ShapeDtypeStruct((b, s, d), BF16),
                   jax.ShapeDtypeStruct((b, d, s), BF16),
                   jax.ShapeDtypeStruct((b, s, d), BF16),
                   jax.ShapeDtypeStruct((b, s, d), BF16),
                   jax.ShapeDtypeStruct((b, N_HEADS, s), F32)],
        scratch_shapes=[pltpu.VMEM((N_HEADS, LANES), F32)],
        compiler_params=pltpu.CompilerParams(
            dimension_semantics=("parallel", "arbitrary"), vmem_limit_bytes=VMEM_LIMIT),
        name="inproj_a",
    )(x, g, wqkfT, wvg, bf, gq, gk)


BIAS_ROWS = 16


def _fox_kernel(q_ref, kT_ref, v_ref, c_ref, woa_ref, wkv_ref, wb_ref, wob_ref,
                o_ref, wo_ref, wkvT_ref, wqT_ref, wgT_ref, woT_ref, bias_ref, vext_ref):
    @pl.when(pl.program_id(1) == 0)
    def _():
        _prep_layer_b_rows(woa_ref, wkv_ref, wb_ref, wob_ref, wo_ref, wkvT_ref, wqT_ref, wgT_ref, woT_ref)

    s_len = q_ref.shape[1]
    n_pairs = q_ref.shape[2] // PAIR
    tq = FOX_Q_TILE
    lane = lax.broadcasted_iota(jnp.int32, (tq, PAIR), 1)
    first = lane < HEAD_DIM
    ones_a = jnp.where(lane < 3, 1.0, 0.0).astype(BF16)
    ones_b = jnp.where(jnp.logical_and(lane >= 3, lane < 6), 1.0, 0.0).astype(BF16)
    row = lax.broadcasted_iota(jnp.int32, (2 * tq, tq), 0) % tq
    col = lax.broadcasted_iota(jnp.int32, (2 * tq, tq), 1)
    causal = col <= row

    brow = lax.broadcasted_iota(jnp.int32, (BIAS_ROWS, s_len), 0)
    for p in range(n_pairs):
        head = 2 * (pl.program_id(1) * n_pairs + p)
        parts = _split3(-c_ref[0, pl.ds(head, 1), :]) + _split3(-c_ref[0, pl.ds(head + 1, 1), :])
        bias = jnp.zeros((BIAS_ROWS, s_len), F32)
        for r, part in enumerate(parts):
            bias = jnp.where(brow == r, part.astype(F32), bias)
        bias_ref[p] = bias.astype(BF16)
        vext_ref[p, :, :PAIR] = v_ref[0, :, p * PAIR:(p + 1) * PAIR]
        vext_ref[p, :, PAIR:] = jnp.ones((s_len, LANES), BF16)

    def rhs(p, c0, c1):
        pad = jnp.zeros((PAIR - BIAS_ROWS, c1 - c0), BF16)
        return jnp.concatenate(
            [kT_ref[0, p * PAIR:(p + 1) * PAIR, c0:c1], bias_ref[p, :, c0:c1], pad], axis=0)

    def logits(item):
        p, qb = item
        r0 = qb * tq
        q2 = q_ref[0, r0:r0 + tq, p * PAIR:(p + 1) * PAIR]
        zero = jnp.zeros_like(q2)
        lhs = jnp.concatenate(
            [jnp.concatenate([jnp.where(first, q2, zero), ones_a], axis=1),
             jnp.concatenate([jnp.where(first, zero, q2), ones_b], axis=1)], axis=0)
        s_all = _dot(lhs, rhs(p, 0, r0 + tq))
        s_diag = jnp.where(causal, s_all[:, r0:], NEG)
        s_off = s_all[:, :r0] if qb > 0 else None
        return s_diag, s_off

    order = [(p, qb) for qb in reversed(range(s_len // tq)) for p in range(n_pairs)]
    pending = [logits(item) for item in order[:FOX_LOOKAHEAD]]
    for idx, (p, qb) in enumerate(order):
        r0 = qb * tq
        s_diag, s_off = pending.pop(0)
        if idx + FOX_LOOKAHEAD < len(order):
            pending.append(logits(order[idx + FOX_LOOKAHEAD]))
        m = jnp.max(s_diag, axis=1, keepdims=True)
        if qb > 0:
            m = jnp.maximum(m, jnp.max(s_off, axis=1, keepdims=True))
        probs = jnp.exp2((s_diag - m).astype(BF16))
        if qb > 0:
            probs = jnp.concatenate([jnp.exp2((s_off - m).astype(BF16)), probs], axis=1)
        pv = _dot(probs, vext_ref[p, :r0 + tq, :])
        o = pv[:, :PAIR] / pv[:, PAIR:]
        o_ref[0, r0:r0 + tq, p * PAIR:(p + 1) * PAIR] = jnp.where(first, o[:tq], o[tq:]).astype(BF16)


def _fox_attention(q, kT, v, cT, w_out_a, w_kv, w_in_b, w_out_b):
    b, s, d = q.shape
    np_, wd = FOX_PAIRS, FOX_PAIRS * PAIR
    r = d // b
    assert r * b == d and r % LANES == 0
    rows3 = lambda a: pl.BlockSpec((1, r, a.shape[2]), lambda bi, j: (0, bi, 0))
    same = lambda n: pl.BlockSpec((r, n), lambda bi, j: (bi, 0))
    trans = lambda m: pl.BlockSpec((m, r), lambda bi, j: (0, bi))
    weights = [((d, d), same(d)), ((2 * KV_WIDTH, d), trans(2 * KV_WIDTH)),
               ((d, d), trans(d)), ((d, d), trans(d)), ((d, d), same(d))]
    return pl.pallas_call(
        _fox_kernel,
        grid=(b, d // wd),
        in_specs=[pl.BlockSpec((1, s, wd), lambda bi, j: (bi, 0, j)),
                  pl.BlockSpec((1, wd, s), lambda bi, j: (bi, j, 0)),
                  pl.BlockSpec((1, s, wd), lambda bi, j: (bi, 0, j)),
                  pl.BlockSpec((1, N_HEADS, s), lambda bi, j: (bi, 0, 0)),
                  rows3(w_out_a), pl.BlockSpec((r, w_kv.shape[1]), lambda bi, j: (bi, 0)),
                  rows3(w_in_b), rows3(w_out_b)],
        out_specs=[pl.BlockSpec((1, s, wd), lambda bi, j: (bi, 0, j))] + [spec for _, spec in weights],
        out_shape=[jax.ShapeDtypeStruct((b, s, d), BF16)]
        + [jax.ShapeDtypeStruct(shape, BF16) for shape, _ in weights],
        scratch_shapes=[pltpu.VMEM((np_, BIAS_ROWS, s), BF16), pltpu.VMEM((np_, s, PAIR + LANES), BF16)],
        compiler_params=pltpu.CompilerParams(
            dimension_semantics=("parallel", "arbitrary"), vmem_limit_bytes=VMEM_LIMIT),
        name="fox_attention",
    )(q, kT, v, cT, w_out_a, w_kv, w_in_b, w_out_b)


def _layer_b_kernel(sinks_ref, o_ref, sg_ref, x_ref, wo_ref, gkv_ref, gb_ref, wkvT_ref,
                    gkn_ref, wqT_ref, wgT_ref, gqn_ref, pos_ref, invf_ref, woT_ref,
                    out_ref, kext_ref, vext_ref, qT_ref, sgT_ref, oT_ref):
    tm = x_ref.shape[1]
    w = WINDOW
    nq = GROUP * w
    first_tile = pl.program_id(1) == 0

    lane_head = lax.shift_right_logical(
        lax.broadcasted_iota(jnp.int32, (2 * w, KV_WIDTH), 1), HEAD_DIM.bit_length() - 1)
    q_head = lax.shift_right_logical(lax.broadcasted_iota(jnp.int32, (1, nq), 1), w.bit_length() - 1)
    sinks = []
    for g in range(N_KV_HEADS):
        sink = jnp.zeros((1, nq), F32)
        for i in range(GROUP):
            sink = jnp.where(q_head == i, sinks_ref[g * GROUP + i] * LOG2E, sink)
        sinks.append(sink)
    key = lax.broadcasted_iota(jnp.int32, (w, nq), 0)
    qry = jnp.bitwise_and(lax.broadcasted_iota(jnp.int32, (w, nq), 1), w - 1)
    own = key <= qry
    ones_rows = jnp.ones((SUM_ROWS, 2 * w), BF16)
    gkn = _col(gkn_ref)
    gqn = _col(gqn_ref, SCALE * LOG2E)

    parts = [slice(j * SUB_TILE, (j + 1) * SUB_TILE) for j in range(tm // SUB_TILE)]
    for rows in parts:
        og = (o_ref[0, rows, :].astype(F32) * sg_ref[0, rows, :].astype(F32)).astype(BF16)
        out_ref[0, rows, :] = x_ref[0, rows, :] + _dot(og, wo_ref[...])

    @pl.when(first_tile)
    def _():
        kext_ref[0:w, :] = jnp.zeros((w, KV_WIDTH), BF16)
        vext_ref[:, 0:w] = jnp.zeros((KV_WIDTH, w), BF16)

    @pl.when(jnp.logical_not(first_tile))
    def _():
        kext_ref[0:w, :] = kext_ref[tm:tm + w, :]
        vext_ref[:, 0:w] = vext_ref[:, tm:tm + w]

    def projection_stages(cols):
        st = {}
        ext = slice(w + cols.start, w + cols.stop)

        def k_stage():
            h = out_ref[0, cols, :]
            ms = jnp.mean(h * h, axis=-1, keepdims=True)
            hn = h * lax.rsqrt(ms + EPS)
            st["u_kv"] = (hn * gkv_ref[...]).astype(BF16)
            st["u_b"] = (hn * gb_ref[...]).astype(BF16)
            ang = invf_ref[...] * pos_ref[:, cols].astype(F32)
            st["cos"] = jnp.cos(ang)
            st["sin"] = jnp.sin(ang)
            kvT = _dot_nt(wkvT_ref[...], st["u_kv"])
            kT = _head_norm_fm(kvT[:KV_WIDTH], gkn)
            kext_ref[ext, :] = _rope_fm(kT, st["cos"], st["sin"]).astype(BF16).T
            vext_ref[:, ext] = kvT[KV_WIDTH:].astype(BF16)

        def q_stage():
            qT = _head_norm_fm(_dot_nt(wqT_ref[...], st["u_b"]), gqn)
            qT_ref[:, cols] = _rope_fm(qT, st["cos"], st["sin"]).astype(BF16)

        def gate_stage():
            gateT = _dot_nt(wgT_ref[...], st["u_b"])
            sgT_ref[:, cols] = (gateT * jax.nn.sigmoid(gateT)).astype(BF16)

        return [k_stage, q_stage, gate_stage]

    def logits(item):
        g, n = item
        k2 = kext_ref[n * w:(n + 2) * w, :]
        k_band = jnp.where(lane_head == g, k2, jnp.zeros_like(k2))
        q_rep = jnp.concatenate(
            [jnp.concatenate(
                [qT_ref[(g * GROUP + i) * HEAD_DIM:(g * GROUP + i + 1) * HEAD_DIM, n * w:(n + 1) * w]]
                * N_KV_HEADS, axis=0) for i in range(GROUP)], axis=1)
        s_all = _dot(k_band, q_rep)
        s_prev = s_all[:w]
        if n == 0:
            s_prev = jnp.where(first_tile, NEG, s_prev)
        return jnp.where(own, s_all[w:], s_prev)

    items = [(g, n) for n in range(tm // w) for g in range(N_KV_HEADS)]
    items_per_part = (SUB_TILE // w) * N_KV_HEADS
    pending = {}

    def attend(idx):
        ready_end = (idx // items_per_part + 1) * items_per_part
        if idx % items_per_part >= Q_STAGE_SLOT:
            ready_end = min(ready_end + items_per_part, len(items))
        for ahead in range(idx, min(idx + SWA_LOOKAHEAD + 1, ready_end)):
            if ahead not in pending:
                pending[ahead] = logits(items[ahead])
        g, n = items[idx]
        sT = pending.pop(idx)
        m = jnp.maximum(jnp.max(sT, axis=0, keepdims=True), sinks[g])
        p = jnp.exp2(sT - m).astype(BF16)
        zero = jnp.zeros_like(p)
        p2 = jnp.concatenate([jnp.where(own, zero, p), jnp.where(own, p, zero)], axis=0)
        v_ext = jnp.concatenate(
            [vext_ref[g * HEAD_DIM:(g + 1) * HEAD_DIM, n * w:(n + 2) * w], ones_rows], axis=0)
        out_ext = _dot(v_ext, p2)
        denom = out_ext[HEAD_DIM:HEAD_DIM + 1] + jnp.exp2(sinks[g] - m)
        outT = out_ext[:HEAD_DIM] * (1.0 / denom)
        for i in range(GROUP):
            r0 = (g * GROUP + i) * HEAD_DIM
            oT_ref[r0:r0 + HEAD_DIM, n * w:(n + 1) * w] = outT[:, i * w:(i + 1) * w].astype(BF16)

    def output_projection(cols):
        ogT = (oT_ref[:, cols].astype(F32) * sgT_ref[:, cols].astype(F32)).astype(BF16)
        out_ref[0, cols, :] = out_ref[0, cols, :] + _dot(ogT.T, woT_ref[...])

    stages = [projection_stages(cols) for cols in parts]
    for stage in stages[0]:
        stage()
    for j, cols in enumerate(parts):
        upcoming = stages[j + 1] if j + 1 < len(parts) else []
        for k in range(items_per_part):
            if k % STAGE_SPACING == 0 and k // STAGE_SPACING < len(upcoming):
                upcoming[k // STAGE_SPACING]()
            attend(j * items_per_part + k)
        output_projection(cols)


def _layer_b(sinks, o, sg, x, wo, gkv, gb, wkvT, gkn, wqT, wgT, gqn, pos, invf, woT):
    b, s, d = x.shape
    tm = ROW_TILE
    full = lambda a: pl.BlockSpec(a.shape, lambda bi, i: (0,) * a.ndim)
    tok = pl.BlockSpec((1, tm, d), lambda bi, i: (bi, i, 0))
    return pl.pallas_call(
        _layer_b_kernel,
        grid=(b, s // tm),
        in_specs=[pl.BlockSpec(memory_space=pltpu.SMEM),
                  tok, tok, tok, full(wo), full(gkv), full(gb), full(wkvT), full(gkn),
                  full(wqT), full(wgT), full(gqn),
                  pl.BlockSpec((1, tm), lambda bi, i: (0, i)), full(invf), full(woT)],
        out_specs=tok,
        out_shape=jax.ShapeDtypeStruct((b, s, d), F32),
        scratch_shapes=[pltpu.VMEM((tm + WINDOW, KV_WIDTH), BF16),
                        pltpu.VMEM((KV_WIDTH, tm + WINDOW), BF16),
                        pltpu.VMEM((d, tm), BF16),
                        pltpu.VMEM((d, tm), BF16),
                        pltpu.VMEM((d, tm), BF16)],
        compiler_params=pltpu.CompilerParams(
            dimension_semantics=("parallel", "arbitrary"), vmem_limit_bytes=VMEM_LIMIT),
        name="layer_b",
    )(sinks, o, sg, x, wo, gkv, gb, wkvT, gkn, wqT, wgT, gqn, pos, invf, woT)


def kernel(x, positions, norm_a_g, w_in_a, b_forget, qnorm_a_g, knorm_a_g, w_out_a, kv_norm_g, w_kv,
           knorm_b_g, norm_b_g, w_in_b, qnorm_b_g, sinks, w_out_b):
    b, s, d = x.shape
    wqkfT, wvg = _prep_weights_a(w_in_a[0].T)
    q, kT, v, sg, cT = _inproj_a(
        x, norm_a_g[0].reshape(1, d), wqkfT, wvg, b_forget[0].reshape(1, N_HEADS),
        qnorm_a_g[0].reshape(1, HEAD_DIM), knorm_a_g[0].reshape(1, HEAD_DIM))
    o, wo_a, wkvT, wqT, wgT, woT_b = _fox_attention(q, kT, v, cT, w_out_a, w_kv, w_in_b, w_out_b)

    inv_freq = jnp.power(jnp.float32(ROPE_THETA),
                         -jnp.arange(0, ROT_DIM, 2, dtype=F32) / ROT_DIM).reshape(ROT_HALF, 1)
    return _layer_b(
        sinks[0], o, sg, x, wo_a, kv_norm_g.reshape(1, d), norm_b_g[0].reshape(1, d),
        wkvT, knorm_b_g.reshape(1, HEAD_DIM),
        wqT, wgT, qnorm_b_g[0].reshape(1, HEAD_DIM),
        positions.reshape(1, s), inv_freq, woT_b)
```
